```python
import math
import jax
import jax.numpy as jnp
from jax import lax
import numpy as np

D_MODEL = 1024
BATCH = 16
SEQ = 2048
DEPTH = 4

GRID_W = 64
CTX_LEN = 256
N_MIXERS = 4
CTX_READING_MIXERS = (0, 2)
NORM_EPS = 1e-6

GDN_HEADS = 8
GDN_DK = D_MODEL // GDN_HEADS
GDN_DV = GDN_DK
GDN_CHUNK = 64
SHORT_CONV = 3
ROPE_BASE = 10000.0

HY_EMB_DIM = 33
HY_FILTER_WIDTH = 64
HY_DECAY_TARGET = 1e-2
HY_DECAY_SHORT_PCT = 0.3
HY_DECAY_LONG_PCT = 1.5
HY_MAX_DECAY = math.log(HY_DECAY_TARGET) / HY_DECAY_SHORT_PCT
HY_MIN_DECAY = math.log(HY_DECAY_TARGET) / HY_DECAY_LONG_PCT

NA_HEADS = 16
NA_DH = D_MODEL // NA_HEADS
NA_WIN_R = 8
NA_WIN_C = 16
NA_QB_C = 16
NA_KB_C = 32

MOE_GROUPS = 4
MOE_EXPERTS_PER_GROUP = 8
MOE_TOPK = 2
MOE_FF = D_MODEL // 4

kernel_name = "hybrid_interleaved_flow_backbone"


def _layers_of(m):
    return len(range(m, DEPTH, N_MIXERS))


def _rms_norm(x, g):
    xf = x.astype(jnp.float32)
    xf = xf * lax.rsqrt(jnp.mean(xf * xf, axis=-1, keepdims=True) + NORM_EPS)
    return (xf * g.astype(jnp.float32)).astype(x.dtype)


def _modulate(x, g, shift, scale):
    return _rms_norm(x, g) * (1 + scale[:, None, :]) + shift[:, None, :]


def _dwconv(x, w):
    k = w.shape[0]
    return lax.conv_general_dilated(
        x, w[:, None, :].astype(x.dtype), window_strides=(1,),
        padding=[((k - 1) // 2, k // 2)],
        dimension_numbers=("NWC", "WIO", "NWC"),
        feature_group_count=x.shape[-1])


def _l2norm(x):
    return x * lax.rsqrt(jnp.sum(x * x, axis=-1, keepdims=True) + NORM_EPS)


def _axial_rope(x):
    n_tok, dh = x.shape[1], x.shape[-1]
    pos = jnp.arange(n_tok)
    row = (pos // GRID_W).astype(jnp.float32)
    col = (pos % GRID_W).astype(jnp.float32)
    n_freq = dh // 4
    inv = ROPE_BASE ** (-jnp.arange(n_freq, dtype=jnp.float32) / n_freq)
    ang = jnp.concatenate([row[:, None] * inv, col[:, None] * inv], axis=-1)[None, :, None, :]
    cos, sin = jnp.cos(ang), jnp.sin(ang)
    x1, x2 = x[..., : dh // 2], x[..., dh // 2:]
    return jnp.concatenate([x1 * cos - x2 * sin, x1 * sin + x2 * cos], axis=-1)


def _gdn_chunked(q, k, v, g, beta, s0):
    bn, nh, n_tok, dk = k.shape
    dv = v.shape[-1]
    cs = GDN_CHUNK
    nc = n_tok // cs
    q = q.reshape(bn, nh, nc, cs, dk)
    k = k.reshape(bn, nh, nc, cs, dk)
    v = v.reshape(bn, nh, nc, cs, dv)
    g = jnp.cumsum(g.reshape(bn, nh, nc, cs), axis=-1)
    beta = beta.reshape(bn, nh, nc, cs)
    lower = jnp.tril(jnp.ones((cs, cs), dtype=bool))
    strict = jnp.tril(jnp.ones((cs, cs), dtype=bool), -1)
    diff = g[..., :, None] - g[..., None, :]
    decay = jnp.where(lower, jnp.exp(jnp.where(lower, diff, 0.0)), 0.0)
    kb = k * beta[..., None]
    a_mat = jnp.where(strict, jnp.einsum("bhnid,bhnjd->bhnij", kb, k) * decay, 0.0)
    rhs = jnp.concatenate([v * beta[..., None], kb * jnp.exp(g)[..., None]], axis=-1)
    sol = lax.linalg.triangular_solve(a_mat + jnp.eye(cs, dtype=a_mat.dtype), rhs,
                                      left_side=True, lower=True, unit_diagonal=True)
    u, w = sol[..., :dv], sol[..., dv:]
    attn = jnp.where(lower, jnp.einsum("bhnid,bhnjd->bhnij", q, k) * decay, 0.0)
    q_dec = q * jnp.exp(g)[..., None]
    k_dec = k * jnp.exp(g[..., -1:] - g)[..., None]
    last = jnp.exp(g[..., -1])

    def step(state, xs):
        u_i, w_i, a_i, qd_i, kd_i, l_i = xs
        v_new = u_i - jnp.einsum("bhcd,bhde->bhce", w_i, state)
        o = jnp.einsum("bhcd,bhde->bhce", qd_i, state) + jnp.einsum("bhcs,bhse->bhce", a_i, v_new)
        state = state * l_i[..., None, None] + jnp.einsum("bhcd,bhce->bhde", kd_i, v_new)
        return state, o

    xs = tuple(jnp.moveaxis(t, 2, 0) for t in (u, w, attn, q_dec, k_dec, last))
    state, o = lax.scan(step, s0, xs)
    return state, jnp.moveaxis(o, 0, 2).reshape(bn, nh, n_tok, dv)


def _gdn_mixer(h_lat, h_ctx, w_in, conv_w, a_log, dt_bias, norm_g, w_out, ctx_out):
    hd = GDN_HEADS * GDN_DK
    f32 = jnp.float32

    def prep(h, rope):
        bn, n_tok, _ = h.shape
        z = h @ w_in
        qkv = jax.nn.silu(_dwconv(z[..., : 3 * hd], conv_w)).astype(f32)
        gate = z[..., 3 * hd: 4 * hd].reshape(bn, n_tok, GDN_HEADS, GDN_DV)
        ab = z[..., 4 * hd:].astype(f32).reshape(bn, n_tok, 2, 2, GDN_HEADS)
        q, k, v = [t.reshape(bn, n_tok, GDN_HEADS, GDN_DK) for t in jnp.split(qkv, 3, axis=-1)]
        q = _l2norm(q) * (GDN_DK ** -0.5)
        k = _l2norm(k)
        if rope:
            q, k = _axial_rope(q), _axial_rope(k)
        g = -jnp.exp(a_log.astype(f32)) * jax.nn.softplus(ab[:, :, 0] + dt_bias.astype(f32))
        beta = jax.nn.sigmoid(ab[:, :, 1])
        heads = lambda t: t.transpose(0, 2, 1, 3)
        return heads(q), heads(k), heads(v), g.transpose(2, 0, 3, 1), beta.transpose(2, 0, 3, 1), gate

    ql, kl, vl, gl, bl, zl = prep(h_lat, True)
    qc, kc, vc, gc, bc, zc = prep(h_ctx, False)
    s0 = jnp.zeros((h_lat.shape[0], GDN_HEADS, GDN_DK, GDN_DV), f32)
    flip = lambda t: jnp.flip(t, axis=2)
    sc_f, oc_f = _gdn_chunked(qc, kc, vc, gc[0], bc[0], s0)
    _, ol_f = _gdn_chunked(ql, kl, vl, gl[0], bl[0], sc_f)
    sc_b, oc_b = _gdn_chunked(flip(qc), flip(kc), flip(vc), flip(gc[1]), flip(bc[1]), s0)
    _, ol_b = _gdn_chunked(flip(ql), flip(kl), flip(vl), flip(gl[1]), flip(bl[1]), sc_b)

    def out(o, z, dtype):
        bn, _, n_tok, _ = o.shape
        o = _rms_norm(o.transpose(0, 2, 1, 3), norm_g) * jax.nn.silu(z.astype(f32))
        return o.reshape(bn, n_tok, hd).astype(dtype) @ w_out

    y_lat = out(ol_f + flip(ol_b), zl, h_lat.dtype)
    y_ctx = out(oc_f + flip(oc_b), zc, h_ctx.dtype) if ctx_out else None
    return y_lat, y_ctx


def _hyena_filter(n_tok, w1, b1, w2, b2, w3, b3, freq, w4):
    f32 = jnp.float32
    t = jnp.linspace(0.0, 1.0, n_tok, dtype=f32)[:, None]
    bands = (HY_EMB_DIM - 1) // 2
    wpos = 2.0 * math.pi * jnp.arange(n_tok, dtype=f32)[:, None] / n_tok
    fr = jnp.linspace(1e-4, bands - 1, bands, dtype=f32)[None, :]
    z = jnp.concatenate([t, jnp.cos(fr * wpos), -jnp.sin(fr * wpos)], axis=-1)
    freq = freq.astype(f32)
    hdn = jnp.sin(freq[0] * (z @ w1.astype(f32) + b1.astype(f32)))
    hdn = jnp.sin(freq[1] * (hdn @ w2.astype(f32) + b2.astype(f32)))
    hdn = jnp.sin(freq[2] * (hdn @ w3.astype(f32) + b3.astype(f32)))
    filt = (hdn @ w4.astype(f32)).reshape(n_tok, 2, D_MODEL)
    deltas = jnp.abs(jnp.linspace(HY_MIN_DECAY, HY_MAX_DECAY, D_MODEL, dtype=f32))
    filt = filt * jnp.exp(-t * deltas)[:, None, :]
    h_fwd, h_bwd = filt[:, 0], filt[:, 1]
    return jnp.concatenate([h_fwd, jnp.zeros((1, D_MODEL), f32), h_bwd[:0:-1]], axis=0)


def _hyena_seq(h, w_in, short_w, short_b, f_w1, f_b1, f_w2, f_b2, f_w3, f_b3, f_freq, f_w4, f_bias, w_out):
    n_tok = h.shape[1]
    z = _dwconv(h @ w_in, short_w) + short_b
    x0, x1, v = jnp.split(z, 3, axis=-1)
    taps = _hyena_filter(n_tok, f_w1, f_b1, f_w2, f_b2, f_w3, f_b3, f_freq, f_w4)
    u = (v * x1).astype(jnp.float32)
    uf = jnp.fft.rfft(u, n=2 * n_tok, axis=1)
    kf = jnp.fft.rfft(taps, axis=0)
    y = jnp.fft.irfft(uf * kf[None], n=2 * n_tok, axis=1)[:, :n_tok] + u * f_bias.astype(jnp.float32)
    return (y.astype(h.dtype) * x0) @ w_out


def _na_mixer(h_lat, h_ctx, w_qkv, rpb, w_out, ctx_out):
    f32 = jnp.float32
    bn, n_tok, _ = h_lat.shape
    rows = n_tok // GRID_W
    wr = min(NA_WIN_R, rows)
    nh, dh = NA_HEADS, NA_DH
    scale = dh ** -0.5

    def qkv_heads(h):
        z = (h @ w_qkv).reshape(h.shape[0], h.shape[1], 3, nh, dh)
        return [z[:, :, i].transpose(0, 2, 1, 3) for i in range(3)]

    q, k, v = qkv_heads(h_lat)
    qc, kc, vc = qkv_heads(h_ctx)

    ncb = GRID_W // NA_QB_C
    qcol = np.arange(GRID_W).reshape(ncb, NA_QB_C)
    cstart = np.clip(qcol - NA_WIN_C // 2, 0, GRID_W - NA_WIN_C)
    kc0 = np.clip(np.arange(ncb) * NA_QB_C - NA_WIN_C // 2, 0, GRID_W - NA_KB_C)
    kcol = kc0[:, None] + np.arange(NA_KB_C)
    valid = (kcol[:, None, :] >= cstart[:, :, None]) & (kcol[:, None, :] < cstart[:, :, None] + NA_WIN_C)
    dc_idx = np.clip(kcol[:, None, :] - qcol[:, :, None] + NA_WIN_C - 1, 0, 2 * NA_WIN_C - 2)
    rpb_c = rpb[:, :, dc_idx]

    kg = k.reshape(bn, nh, rows, GRID_W, dh)
    vg = v.reshape(bn, nh, rows, GRID_W, dh)
    q_rows = jnp.moveaxis(q.reshape(bn, nh, rows, ncb, NA_QB_C, dh), 2, 0)
    neg = jnp.float32(-1e30)

    def row_block(args):
        r, q_r = args
        rs = jnp.clip(r - wr // 2, 0, rows - wr)
        k_blk = lax.dynamic_slice_in_dim(kg, rs, wr, axis=2)[:, :, :, kcol]
        v_blk = lax.dynamic_slice_in_dim(vg, rs, wr, axis=2)[:, :, :, kcol]
        s_loc = jnp.einsum("bhjqd,bhrjkd->bhjqrk", q_r, k_blk, preferred_element_type=f32) * scale
        dr_idx = rs + jnp.arange(wr) - r + NA_WIN_R - 1
        bias = jnp.take(rpb_c, dr_idx, axis=1).transpose(0, 2, 3, 1, 4)
        s_loc = jnp.where(valid[:, :, None, :], s_loc + bias.astype(f32), neg)
        s_ctx = jnp.einsum("bhjqd,bhcd->bhjqc", q_r, kc, preferred_element_type=f32) * scale
        s = jnp.concatenate([s_loc.reshape(bn, nh, ncb, NA_QB_C, wr * NA_KB_C), s_ctx], axis=-1)
        p = jax.nn.softmax(s, axis=-1)
        p_loc = p[..., : wr * NA_KB_C].reshape(bn, nh, ncb, NA_QB_C, wr, NA_KB_C).astype(v.dtype)
        p_ctx = p[..., wr * NA_KB_C:].astype(v.dtype)
        return (jnp.einsum("bhjqrk,bhrjkd->bhjqd", p_loc, v_blk)
                + jnp.einsum("bhjqc,bhcd->bhjqd", p_ctx, vc))

    o = lax.map(row_block, (jnp.arange(rows), q_rows))
    y_lat = o.transpose(1, 0, 3, 4, 2, 5).reshape(bn, n_tok, nh * dh) @ w_out
    y_ctx = None
    if ctx_out:
        sc = jnp.einsum("bhqd,bhkd->bhqk", qc, kc, preferred_element_type=f32) * scale
        oc = jnp.einsum("bhqk,bhkd->bhqd", jax.nn.softmax(sc, axis=-1).astype(vc.dtype), vc)
        y_ctx = oc.transpose(0, 2, 1, 3).reshape(h_ctx.shape[0], h_ctx.shape[1], nh * dh) @ w_out
    return y_lat, y_ctx


def _shortconv_seq(h, w_in, conv_w, w_out):
    u, b_gate, c_gate = jnp.split(h @ w_in, 3, axis=-1)
    return (b_gate * _dwconv(c_gate * u, conv_w)) @ w_out


def _hier_moe(h, w_gr, b_gr, w_er, b_er, w_gate, w_up, w_down):
    f32 = jnp.float32
    n_g, n_e = MOE_GROUPS, MOE_EXPERTS_PER_GROUP
    lg = (h @ w_gr).astype(f32) + b_gr.astype(f32)
    pg = jax.nn.softmax(lg, axis=-1)
    gsel = jnp.argmax(lg, axis=-1)
    wg = jnp.take_along_axis(pg, gsel[..., None], axis=-1)
    le = (h @ w_er).astype(f32).reshape(*h.shape[:-1], n_g, n_e) + b_er.astype(f32)
    le = jnp.take_along_axis(le, gsel[..., None, None], axis=-2)[..., 0, :]
    te, ti = lax.top_k(jax.nn.softmax(le, axis=-1), MOE_TOPK)
    te = te / jnp.sum(te, axis=-1, keepdims=True)
    eid = gsel[..., None] * n_e + ti
    gates = jnp.sum(jax.nn.one_hot(eid, n_g * n_e, dtype=f32) * (wg * te)[..., None], axis=-2)
    out = jnp.zeros(h.shape, f32)
    for e in range(n_g * n_e):
        hid = jax.nn.silu(h @ w_gate[e]) * (h @ w_up[e])
        out = out + (hid @ w_down[e]).astype(f32) * gates[..., e:e + 1]
    return out.astype(h.dtype)


def setup_inputs(seed: int = 0) -> dict:
    key = jax.random.key(seed)
    keys = iter(jax.random.split(key, 64))

    def nrm(shape, std):
        return jax.random.normal(next(keys), shape, jnp.float32) * std

    d = D_MODEL
    n_a, n_b, n_c, n_d = [_layers_of(m) for m in range(N_MIXERS)]
    hd_g = GDN_HEADS * GDN_DK
    hd_n = NA_HEADS * NA_DH
    n_exp = MOE_GROUPS * MOE_EXPERTS_PER_GROUP
    fw = HY_FILTER_WIDTH
    dt = jnp.exp(jax.random.uniform(next(keys), (n_a, 2, GDN_HEADS), jnp.float32,
                                    math.log(1e-3), math.log(1e-1)))
    a_log = jnp.log(jax.random.uniform(next(keys), (n_a, 2, GDN_HEADS), jnp.float32, 1.0, 16.0))
    return {
        "x": nrm((BATCH, SEQ, d), 1.0),
        "c": nrm((BATCH, d), 1.0),
        "ctx": nrm((BATCH, CTX_LEN, d), 1.0),
        "c_ctx": nrm((d,), 1.0),
        "ln_g": 1.0 + nrm((DEPTH, 2, d), 0.05),
        "w_mod": nrm((DEPTH, d, 6 * d), 0.5 * d ** -0.5),
        "b_mod": nrm((DEPTH, 6 * d), 0.02),
        "final_g": 1.0 + nrm((d,), 0.05),
        "gdn_w_in": nrm((n_a, d, 4 * hd_g + 4 * GDN_HEADS), d ** -0.5),
        "gdn_conv_w": nrm((n_a, SHORT_CONV, 3 * hd_g), SHORT_CONV ** -0.5),
        "gdn_a_log": a_log,
        "gdn_dt_bias": dt + jnp.log(-jnp.expm1(-dt)),
        "gdn_norm_g": 1.0 + nrm((n_a, GDN_DV), 0.05),
        "gdn_w_out": nrm((n_a, hd_g, d), hd_g ** -0.5),
        "hy_w_in": nrm((n_b, d, 3 * d), d ** -0.5),
        "hy_short_w": nrm((n_b, SHORT_CONV, 3 * d), SHORT_CONV ** -0.5),
        "hy_short_b": nrm((n_b, 3 * d), 0.02),
        "hy_f_w1": nrm((n_b, HY_EMB_DIM, fw), HY_EMB_DIM ** -0.5),
        "hy_f_b1": nrm((n_b, fw), 0.1),
        "hy_f_w2": nrm((n_b, fw, fw), fw ** -0.5),
        "hy_f_b2": nrm((n_b, fw), 0.1),
        "hy_f_w3": nrm((n_b, fw, fw), fw ** -0.5),
        "hy_f_b3": nrm((n_b, fw), 0.1),
        "hy_f_freq": 1.0 + nrm((n_b, 3, fw), 0.05),
        "hy_f_w4": nrm((n_b, fw, 2 * d), 0.01),
        "hy_f_bias": nrm((n_b, d), 1.0),
        "hy_w_out": nrm((n_b, d, d), d ** -0.5),
        "na_w_qkv": nrm((n_c, d, 3 * hd_n), d ** -0.5),
        "na_rpb": nrm((n_c, NA_HEADS, 2 * NA_WIN_R - 1, 2 * NA_WIN_C - 1), 0.05),
        "na_w_out": nrm((n_c, hd_n, d), hd_n ** -0.5),
        "sc_w_in": nrm((n_d, d, 3 * d), d ** -0.5),
        "sc_conv_w": nrm((n_d, SHORT_CONV, d), SHORT_CONV ** -0.5),
        "sc_w_out": nrm((n_d, d, d), d ** -0.5),
        "moe_w_gr": nrm((DEPTH, d, MOE_GROUPS), d ** -0.5),
        "moe_b_gr": nrm((DEPTH, MOE_GROUPS), 0.01),
        "moe_w_er": nrm((DEPTH, d, n_exp), d ** -0.5),
        "moe_b_er": nrm((DEPTH, MOE_GROUPS, MOE_EXPERTS_PER_GROUP), 0.01),
        "moe_w_gate": nrm((DEPTH, n_exp, d, MOE_FF), d ** -0.5),
        "moe_w_up": nrm((DEPTH, n_exp, d, MOE_FF), d ** -0.5),
        "moe_w_down": nrm((DEPTH, n_exp, MOE_FF, d), MOE_FF ** -0.5),
    }


def reference(x, c, ctx, c_ctx, ln_g, w_mod, b_mod, final_g,
              gdn_w_in, gdn_conv_w, gdn_a_log, gdn_dt_bias, gdn_norm_g, gdn_w_out,
              hy_w_in, hy_short_w, hy_short_b, hy_f_w1, hy_f_b1, hy_f_w2, hy_f_b2, hy_f_w3, hy_f_b3,
              hy_f_freq, hy_f_w4, hy_f_bias, hy_w_out,
              na_w_qkv, na_rpb, na_w_out,
              sc_w_in, sc_conv_w, sc_w_out,
              moe_w_gr, moe_b_gr, moe_w_er, moe_b_er, moe_w_gate, moe_w_up, moe_w_down):
    silu_c = jax.nn.silu(c)
    silu_cc = jax.nn.silu(c_ctx)[None, :]
    xc = ctx
    for i in range(DEPTH):
        m, s = i % N_MIXERS, i // N_MIXERS
        reads_ctx = m in CTX_READING_MIXERS
        upd_ctx = any((j % N_MIXERS) in CTX_READING_MIXERS for j in range(i + 1, DEPTH))
        sh1, sc1, g1, sh2, sc2, g2 = jnp.split(silu_c @ w_mod[i] + b_mod[i], 6, axis=-1)
        h = _modulate(x, ln_g[i, 0], sh1, sc1)
        hc = None
        if reads_ctx or upd_ctx:
            csh1, csc1, cg1, csh2, csc2, cg2 = jnp.split(silu_cc @ w_mod[i] + b_mod[i], 6, axis=-1)
            hc = _modulate(xc, ln_g[i, 0], csh1, csc1)
        if m == 0:
            y, yc = _gdn_mixer(h, hc, gdn_w_in[s], gdn_conv_w[s], gdn_a_log[s], gdn_dt_bias[s],
                               gdn_norm_g[s], gdn_w_out[s], upd_ctx)
        elif m == 1:
            hy_args = (hy_w_in[s], hy_short_w[s], hy_short_b[s], hy_f_w1[s], hy_f_b1[s], hy_f_w2[s],
                       hy_f_b2[s], hy_f_w3[s], hy_f_b3[s], hy_f_freq[s], hy_f_w4[s], hy_f_bias[s], hy_w_out[s])
            y = _hyena_seq(h, *hy_args)
            yc = _hyena_seq(hc, *hy_args) if upd_ctx else None
        elif m == 2:
            y, yc = _na_mixer(h, hc, na_w_qkv[s], na_rpb[s], na_w_out[s], upd_ctx)
        else:
            y = _shortconv_seq(h, sc_w_in[s], sc_conv_w[s], sc_w_out[s])
            yc = _shortconv_seq(hc, sc_w_in[s], sc_conv_w[s], sc_w_out[s]) if upd_ctx else None
        moe_args = (moe_w_gr[i], moe_b_gr[i], moe_w_er[i], moe_b_er[i], moe_w_gate[i], moe_w_up[i], moe_w_down[i])
        x = x + g1[:, None, :] * y
        x = x + g2[:, None, :] * _hier_moe(_modulate(x, ln_g[i, 1], sh2, sc2), *moe_args)
        if upd_ctx:
            xc = xc + cg1[:, None, :] * yc
            xc = xc + cg2[:, None, :] * _hier_moe(_modulate(xc, ln_g[i, 1], csh2, csc2), *moe_args)
    return _rms_norm(x, final_g)
```

```python
import functools
import math

import numpy as np
import jax
import jax.numpy as jnp
from jax import lax
from jax.experimental import pallas as pl
from jax.experimental.pallas import tpu as pltpu

F32 = jnp.float32
BF16 = jnp.bfloat16

D_MODEL = 1024
DEPTH = 4
GRID_W = 64
N_MIXERS = 4
CTX_READING_MIXERS = (0, 2)
NORM_EPS = 1e-6

GDN_HEADS = 8
GDN_DK = 128
GDN_CHUNK = 64
ROPE_BASE = 10000.0

HY_EMB_DIM = 33
HY_DECAY_TARGET = 1e-2
HY_MAX_DECAY = math.log(HY_DECAY_TARGET) / 0.3
HY_MIN_DECAY = math.log(HY_DECAY_TARGET) / 1.5

NA_HEADS = 16
NA_DH = 64
NA_WIN_R = 8
NA_WIN_C = 16

MOE_GROUPS = 4
MOE_EPG = 8
MOE_FF = 256
MOE_PAIRS = MOE_EPG * (MOE_EPG - 1) // 2
MOE_CLASSES = MOE_GROUPS * MOE_PAIRS
MOE_TM = 128

LANES = 128
VMEM_LIMIT = 56 << 20


def _cparams(sem, vmem=VMEM_LIMIT):
    return pltpu.CompilerParams(dimension_semantics=sem, vmem_limit_bytes=vmem)


def _rms_mod(x, m):
    ms = jnp.mean(x * x, axis=-1, keepdims=True)
    xn = (x * lax.rsqrt(ms + NORM_EPS)) * m[0:1]
    return xn * (1.0 + m[2:3]) + m[1:2]


def _split_bf16(x):
    hi = x.astype(BF16)
    lo = (x - hi.astype(F32)).astype(BF16)
    return hi, lo


def _dot(a, b):
    return jnp.dot(a, b, preferred_element_type=F32)


def _dot_nt(a, b):
    return lax.dot_general(a, b, (((1,), (1,)), ((), ())), preferred_element_type=F32)


def _bdot(a, b):
    return lax.dot_general(a, b, (((2,), (1,)), ((0,), (0,))), preferred_element_type=F32)


def _bdot_nt(a, b):
    return lax.dot_general(a, b, (((2,), (2,)), ((0,), (0,))), preferred_element_type=F32)


def _bdot_tn(a, b):
    return lax.dot_general(a, b, (((1,), (1,)), ((0,), (0,))), preferred_element_type=F32)


def _bdot3(a, b):
    ah, al = _split_bf16(a)
    bh, bl = _split_bf16(b)
    return _bdot(ah, bh) + (_bdot(ah, bl) + _bdot(al, bh))


def _silu(x):
    return x * (1.0 / (1.0 + jnp.exp(-x)))


def _modvec_kernel(c_ref, w_ref, b_ref, o_ref):
    c = c_ref[...]
    a = _silu(c)
    ah, al = _split_bf16(a)
    wh, wl = _split_bf16(w_ref[0])
    o_ref[0] = _dot(ah, wh) + (_dot(ah, wl) + _dot(al, wh)) + b_ref[0]


def _mod_vectors(c_all, w_mod, b_mod):
    rows, d = c_all.shape
    depth, _, n = w_mod.shape
    tn = 1536
    return pl.pallas_call(
        _modvec_kernel,
        out_shape=jax.ShapeDtypeStruct((depth, rows, n), F32),
        grid=(depth, n // tn),
        in_specs=[pl.BlockSpec((rows, d), lambda i, j: (0, 0)),
                  pl.BlockSpec((1, d, tn), lambda i, j: (i, 0, j)),
                  pl.BlockSpec((1, 1, tn), lambda i, j: (i, 0, j))],
        out_specs=pl.BlockSpec((1, rows, tn), lambda i, j: (i, 0, j)),
        compiler_params=_cparams(("arbitrary", "arbitrary")),
        name="mod_vectors",
    )(c_all, w_mod, b_mod.reshape(depth, 1, n))


def _modmm_kernel(x_ref, m_ref, w_ref, o_ref, *, ncol):
    h = _rms_mod(x_ref[...], m_ref[0]).astype(BF16)
    n = w_ref.shape[1]
    for n0 in range(0, n, ncol):
        o_ref[:, n0:n0 + ncol] = _dot(h, w_ref[:, n0:n0 + ncol]).astype(o_ref.dtype)


def _modmm(x, mod3, w, rows_per_seg, out_dtype=F32, tl=512):
    r, d = x.shape
    n = w.shape[1]
    tps = rows_per_seg // tl
    ncol = 512 if n % 512 == 0 else n
    return pl.pallas_call(
        functools.partial(_modmm_kernel, ncol=ncol),
        out_shape=jax.ShapeDtypeStruct((r, n), out_dtype),
        grid=(r // tl,),
        in_specs=[pl.BlockSpec((tl, d), lambda i: (i, 0)),
                  pl.BlockSpec((1, 3, d), lambda i: (i // tps, 0, 0)),
                  pl.BlockSpec((d, n), lambda i: (0, 0))],
        out_specs=pl.BlockSpec((tl, n), lambda i: (i, 0)),
        compiler_params=_cparams(("arbitrary",)),
        name="mod_matmul",
    )(x, mod3, w)


def _mmres_kernel(a_ref, w_ref, x_ref, g_ref, o_ref):
    o_ref[...] = x_ref[...] + g_ref[0] * _dot(a_ref[...], w_ref[...])


def _mm_res(a, w, x, gate, rows_per_seg, tl=512):
    r, k = a.shape
    d = w.shape[1]
    tps = rows_per_seg // tl
    return pl.pallas_call(
        _mmres_kernel,
        out_shape=jax.ShapeDtypeStruct((r, d), F32),
        grid=(r // tl,),
        in_specs=[pl.BlockSpec((tl, k), lambda i: (i, 0)),
                  pl.BlockSpec((k, d), lambda i: (0, 0)),
                  pl.BlockSpec((tl, d), lambda i: (i, 0)),
                  pl.BlockSpec((1, 1, d), lambda i: (i // tps, 0, 0))],
        out_specs=pl.BlockSpec((tl, d), lambda i: (i, 0)),
        compiler_params=_cparams(("arbitrary",)),
        name="matmul_residual",
    )(a, w, x, gate)


def _router_kernel(x_ref, m_ref, wh_ref, wl_ref, b_ref, h_ref, route_ref, cnt_ref, carry_ref):
    i = pl.program_id(0)

    @pl.when(i == 0)
    def _():
        carry_ref[...] = jnp.zeros_like(carry_ref)

    h = _rms_mod(x_ref[...], m_ref[0])
    hh, hl = _split_bf16(h)
    h_ref[...] = hh
    wh = wh_ref[...]
    logits = _dot(hh, wh) + (_dot(hl, wh) + _dot(hh, wl_ref[...])) + b_ref[...]
    t = logits.shape[0]
    lane = lax.broadcasted_iota(jnp.int32, logits.shape, 1).astype(F32)
    neg = -jnp.inf
    big = 1e9

    def first_argmax(vals):
        m = jnp.max(vals, axis=1, keepdims=True)
        idx = jnp.min(jnp.where(vals == m, lane, big), axis=1, keepdims=True)
        return m, idx

    lg = jnp.where(lane < MOE_GROUPS, logits, neg)
    mg, gsel = first_argmax(lg)
    wg = 1.0 / jnp.sum(jnp.exp(lg - mg), axis=1, keepdims=True)
    e_first = MOE_GROUPS + gsel * MOE_EPG
    in_grp = (lane >= e_first) & (lane < e_first + MOE_EPG)
    le = jnp.where(in_grp, logits, neg)
    m1, i1 = first_argmax(le)
    m2, i2 = first_argmax(jnp.where(lane == i1, neg, le))
    p = jnp.exp(m2 - m1)
    w1 = wg / (1.0 + p)
    w2 = wg * p / (1.0 + p)
    l1 = i1 - e_first
    l2 = i2 - e_first
    lo = jnp.minimum(l1, l2)
    hi = jnp.maximum(l1, l2)
    first_is_lo = l1 < l2
    wlo = jnp.where(first_is_lo, w1, w2)
    whi = jnp.where(first_is_lo, w2, w1)
    cls = gsel * MOE_PAIRS + lo * (2 * MOE_EPG - 1 - lo) * 0.5 + (hi - lo - 1.0)

    onehot = lane == cls
    ri = lax.broadcasted_iota(jnp.int32, (t, t), 0)
    ci = lax.broadcasted_iota(jnp.int32, (t, t), 1)
    before = (ci < ri).astype(BF16)
    prefix = _dot(before, onehot.astype(BF16)) + carry_ref[...]
    rank = jnp.sum(jnp.where(onehot, prefix, 0.0), axis=1, keepdims=True)
    carry = carry_ref[...] + jnp.sum(onehot.astype(F32), axis=0, keepdims=True)
    carry_ref[...] = carry
    cnt_ref[...] = carry
    route_ref[...] = jnp.where(lane == 0, cls, jnp.where(lane == 1, rank, jnp.where(
        lane == 2, wlo, jnp.where(lane == 3, whi, 0.0))))


def _router(x, mod3, w_r, b_r, rows_per_seg, tl=512):
    r, d = x.shape
    tps = rows_per_seg // tl
    wh, wl = _split_bf16(w_r)
    return pl.pallas_call(
        _router_kernel,
        out_shape=(jax.ShapeDtypeStruct((r, d), BF16),
                   jax.ShapeDtypeStruct((r, LANES), F32),
                   jax.ShapeDtypeStruct((1, LANES), F32)),
        grid=(r // tl,),
        in_specs=[pl.BlockSpec((tl, d), lambda i: (i, 0)),
                  pl.BlockSpec((1, 3, d), lambda i: (i // tps, 0, 0)),
                  pl.BlockSpec((d, LANES), lambda i: (0, 0)),
                  pl.BlockSpec((d, LANES), lambda i: (0, 0)),
                  pl.BlockSpec((1, LANES), lambda i: (0, 0))],
        out_specs=(pl.BlockSpec((tl, d), lambda i: (i, 0)),
                   pl.BlockSpec((tl, LANES), lambda i: (i, 0)),
                   pl.BlockSpec((1, LANES), lambda i: (0, 0))),
        scratch_shapes=[pltpu.VMEM((1, LANES), F32)],
        compiler_params=_cparams(("arbitrary",)),
        name="moe_router",
    )(x, mod3, wh, wl, b_r)


def _moe_kernel(elo_ref, ehi_ref, val_ref, x_ref, wt_ref, wgl, wul, wdl, wgh, wuh, wdh, o_ref):
    t = pl.program_id(0)

    @pl.when(val_ref[t] != 0)
    def _():
        x = x_ref[...]
        wt = wt_ref[...]
        hl = (_silu(_dot(x, wgl[0])) * _dot(x, wul[0])) * wt[:, 0:1]
        hh = (_silu(_dot(x, wgh[0])) * _dot(x, wuh[0])) * wt[:, 1:2]
        o_ref[...] = _dot(hl.astype(BF16), wdl[0]) + _dot(hh.astype(BF16), wdh[0])

    @pl.when(val_ref[t] == 0)
    def _():
        o_ref[...] = jnp.zeros_like(o_ref)


def _moe_experts(xs, wts, tile_elo, tile_ehi, tile_val, w_gate, w_up, w_down):
    p, d = xs.shape
    ff = w_gate.shape[2]
    nt = p // MOE_TM
    lo = lambda t, elo, ehi, val: (elo[t], 0, 0)
    hi = lambda t, elo, ehi, val: (ehi[t], 0, 0)
    row = lambda t, elo, ehi, val: (t, 0)
    grid_spec = pltpu.PrefetchScalarGridSpec(
        num_scalar_prefetch=3,
        grid=(nt,),
        in_specs=[pl.BlockSpec((MOE_TM, d), row),
                  pl.BlockSpec((MOE_TM, 2), row),
                  pl.BlockSpec((1, d, ff), lo), pl.BlockSpec((1, d, ff), lo), pl.BlockSpec((1, ff, d), lo),
                  pl.BlockSpec((1, d, ff), hi), pl.BlockSpec((1, d, ff), hi), pl.BlockSpec((1, ff, d), hi)],
        out_specs=pl.BlockSpec((MOE_TM, d), row),
    )
    return pl.pallas_call(
        _moe_kernel,
        out_shape=jax.ShapeDtypeStruct((p, d), F32),
        grid_spec=grid_spec,
        compiler_params=_cparams(("arbitrary",)),
        name="moe_experts",
    )(tile_elo, tile_ehi, tile_val, xs, wts, w_gate, w_up, w_down, w_gate, w_up, w_down)


def _class_tables():
    elo = np.zeros((MOE_CLASSES,), np.int32)
    ehi = np.zeros((MOE_CLASSES,), np.int32)
    for g in range(MOE_GROUPS):
        for lo in range(MOE_EPG):
            for hi in range(lo + 1, MOE_EPG):
                c = g * MOE_PAIRS + lo * (2 * MOE_EPG - 1 - lo) // 2 + (hi - lo - 1)
                elo[c] = g * MOE_EPG + lo
                ehi[c] = g * MOE_EPG + hi
    return elo, ehi


_CLASS_ELO, _CLASS_EHI = _class_tables()


def _hier_moe_residual(x, mod3, gate, rows_per_seg, w_gr, b_gr, w_er, b_er, w_gate, w_up, w_down):
    r, d = x.shape
    w_r = jnp.zeros((d, LANES), F32).at[:, :MOE_GROUPS].set(w_gr)
    w_r = w_r.at[:, MOE_GROUPS:MOE_GROUPS + MOE_GROUPS * MOE_EPG].set(w_er)
    b_r = jnp.zeros((1, LANES), F32).at[0, :MOE_GROUPS].set(b_gr)
    b_r = b_r.at[0, MOE_GROUPS:MOE_GROUPS + MOE_GROUPS * MOE_EPG].set(b_er.reshape(-1))
    h, route, cnt = _router(x, mod3, w_r, b_r, rows_per_seg)

    cls = route[:, 0].astype(jnp.int32)
    rank = route[:, 1].astype(jnp.int32)
    counts = cnt[0, :MOE_CLASSES].astype(jnp.int32)
    pc = ((counts + MOE_TM - 1) // MOE_TM) * MOE_TM
    pend = jnp.cumsum(pc)
    pstart = pend - pc
    dest = pstart[cls] + rank
    p_rows = ((r + MOE_CLASSES * (MOE_TM - 1)) // MOE_TM + 1) * MOE_TM
    nt = p_rows // MOE_TM
    vals = jnp.stack([jnp.arange(r, dtype=F32), route[:, 2], route[:, 3]], axis=1)
    tab = jnp.zeros((p_rows, 3), F32).at[dest].set(vals)
    src = tab[:, 0].astype(jnp.int32)
    wts = tab[:, 1:3]
    tile_p0 = jnp.arange(nt, dtype=jnp.int32) * MOE_TM
    tile_val = (tile_p0 < pend[-1]).astype(jnp.int32)
    last_cls = jnp.searchsorted(pend, pend[-1] - 1, side="right")
    tile_cls = jnp.where(tile_val != 0, jnp.searchsorted(pend, tile_p0, side="right"), last_cls)
    tile_cls = jnp.clip(tile_cls, 0, MOE_CLASSES - 1).astype(jnp.int32)
    tile_elo = jnp.asarray(_CLASS_ELO)[tile_cls]
    tile_ehi = jnp.asarray(_CLASS_EHI)[tile_cls]

    xs = jnp.take(h, src, axis=0)
    ys = _moe_experts(xs, wts, tile_elo, tile_ehi, tile_val,
                      w_gate.astype(BF16), w_up.astype(BF16), w_down.astype(BF16))
    y = jnp.take(ys, dest, axis=0)
    segs = gate.shape[0]
    return (x.reshape(segs, rows_per_seg, d) + gate * y.reshape(segs, rows_per_seg, d)).reshape(r, d)


def _gdn_kernel(q_ref, k_ref, v_ref, gc_ref, be_ref, gt_ref, o_ref,
                s_ref, u_s, w_s, at_s, qd_s, kd_s, *, nb):
    sb, ltot, dk = q_ref.shape
    cs = GDN_CHUNK
    nc = ltot // cs
    b2 = sb * nb
    s_ref[...] = jnp.zeros_like(s_ref)
    ii = lax.broadcasted_iota(jnp.int32, (cs, cs), 0)
    jj = lax.broadcasted_iota(jnp.int32, (cs, cs), 1)
    eye = (ii == jj)[None]
    lower = (ii >= jj)[None]
    strict = (ii > jj)[None]
    eye_f = eye.astype(F32)

    def col(rowv):
        return jnp.sum(jnp.where(eye, rowv, 0.0), axis=2, keepdims=True)

    def window(wi, carry):
        t0 = pl.multiple_of(wi * (nb * cs), nb * cs)
        c0 = wi * nb
        q = q_ref[:, pl.ds(t0, nb * cs), :].reshape(b2, cs, dk)
        k = k_ref[:, pl.ds(t0, nb * cs), :].reshape(b2, cs, dk)
        v = v_ref[:, pl.ds(t0, nb * cs), :].reshape(b2, cs, dk)
        gc = gc_ref[:, pl.ds(c0, nb)].reshape(b2, 1, cs)
        be = be_ref[:, pl.ds(c0, nb)].reshape(b2, 1, cs)
        gt = gt_ref[:, pl.ds(c0, nb)].reshape(b2, 1, LANES)[:, :, :cs]

        diff = col(gc) - gc
        decay = jnp.where(lower, jnp.exp(jnp.where(lower, diff, 0.0)), 0.0)
        kk = _bdot_nt(k, k)
        a = jnp.where(strict, kk * col(be) * decay, 0.0)
        x = _bdot3(a, a)
        p = eye_f - a
        p = p + _bdot3(p, x)
        for _ in range(4):
            x = _bdot3(x, x)
            p = p + _bdot3(p, x)
        eg = jnp.exp(gc)
        u = _bdot((p * be).astype(BF16), v)
        w = _bdot((p * (be * eg)).astype(BF16), k)
        attn = jnp.where(lower, _bdot_nt(q, k) * decay, 0.0)
        qd = q.astype(F32) * col(eg)
        kd = k.astype(F32) * col(jnp.exp(gt - gc))
        u_s[...] = u.reshape(sb, nb, cs, dk)
        w_s[...] = w.astype(BF16).reshape(sb, nb, cs, dk)
        at_s[...] = attn.astype(BF16).reshape(sb, nb, cs, cs)
        qd_s[...] = qd.astype(BF16).reshape(sb, nb, cs, dk)
        kd_s[...] = kd.astype(BF16).reshape(sb, nb, cs, dk)

        for c in range(nb):
            s = s_ref[...]
            sbf = s.astype(BF16)
            v_new = u_s[:, c] - _bdot(w_s[:, c], sbf)
            vb = v_new.astype(BF16)
            o = _bdot(qd_s[:, c], sbf) + _bdot(at_s[:, c], vb)
            last = jnp.exp(gt_ref[:, pl.ds(c0 + c, 1)].reshape(sb, 1, LANES))
            s_ref[...] = s * last + _bdot_tn(kd_s[:, c], vb)
            o_ref[:, pl.ds(t0 + c * cs, cs), :] = o.astype(o_ref.dtype)
        return carry

    lax.fori_loop(0, nc // nb, window, 0)


def _gdn_scan(q, k, v, gc, be, gt, sb=8, nb=6):
    s, ltot, dk = q.shape
    nc = ltot // GDN_CHUNK
    seq = pl.BlockSpec((sb, ltot, dk), lambda i: (i, 0, 0))
    r64 = pl.BlockSpec((sb, nc, 1, GDN_CHUNK), lambda i: (i, 0, 0, 0))
    r128 = pl.BlockSpec((sb, nc, 1, LANES), lambda i: (i, 0, 0, 0))
    cs = GDN_CHUNK
    return pl.pallas_call(
        functools.partial(_gdn_kernel, nb=nb),
        out_shape=jax.ShapeDtypeStruct((s, ltot, dk), BF16),
        grid=(s // sb,),
        in_specs=[seq, seq, seq, r64, r64, r128],
        out_specs=seq,
        scratch_shapes=[pltpu.VMEM((sb, dk, dk), F32),
                        pltpu.VMEM((sb, nb, cs, dk), F32),
                        pltpu.VMEM((sb, nb, cs, dk), BF16),
                        pltpu.VMEM((sb, nb, cs, cs), BF16),
                        pltpu.VMEM((sb, nb, cs, dk), BF16),
                        pltpu.VMEM((sb, nb, cs, dk), BF16)],
        compiler_params=_cparams(("arbitrary",)),
        name="gdn_scan",
    )(q, k, v, gc, be, gt)


def _hyena_kernel(u_ref, x0_ref, c_ref, s_ref, kr_ref, ki_ref, kn_ref, fb_ref, o_ref):
    ub = u_ref[0]
    l = ub.shape[0]
    cm = c_ref[...]
    sm = s_ref[...]
    a = _dot(cm, ub)
    b = _dot(sm, ub)
    kr = kr_ref[...]
    ki = ki_ref[...]
    zr = (a * kr + b * ki).astype(BF16)
    zi = (b * kr - a * ki).astype(BF16)
    y = _dot(cm, zr) + _dot(sm, zi)
    uf = ub.astype(F32)
    tpar = lax.broadcasted_iota(jnp.int32, (l, 1), 0) & 1
    alt = (1 - 2 * tpar).astype(F32)
    un = jnp.sum(uf * alt, axis=0, keepdims=True)
    y = y + alt * (un * kn_ref[...]) + uf * fb_ref[...]
    o_ref[0] = (y * x0_ref[0].astype(F32)).astype(o_ref.dtype)


def _hyena_conv(u, x0, cmat, smat, kr, ki, kn, fbias, td=256):
    b, l, d = u.shape
    seq = pl.BlockSpec((1, l, td), lambda i, j: (i, 0, j))
    mat = pl.BlockSpec((l, l), lambda i, j: (0, 0), pipeline_mode=pl.Buffered(1))
    spec = pl.BlockSpec((l, td), lambda i, j: (0, j))
    vec = pl.BlockSpec((1, td), lambda i, j: (0, j))
    return pl.pallas_call(
        _hyena_kernel,
        out_shape=jax.ShapeDtypeStruct((b, l, d), BF16),
        grid=(b, d // td),
        in_specs=[seq, seq, mat, mat, spec, spec, vec, vec],
        out_specs=seq,
        compiler_params=_cparams(("arbitrary", "arbitrary")),
        name="hyena_conv",
    )(u, x0, cmat, smat, kr, ki, kn, fbias)


def _mm3_kernel(a_ref, b_ref, o_ref):
    ah, al = _split_bf16(a_ref[...])
    bh, bl = _split_bf16(b_ref[...])
    o_ref[...] = _dot(ah, bh) + (_dot(ah, bl) + _dot(al, bh))


def _mm3(a, b, tm=256, tn=256):
    m, k = a.shape
    n = b.shape[1]
    tm = min(tm, m)
    return pl.pallas_call(
        _mm3_kernel,
        out_shape=jax.ShapeDtypeStruct((m, n), F32),
        grid=(n // tn, m // tm),
        in_specs=[pl.BlockSpec((tm, k), lambda j, i: (i, 0)),
                  pl.BlockSpec((k, tn), lambda j, i: (0, j))],
        out_specs=pl.BlockSpec((tm, tn), lambda j, i: (i, j)),
        compiler_params=_cparams(("arbitrary", "arbitrary")),
        name="matmul_f32x3",
    )(a, b)


def _na_kernel(q_ref, k_ref, v_ref, kc_ref, vc_ref, bias_ref, o_ref, *, rows):
    r = pl.program_id(1)
    rs = jnp.clip(r - NA_WIN_R // 2, 0, rows - NA_WIN_R)
    t0 = pl.multiple_of(rs * GRID_W, GRID_W)
    nk = NA_WIN_R * GRID_W
    scale = NA_DH ** -0.5
    lane = lax.broadcasted_iota(jnp.int32, (GRID_W, LANES), 1)
    first = lane < NA_DH
    for hp in range(NA_HEADS // 2):
        cols = slice(hp * LANES, (hp + 1) * LANES)
        q2 = q_ref[0, :, cols]
        k2 = k_ref[0, pl.ds(t0, nk), cols]
        v2 = v_ref[0, pl.ds(t0, nk), cols]
        kc2 = kc_ref[0, :, cols]
        vc2 = vc_ref[0, :, cols]
        outs = []
        for half in range(2):
            qm = jnp.where(first if half == 0 else jnp.logical_not(first), q2, jnp.zeros_like(q2))
            s_loc = _dot_nt(qm, k2) * scale + bias_ref[0, 2 * hp + half]
            s_ctx = _dot_nt(qm, kc2) * scale
            m = jnp.maximum(jnp.max(s_loc, axis=1, keepdims=True), jnp.max(s_ctx, axis=1, keepdims=True))
            p_loc = jnp.exp(s_loc - m)
            p_ctx = jnp.exp(s_ctx - m)
            den = jnp.sum(p_loc, axis=1, keepdims=True) + jnp.sum(p_ctx, axis=1, keepdims=True)
            o = _dot(p_loc.astype(BF16), v2) + _dot(p_ctx.astype(BF16), vc2)
            outs.append(o / den)
        o_ref[0, :, cols] = jnp.where(first, outs[0], outs[1]).astype(o_ref.dtype)


def _na_attention(q, k, v, kc, vc, bias_tab):
    b, l, hd = q.shape
    lc = kc.shape[1]
    rows = l // GRID_W
    half = NA_WIN_R // 2
    return pl.pallas_call(
        functools.partial(_na_kernel, rows=rows),
        out_shape=jax.ShapeDtypeStruct((b, l, hd), BF16),
        grid=(b, rows),
        in_specs=[pl.BlockSpec((1, GRID_W, hd), lambda i, r: (i, r, 0)),
                  pl.BlockSpec((1, l, hd), lambda i, r: (i, 0, 0)),
                  pl.BlockSpec((1, l, hd), lambda i, r: (i, 0, 0)),
                  pl.BlockSpec((1, lc, hd), lambda i, r: (i, 0, 0)),
                  pl.BlockSpec((1, lc, hd), lambda i, r: (i, 0, 0)),
                  pl.BlockSpec((1, NA_HEADS, GRID_W, NA_WIN_R * GRID_W),
                               lambda i, r: (r - jnp.clip(r - half, 0, rows - NA_WIN_R), 0, 0, 0))],
        out_specs=pl.BlockSpec((1, GRID_W, hd), lambda i, r: (i, r, 0)),
        compiler_params=_cparams(("arbitrary", "arbitrary")),
        name="na_attention",
    )(q, k, v, kc, vc, bias_tab)


def _na_bias_table(rpb):
    qc = np.arange(GRID_W)
    kc = np.arange(GRID_W)
    cstart = np.clip(qc - NA_WIN_C // 2, 0, GRID_W - NA_WIN_C)
    valid = (kc[None, :] >= cstart[:, None]) & (kc[None, :] < cstart[:, None] + NA_WIN_C)
    dc = np.clip(kc[None, :] - qc[:, None] + NA_WIN_C - 1, 0, 2 * NA_WIN_C - 2)
    off = np.arange(NA_WIN_R)
    j = np.arange(NA_WIN_R)
    dr = j[None, :] - off[:, None] + NA_WIN_R - 1
    t = rpb[:, dr[:, :, None, None], dc[None, None, :, :]]
    t = jnp.where(jnp.asarray(valid)[None, None, None], t.astype(F32), jnp.float32(-1e30))
    t = t.transpose(1, 0, 3, 2, 4)
    return t.reshape(NA_WIN_R, NA_HEADS, GRID_W, NA_WIN_R * GRID_W)


def _dwconv(x, w):
    k = w.shape[0]
    return lax.conv_general_dilated(
        x, w[:, None, :].astype(x.dtype), window_strides=(1,),
        padding=[((k - 1) // 2, k // 2)],
        dimension_numbers=("NWC", "WIO", "NWC"),
        feature_group_count=x.shape[-1])


def _l2norm(x):
    return x * lax.rsqrt(jnp.sum(x * x, axis=-1, keepdims=True) + NORM_EPS)


def _axial_rope(x):
    n_tok, dh = x.shape[1], x.shape[-1]
    pos = jnp.arange(n_tok)
    row = (pos // GRID_W).astype(F32)
    col = (pos % GRID_W).astype(F32)
    n_freq = dh // 4
    inv = ROPE_BASE ** (-jnp.arange(n_freq, dtype=F32) / n_freq)
    ang = jnp.concatenate([row[:, None] * inv, col[:, None] * inv], axis=-1)[None, :, None, :]
    cos, sin = jnp.cos(ang), jnp.sin(ang)
    x1, x2 = x[..., : dh // 2], x[..., dh // 2:]
    return jnp.concatenate([x1 * cos - x2 * sin, x1 * sin + x2 * cos], axis=-1)


def _gdn_layer(x, xc, mod_l, mod_c, gate_l, gate_c, bn, w_in, conv_w, a_log, dt_bias, norm_g, w_out, ctx_out):
    d = D_MODEL
    hd = GDN_HEADS * GDN_DK
    n_lat = x.shape[0] // bn
    n_ctx = xc.shape[0] // bn
    w_main = w_in[:, :4 * hd].astype(BF16)
    w_ab = jnp.zeros((d, LANES), F32).at[:, :4 * GDN_HEADS].set(w_in[:, 4 * hd:]).astype(BF16)

    def prep(rows, mod3, rows_per_seg, n_tok, rope):
        z = _modmm(rows, mod3, w_main, rows_per_seg).reshape(bn, n_tok, 4 * hd)
        ab = _modmm(rows, mod3, w_ab, rows_per_seg)[:, :4 * GDN_HEADS].reshape(bn, n_tok, 2, 2, GDN_HEADS)
        qkv = jax.nn.silu(_dwconv(z[..., :3 * hd], conv_w))
        gate = z[..., 3 * hd:].reshape(bn, n_tok, GDN_HEADS, GDN_DK)
        q, k, v = [t.reshape(bn, n_tok, GDN_HEADS, GDN_DK) for t in jnp.split(qkv, 3, axis=-1)]
        q = _l2norm(q) * (GDN_DK ** -0.5)
        k = _l2norm(k)
        if rope:
            q, k = _axial_rope(q), _axial_rope(k)
        g = -jnp.exp(a_log.astype(F32)) * jax.nn.softplus(ab[:, :, 0] + dt_bias.astype(F32))
        beta = jax.nn.sigmoid(ab[:, :, 1])
        return q, k, v, g, beta, gate

    ql, kl, vl, gl, bl, zl = prep(x, mod_l, n_lat, n_lat, True)
    qc, kc, vc, gcx, bcx, zc = prep(xc, mod_c, xc.shape[0], n_ctx, False)

    def streams(tl, tc):
        fwd = jnp.concatenate([tc, tl], axis=1)
        bwd = jnp.concatenate([jnp.flip(tc, axis=1), jnp.flip(tl, axis=1)], axis=1)
        return jnp.stack([fwd, bwd], axis=0).transpose(0, 1, 3, 2, 4)

    def streams_dir(tl, tc):
        fwd = jnp.concatenate([tc[:, :, 0], tl[:, :, 0]], axis=1)
        bwd = jnp.concatenate([jnp.flip(tc[:, :, 1], axis=1), jnp.flip(tl[:, :, 1], axis=1)], axis=1)
        return jnp.stack([fwd, bwd], axis=0).transpose(0, 1, 3, 2)

    ltot = n_ctx + n_lat
    nc = ltot // GDN_CHUNK
    s = 2 * bn * GDN_HEADS
    qs = streams(ql, qc).reshape(s, ltot, GDN_DK).astype(BF16)
    ks = streams(kl, kc).reshape(s, ltot, GDN_DK).astype(BF16)
    vs = streams(vl, vc).reshape(s, ltot, GDN_DK).astype(BF16)
    g = streams_dir(gl, gcx).reshape(s, nc, GDN_CHUNK)
    gc = jnp.cumsum(g, axis=-1)
    gt = jnp.broadcast_to(gc[..., -1:], (s, nc, LANES))
    be = streams_dir(bl, bcx).reshape(s, nc, GDN_CHUNK)
    o = _gdn_scan(qs, ks, vs, gc[:, :, None, :], be[:, :, None, :], gt[:, :, None, :])
    o = o.reshape(2, bn, GDN_HEADS, ltot, GDN_DK).astype(F32)

    def out(o_f, o_b, z, rows, gate, rows_per_seg):
        oo = (o_f + jnp.flip(o_b, axis=2)).transpose(0, 2, 1, 3)
        ms = jnp.mean(oo * oo, axis=-1, keepdims=True)
        oo = (oo * lax.rsqrt(ms + NORM_EPS)) * norm_g.astype(F32) * jax.nn.silu(z)
        a = oo.reshape(rows.shape[0], hd).astype(BF16)
        return _mm_res(a, w_out.astype(BF16), rows, gate, rows_per_seg)

    x_new = out(o[0][:, :, n_ctx:], o[1][:, :, n_ctx:], zl, x, gate_l, n_lat)
    xc_new = None
    if ctx_out:
        xc_new = out(o[0][:, :, :n_ctx], o[1][:, :, :n_ctx], zc, xc, gate_c, xc.shape[0])
    return x_new, xc_new


def _hyena_filter_taps(n_tok, w1, b1, w2, b2, w3, b3, freq, w4):
    t = jnp.linspace(0.0, 1.0, n_tok, dtype=F32)[:, None]
    bands = (HY_EMB_DIM - 1) // 2
    wpos = 2.0 * math.pi * jnp.arange(n_tok, dtype=F32)[:, None] / n_tok
    fr = jnp.linspace(1e-4, bands - 1, bands, dtype=F32)[None, :]
    z = jnp.concatenate([t, jnp.cos(fr * wpos), -jnp.sin(fr * wpos)], axis=-1)
    freq = freq.astype(F32)
    hdn = jnp.sin(freq[0] * (z @ w1.astype(F32) + b1.astype(F32)))
    hdn = jnp.sin(freq[1] * (hdn @ w2.astype(F32) + b2.astype(F32)))
    hdn = jnp.sin(freq[2] * (hdn @ w3.astype(F32) + b3.astype(F32)))
    filt = (hdn @ w4.astype(F32)).reshape(n_tok, 2, D_MODEL)
    deltas = jnp.abs(jnp.linspace(HY_MIN_DECAY, HY_MAX_DECAY, D_MODEL, dtype=F32))
    filt = filt * jnp.exp(-t * deltas)[:, None, :]
    return filt[:, 0], filt[:, 1]


def _dft_mats(l):
    k = jnp.arange(l, dtype=jnp.int32)
    ks = (k[:, None] * k[None, :]) % (2 * l)
    ang = ks.astype(F32) * (math.pi / l)
    return jnp.cos(ang), jnp.sin(ang)


def _hyena_layer(rows, mod3, gate, rows_per_seg, bn, w_in, short_w, short_b, f_w1, f_b1, f_w2, f_b2, f_w3, f_b3,
                 f_freq, f_w4, f_bias, w_out):
    d = D_MODEL
    n_tok = rows.shape[0] // bn
    z = _modmm(rows, mod3, w_in.astype(BF16), rows_per_seg).reshape(bn, n_tok, 3 * d)
    z = _dwconv(z, short_w) + short_b
    x0, x1, v = jnp.split(z, 3, axis=-1)
    u = (v * x1).astype(BF16)
    hf, hb = _hyena_filter_taps(n_tok, f_w1, f_b1, f_w2, f_b2, f_w3, f_b3, f_freq, f_w4)
    hb = hb.at[0].set(0.0)
    cmat, smat = _dft_mats(n_tok)
    n2 = 2 * n_tok
    wk = jnp.full((n_tok, 1), 2.0 / n2, F32).at[0, 0].set(1.0 / n2)
    kr = _mm3(cmat, hf + hb) * wk
    ki = _mm3(smat, hb - hf) * wk
    alt = (1.0 - 2.0 * (jnp.arange(n_tok) % 2)).astype(F32)[:, None]
    kn = jnp.sum((hf + hb) * alt, axis=0, keepdims=True) / n2
    y = _hyena_conv(u, x0.astype(BF16), cmat.astype(BF16), smat.astype(BF16), kr, ki, kn,
                    f_bias.astype(F32)[None, :])
    return _mm_res(y.reshape(rows.shape[0], d), w_out.astype(BF16), rows, gate, rows_per_seg)


def _na_layer(x, xc, mod_l, mod_c, gate_l, bn, w_qkv, rpb, w_out):
    hd = NA_HEADS * NA_DH
    n_lat = x.shape[0] // bn
    n_ctx = xc.shape[0] // bn
    wq = w_qkv.astype(BF16)
    z = _modmm(x, mod_l, wq, n_lat, out_dtype=BF16).reshape(bn, n_lat, 3 * hd)
    zc = _modmm(xc, mod_c, wq, xc.shape[0], out_dtype=BF16).reshape(bn, n_ctx, 3 * hd)
    o = _na_attention(z[..., :hd], z[..., hd:2 * hd], z[..., 2 * hd:], zc[..., hd:2 * hd], zc[..., 2 * hd:],
                      _na_bias_table(rpb))
    return _mm_res(o.reshape(x.shape[0], hd), w_out.astype(BF16), x, gate_l, n_lat)


def _shortconv_layer(rows, mod3, gate, rows_per_seg, bn, w_in, conv_w, w_out):
    d = D_MODEL
    n_tok = rows.shape[0] // bn
    z = _modmm(rows, mod3, w_in.astype(BF16), rows_per_seg).reshape(bn, n_tok, 3 * d)
    u, b_gate, c_gate = jnp.split(z, 3, axis=-1)
    a = (b_gate * _dwconv(c_gate * u, conv_w)).astype(BF16)
    return _mm_res(a.reshape(rows.shape[0], d), w_out.astype(BF16), rows, gate, rows_per_seg)


def kernel(x, c, ctx, c_ctx, ln_g, w_mod, b_mod, final_g, gdn_w_in, gdn_conv_w, gdn_a_log, gdn_dt_bias, gdn_norm_g, gdn_w_out, hy_w_in, hy_short_w, hy_short_b, hy_f_w1, hy_f_b1, hy_f_w2, hy_f_b2, hy_f_w3, hy_f_b3, hy_f_freq, hy_f_w4, hy_f_bias, hy_w_out, na_w_qkv, na_rpb, na_w_out, sc_w_in, sc_conv_w, sc_w_out, moe_w_gr, moe_b_gr, moe_w_er, moe_b_er, moe_w_gate, moe_w_up, moe_w_down):
    bn, n_lat, d = x.shape
    n_ctx = ctx.shape[1]
    rows_ctx = bn * n_ctx
    pad = (-(bn + 1)) % 8
    c_all = jnp.concatenate([c, c_ctx[None, :], jnp.zeros((pad, d), F32)], axis=0)
    mods = _mod_vectors(c_all, w_mod, b_mod)
    xr = x.reshape(bn * n_lat, d)
    xcr = ctx.reshape(rows_ctx, d)
    for i in range(DEPTH):
        m, s = i % N_MIXERS, i // N_MIXERS
        reads_ctx = m in CTX_READING_MIXERS
        upd_ctx = any((j % N_MIXERS) in CTX_READING_MIXERS for j in range(i + 1, DEPTH))
        mv = mods[i].reshape(-1, 6, d)

        def mod3(rows, which, norm_g):
            sh, sc = mv[rows, 3 * which], mv[rows, 3 * which + 1]
            return jnp.stack([jnp.broadcast_to(norm_g, sh.shape), sh, sc], axis=1)

        lat = slice(0, bn)
        cx = slice(bn, bn + 1)
        mod_l1, mod_l2 = mod3(lat, 0, ln_g[i, 0]), mod3(lat, 1, ln_g[i, 1])
        mod_c1, mod_c2 = mod3(cx, 0, ln_g[i, 0]), mod3(cx, 1, ln_g[i, 1])
        g1_l, g2_l = mv[lat, 2][:, None, :], mv[lat, 5][:, None, :]
        g1_c, g2_c = mv[cx, 2][:, None, :], mv[cx, 5][:, None, :]
        xc_new = None
        if m == 0:
            xr, xc_new = _gdn_layer(xr, xcr, mod_l1, mod_c1, g1_l, g1_c, bn, gdn_w_in[s], gdn_conv_w[s],
                                    gdn_a_log[s], gdn_dt_bias[s], gdn_norm_g[s], gdn_w_out[s], upd_ctx)
        elif m == 1:
            hy = (hy_w_in[s], hy_short_w[s], hy_short_b[s], hy_f_w1[s], hy_f_b1[s], hy_f_w2[s], hy_f_b2[s],
                  hy_f_w3[s], hy_f_b3[s], hy_f_freq[s], hy_f_w4[s], hy_f_bias[s], hy_w_out[s])
            xr = _hyena_layer(xr, mod_l1, g1_l, n_lat, bn, *hy)
            if upd_ctx:
                xc_new = _hyena_layer(xcr, mod_c1, g1_c, rows_ctx, bn, *hy)
        elif m == 2:
            xr = _na_layer(xr, xcr, mod_l1, mod_c1, g1_l, bn, na_w_qkv[s], na_rpb[s], na_w_out[s])
            assert not upd_ctx
        else:
            xr = _shortconv_layer(xr, mod_l1, g1_l, n_lat, bn, sc_w_in[s], sc_conv_w[s], sc_w_out[s])
            if upd_ctx:
                xc_new = _shortconv_layer(xcr, mod_c1, g1_c, rows_ctx, bn, sc_w_in[s], sc_conv_w[s], sc_w_out[s])
        moe = (moe_w_gr[i], moe_b_gr[i], moe_w_er[i], moe_b_er[i], moe_w_gate[i], moe_w_up[i], moe_w_down[i])
        xr = _hier_moe_residual(xr, mod_l2, g2_l, n_lat, *moe)
        if upd_ctx:
            xcr = _hier_moe_residual(xc_new, mod_c2, g2_c, rows_ctx, *moe)
    xf = xr.reshape(bn, n_lat, d)
    ms = jnp.mean(xf * xf, axis=-1, keepdims=True)
    return (xf * lax.rsqrt(ms + NORM_EPS)) * final_g
```

```python
import functools
import math

import numpy as np
import jax
import jax.numpy as jnp
from jax import lax
from jax.experimental import pallas as pl
from jax.experimental.pallas import tpu as pltpu

F32 = jnp.float32
BF16 = jnp.bfloat16

D_MODEL = 1024
DEPTH = 4
GRID_W = 64
N_MIXERS = 4
CTX_READING_MIXERS = (0, 2)
NORM_EPS = 1e-6

GDN_HEADS = 8
GDN_DK = 128
GDN_CHUNK = 64
ROPE_BASE = 10000.0

HY_EMB_DIM = 33
HY_DECAY_TARGET = 1e-2
HY_MAX_DECAY = math.log(HY_DECAY_TARGET) / 0.3
HY_MIN_DECAY = math.log(HY_DECAY_TARGET) / 1.5

NA_HEADS = 16
NA_DH = 64
NA_WIN_R = 8
NA_WIN_C = 16

MOE_GROUPS = 4
MOE_EPG = 8
MOE_FF = 256
MOE_PAIRS = MOE_EPG * (MOE_EPG - 1) // 2
MOE_CLASSES = MOE_GROUPS * MOE_PAIRS
MOE_TM = 128

LANES = 128
VMEM_LIMIT = 56 << 20


def _cparams(sem, vmem=VMEM_LIMIT):
    return pltpu.CompilerParams(dimension_semantics=sem, vmem_limit_bytes=vmem)


def _rms_mod(x, m):
    ms = jnp.mean(x * x, axis=-1, keepdims=True)
    xn = (x * lax.rsqrt(ms + NORM_EPS)) * m[0:1]
    return xn * (1.0 + m[2:3]) + m[1:2]


def _split_bf16(x):
    hi = x.astype(BF16)
    lo = (x - hi.astype(F32)).astype(BF16)
    return hi, lo


def _dot(a, b):
    return jnp.dot(a, b, preferred_element_type=F32)


def _dot_nt(a, b):
    return lax.dot_general(a, b, (((1,), (1,)), ((), ())), preferred_element_type=F32)


def _bdot(a, b):
    return lax.dot_general(a, b, (((2,), (1,)), ((0,), (0,))), preferred_element_type=F32)


def _bdot_nt(a, b):
    return lax.dot_general(a, b, (((2,), (2,)), ((0,), (0,))), preferred_element_type=F32)


def _bdot_tn(a, b):
    return lax.dot_general(a, b, (((1,), (1,)), ((0,), (0,))), preferred_element_type=F32)


def _bdot3(a, b):
    ah, al = _split_bf16(a)
    bh, bl = _split_bf16(b)
    return _bdot(ah, bh) + (_bdot(ah, bl) + _bdot(al, bh))


def _silu(x):
    return x * (1.0 / (1.0 + jnp.exp(-x)))


def _modvec_kernel(c_ref, w_ref, b_ref, o_ref):
    c = c_ref[...]
    a = _silu(c)
    ah, al = _split_bf16(a)
    wh, wl = _split_bf16(w_ref[0])
    o_ref[0] = _dot(ah, wh) + (_dot(ah, wl) + _dot(al, wh)) + b_ref[0]


def _mod_vectors(c_all, w_mod, b_mod):
    rows, d = c_all.shape
    depth, _, n = w_mod.shape
    tn = 1536
    return pl.pallas_call(
        _modvec_kernel,
        out_shape=jax.ShapeDtypeStruct((depth, rows, n), F32),
        grid=(depth, n // tn),
        in_specs=[pl.BlockSpec((rows, d), lambda i, j: (0, 0)),
                  pl.BlockSpec((1, d, tn), lambda i, j: (i, 0, j)),
                  pl.BlockSpec((1, 1, tn), lambda i, j: (i, 0, j))],
        out_specs=pl.BlockSpec((1, rows, tn), lambda i, j: (i, 0, j)),
        compiler_params=_cparams(("arbitrary", "arbitrary")),
        name="mod_vectors",
    )(c_all, w_mod, b_mod.reshape(depth, 1, n))


def _modmm_kernel(x_ref, m_ref, w_ref, o_ref, *, ncol):
    h = _rms_mod(x_ref[...], m_ref[0]).astype(BF16)
    n = w_ref.shape[1]
    for n0 in range(0, n, ncol):
        o_ref[:, n0:n0 + ncol] = _dot(h, w_ref[:, n0:n0 + ncol]).astype(o_ref.dtype)


def _modmm(x, mod3, w, rows_per_seg, out_dtype=F32, tl=512):
    r, d = x.shape
    n = w.shape[1]
    tps = rows_per_seg // tl
    ncol = 512 if n % 512 == 0 else n
    return pl.pallas_call(
        functools.partial(_modmm_kernel, ncol=ncol),
        out_shape=jax.ShapeDtypeStruct((r, n), out_dtype),
        grid=(r // tl,),
        in_specs=[pl.BlockSpec((tl, d), lambda i: (i, 0)),
                  pl.BlockSpec((1, 3, d), lambda i: (i // tps, 0, 0)),
                  pl.BlockSpec((d, n), lambda i: (0, 0))],
        out_specs=pl.BlockSpec((tl, n), lambda i: (i, 0)),
        compiler_params=_cparams(("arbitrary",)),
        name="mod_matmul",
    )(x, mod3, w)


def _mmres_kernel(a_ref, w_ref, x_ref, g_ref, o_ref):
    o_ref[...] = x_ref[...] + g_ref[0] * _dot(a_ref[...], w_ref[...])


def _mm_res(a, w, x, gate, rows_per_seg, tl=512):
    r, k = a.shape
    d = w.shape[1]
    tps = rows_per_seg // tl
    return pl.pallas_call(
        _mmres_kernel,
        out_shape=jax.ShapeDtypeStruct((r, d), F32),
        grid=(r // tl,),
        in_specs=[pl.BlockSpec((tl, k), lambda i: (i, 0)),
                  pl.BlockSpec((k, d), lambda i: (0, 0)),
                  pl.BlockSpec((tl, d), lambda i: (i, 0)),
                  pl.BlockSpec((1, 1, d), lambda i: (i // tps, 0, 0))],
        out_specs=pl.BlockSpec((tl, d), lambda i: (i, 0)),
        compiler_params=_cparams(("arbitrary",)),
        name="matmul_residual",
    )(a, w, x, gate)


def _router_kernel(x_ref, m_ref, wh_ref, wl_ref, b_ref, h_ref, route_ref, cnt_ref, carry_ref):
    i = pl.program_id(0)

    @pl.when(i == 0)
    def _():
        carry_ref[...] = jnp.zeros_like(carry_ref)

    h = _rms_mod(x_ref[...], m_ref[0])
    hh, hl = _split_bf16(h)
    h_ref[...] = hh
    wh = wh_ref[...]
    logits = _dot(hh, wh) + (_dot(hl, wh) + _dot(hh, wl_ref[...])) + b_ref[...]
    t = logits.shape[0]
    lane = lax.broadcasted_iota(jnp.int32, logits.shape, 1).astype(F32)
    neg = -jnp.inf
    big = 1e9

    def first_argmax(vals):
        m = jnp.max(vals, axis=1, keepdims=True)
        idx = jnp.min(jnp.where(vals == m, lane, big), axis=1, keepdims=True)
        return m, idx

    lg = jnp.where(lane < MOE_GROUPS, logits, neg)
    mg, gsel = first_argmax(lg)
    wg = 1.0 / jnp.sum(jnp.exp(lg - mg), axis=1, keepdims=True)
    e_first = MOE_GROUPS + gsel * MOE_EPG
    in_grp = (lane >= e_first) & (lane < e_first + MOE_EPG)
    le = jnp.where(in_grp, logits, neg)
    m1, i1 = first_argmax(le)
    m2, i2 = first_argmax(jnp.where(lane == i1, neg, le))
    p = jnp.exp(m2 - m1)
    w1 = wg / (1.0 + p)
    w2 = wg * p / (1.0 + p)
    l1 = i1 - e_first
    l2 = i2 - e_first
    lo = jnp.minimum(l1, l2)
    hi = jnp.maximum(l1, l2)
    first_is_lo = l1 < l2
    wlo = jnp.where(first_is_lo, w1, w2)
    whi = jnp.where(first_is_lo, w2, w1)
    cls = gsel * MOE_PAIRS + lo * (2 * MOE_EPG - 1 - lo) * 0.5 + (hi - lo - 1.0)

    onehot = lane == cls
    ri = lax.broadcasted_iota(jnp.int32, (t, t), 0)
    ci = lax.broadcasted_iota(jnp.int32, (t, t), 1)
    before = (ci < ri).astype(BF16)
    prefix = _dot(before, onehot.astype(BF16)) + carry_ref[...]
    rank = jnp.sum(jnp.where(onehot, prefix, 0.0), axis=1, keepdims=True)
    carry = carry_ref[...] + jnp.sum(onehot.astype(F32), axis=0, keepdims=True)
    carry_ref[...] = carry
    cnt_ref[...] = carry
    route_ref[...] = jnp.where(lane == 0, cls, jnp.where(lane == 1, rank, jnp.where(
        lane == 2, wlo, jnp.where(lane == 3, whi, 0.0))))


def _router(x, mod3, w_r, b_r, rows_per_seg, tl=512):
    r, d = x.shape
    tps = rows_per_seg // tl
    wh, wl = _split_bf16(w_r)
    return pl.pallas_call(
        _router_kernel,
        out_shape=(jax.ShapeDtypeStruct((r, d), BF16),
                   jax.ShapeDtypeStruct((r, LANES), F32),
                   jax.ShapeDtypeStruct((1, LANES), F32)),
        grid=(r // tl,),
        in_specs=[pl.BlockSpec((tl, d), lambda i: (i, 0)),
                  pl.BlockSpec((1, 3, d), lambda i: (i // tps, 0, 0)),
                  pl.BlockSpec((d, LANES), lambda i: (0, 0)),
                  pl.BlockSpec((d, LANES), lambda i: (0, 0)),
                  pl.BlockSpec((1, LANES), lambda i: (0, 0))],
        out_specs=(pl.BlockSpec((tl, d), lambda i: (i, 0)),
                   pl.BlockSpec((tl, LANES), lambda i: (i, 0)),
                   pl.BlockSpec((1, LANES), lambda i: (0, 0))),
        scratch_shapes=[pltpu.VMEM((1, LANES), F32)],
        compiler_params=_cparams(("arbitrary",)),
        name="moe_router",
    )(x, mod3, wh, wl, b_r)


def _moe_kernel(elo_ref, ehi_ref, val_ref, x_ref, wt_ref, wgl, wul, wdl, wgh, wuh, wdh, o_ref):
    t = pl.program_id(0)

    @pl.when(val_ref[t] != 0)
    def _():
        x = x_ref[...]
        wt = wt_ref[...]
        hl = (_silu(_dot(x, wgl[0])) * _dot(x, wul[0])) * wt[:, 0:1]
        hh = (_silu(_dot(x, wgh[0])) * _dot(x, wuh[0])) * wt[:, 1:2]
        o_ref[...] = (_dot(hl.astype(BF16), wdl[0]) + _dot(hh.astype(BF16), wdh[0])).astype(o_ref.dtype)

    @pl.when(val_ref[t] == 0)
    def _():
        o_ref[...] = jnp.zeros_like(o_ref)


def _moe_experts(xs, wts, tile_elo, tile_ehi, tile_val, w_gate, w_up, w_down):
    p, d = xs.shape
    ff = w_gate.shape[2]
    nt = p // MOE_TM
    lo = lambda t, elo, ehi, val: (elo[t], 0, 0)
    hi = lambda t, elo, ehi, val: (ehi[t], 0, 0)
    row = lambda t, elo, ehi, val: (t, 0)
    grid_spec = pltpu.PrefetchScalarGridSpec(
        num_scalar_prefetch=3,
        grid=(nt,),
        in_specs=[pl.BlockSpec((MOE_TM, d), row),
                  pl.BlockSpec((MOE_TM, 2), row),
                  pl.BlockSpec((1, d, ff), lo), pl.BlockSpec((1, d, ff), lo), pl.BlockSpec((1, ff, d), lo),
                  pl.BlockSpec((1, d, ff), hi), pl.BlockSpec((1, d, ff), hi), pl.BlockSpec((1, ff, d), hi)],
        out_specs=pl.BlockSpec((MOE_TM, d), row),
    )
    return pl.pallas_call(
        _moe_kernel,
        out_shape=jax.ShapeDtypeStruct((p, d), BF16),
        grid_spec=grid_spec,
        compiler_params=_cparams(("arbitrary",)),
        name="moe_experts",
    )(tile_elo, tile_ehi, tile_val, xs, wts, w_gate, w_up, w_down, w_gate, w_up, w_down)


def _class_tables():
    elo = np.zeros((MOE_CLASSES,), np.int32)
    ehi = np.zeros((MOE_CLASSES,), np.int32)
    for g in range(MOE_GROUPS):
        for lo in range(MOE_EPG):
            for hi in range(lo + 1, MOE_EPG):
                c = g * MOE_PAIRS + lo * (2 * MOE_EPG - 1 - lo) // 2 + (hi - lo - 1)
                elo[c] = g * MOE_EPG + lo
                ehi[c] = g * MOE_EPG + hi
    return elo, ehi


_CLASS_ELO, _CLASS_EHI = _class_tables()


def _hier_moe_residual(x, mod3, gate, rows_per_seg, w_gr, b_gr, w_er, b_er, w_gate, w_up, w_down):
    r, d = x.shape
    w_r = jnp.zeros((d, LANES), F32).at[:, :MOE_GROUPS].set(w_gr)
    w_r = w_r.at[:, MOE_GROUPS:MOE_GROUPS + MOE_GROUPS * MOE_EPG].set(w_er)
    b_r = jnp.zeros((1, LANES), F32).at[0, :MOE_GROUPS].set(b_gr)
    b_r = b_r.at[0, MOE_GROUPS:MOE_GROUPS + MOE_GROUPS * MOE_EPG].set(b_er.reshape(-1))
    h, route, cnt = _router(x, mod3, w_r, b_r, rows_per_seg)

    cls = route[:, 0].astype(jnp.int32)
    rank = route[:, 1].astype(jnp.int32)
    counts = cnt[0, :MOE_CLASSES].astype(jnp.int32)
    pc = ((counts + MOE_TM - 1) // MOE_TM) * MOE_TM
    pend = jnp.cumsum(pc)
    pstart = pend - pc
    dest = pstart[cls] + rank
    p_rows = ((r + MOE_CLASSES * (MOE_TM - 1)) // MOE_TM + 1) * MOE_TM
    nt = p_rows // MOE_TM
    vals = jnp.stack([jnp.arange(r, dtype=F32), route[:, 2], route[:, 3]], axis=1)
    tab = jnp.zeros((p_rows, 3), F32).at[dest].set(vals)
    src = tab[:, 0].astype(jnp.int32)
    wts = tab[:, 1:3]
    tile_p0 = jnp.arange(nt, dtype=jnp.int32) * MOE_TM
    tile_val = (tile_p0 < pend[-1]).astype(jnp.int32)
    last_cls = jnp.searchsorted(pend, pend[-1] - 1, side="right")
    tile_cls = jnp.where(tile_val != 0, jnp.searchsorted(pend, tile_p0, side="right"), last_cls)
    tile_cls = jnp.clip(tile_cls, 0, MOE_CLASSES - 1).astype(jnp.int32)
    tile_elo = jnp.asarray(_CLASS_ELO)[tile_cls]
    tile_ehi = jnp.asarray(_CLASS_EHI)[tile_cls]

    xs = jnp.take(h, src, axis=0)
    ys = _moe_experts(xs, wts, tile_elo, tile_ehi, tile_val,
                      w_gate.astype(BF16), w_up.astype(BF16), w_down.astype(BF16))
    y = jnp.take(ys, dest, axis=0)
    segs = gate.shape[0]
    return (x.reshape(segs, rows_per_seg, d) + gate * y.reshape(segs, rows_per_seg, d)).reshape(r, d)


GDN_HB = 4
GDN_NB = 4


def _seq_conv3(z, w, row):
    n = z.shape[0]
    zp = jnp.where(row == 0, 0.0, pltpu.roll(z, 1, 0))
    zn = jnp.where(row == n - 1, 0.0, pltpu.roll(z, n - 1, 0))
    return w[0:1] * zp + w[1:2] * z + w[2:3] * zn


def _gdn_proj_kernel(x_ref, m_ref, w_ref, wab_ref, cw_ref, cos_ref, sin_ref, al_ref, dt_ref,
                     o_ref, g_ref, h_s, *, rope):
    j = pl.program_id(1)
    n = x_ref.shape[1]
    dk = GDN_DK

    @pl.when(j == 0)
    def _():
        h_s[...] = _rms_mod(x_ref[0], m_ref[0]).astype(BF16)

    row = lax.broadcasted_iota(jnp.int32, (n, 1), 0)

    def conv_silu():
        return _silu(_seq_conv3(_dot(h_s[...], w_ref[...]), cw_ref[...], row))

    @pl.when(j < 4)
    def _():
        z = conv_silu()
        qscale = jnp.where(j < 2, dk ** -0.5, 1.0).astype(F32)
        for hh in range(z.shape[1] // dk):
            zh = z[:, hh * dk:(hh + 1) * dk]
            t = (zh * lax.rsqrt(jnp.sum(zh * zh, axis=-1, keepdims=True) + NORM_EPS)) * qscale
            if rope:
                t = t * cos_ref[...] + pltpu.roll(t, dk // 2, 1) * sin_ref[...]
            o_ref[0, :, hh * dk:(hh + 1) * dk] = t.astype(o_ref.dtype)

    @pl.when((j >= 4) & (j < 6))
    def _():
        o_ref[0] = conv_silu().astype(o_ref.dtype)

    @pl.when((j >= 6) & (j < 8))
    def _():
        o_ref[0] = _dot(h_s[...], w_ref[...]).astype(o_ref.dtype)

    @pl.when(j == 8)
    def _():
        ab = _dot(h_s[...], wab_ref[...])
        lane = lax.broadcasted_iota(jnp.int32, ab.shape, 1) % LANES
        xs = ab + dt_ref[...]
        softplus = jnp.maximum(xs, 0.0) + jnp.log(1.0 + jnp.exp(-jnp.abs(xs)))
        g = -jnp.exp(al_ref[...]) * softplus
        beta = 1.0 / (1.0 + jnp.exp(-ab))
        g_ref[0] = jnp.where(lane < 2 * GDN_HB, g, jnp.where(lane < 4 * GDN_HB, beta, 0.0))


def _gdn_proj(x3, mod3, w_main, w_ab, conv_w, cosf, sinf, a_log_l, dt_l, rope):
    bn, n, d = x3.shape
    segs = mod3.shape[0]
    tc = 512
    nq = w_main.shape[1] // tc
    gl = w_ab.shape[1]
    mi = (lambda b, j: (b, 0, 0)) if segs > 1 else (lambda b, j: (0, 0, 0))
    const = lambda b, j: (0, 0)
    return pl.pallas_call(
        functools.partial(_gdn_proj_kernel, rope=rope),
        out_shape=(jax.ShapeDtypeStruct((bn, n, w_main.shape[1]), BF16),
                   jax.ShapeDtypeStruct((bn, n, gl), F32)),
        grid=(bn, nq + 1),
        in_specs=[pl.BlockSpec((1, n, d), lambda b, j: (b, 0, 0)),
                  pl.BlockSpec((1, 3, d), mi),
                  pl.BlockSpec((d, tc), lambda b, j: (0, jnp.minimum(j, nq - 1))),
                  pl.BlockSpec((d, gl), const),
                  pl.BlockSpec((3, tc), lambda b, j: (0, jnp.minimum(j, 5))),
                  pl.BlockSpec((n, GDN_DK), const), pl.BlockSpec((n, GDN_DK), const),
                  pl.BlockSpec((1, gl), const), pl.BlockSpec((1, gl), const)],
        out_specs=(pl.BlockSpec((1, n, tc), lambda b, j: (b, 0, jnp.minimum(j, nq - 1))),
                   pl.BlockSpec((1, n, gl), lambda b, j: (b, 0, 0))),
        scratch_shapes=[pltpu.VMEM((n, d), BF16)],
        compiler_params=_cparams(("arbitrary", "arbitrary")),
        name="gdn_proj",
    )(x3, mod3, w_main, w_ab, conv_w, cosf, sinf, a_log_l, dt_l)


def _gdn_scan_kernel(ql, kl, vl, gl, qc, kc, vc, gx, ol, oc, s_ref, u_s, w_s, at_s, qd_s, kd_s, la_s, *, hb):
    cs = GDN_CHUNK
    dk = GDN_DK
    ns = 2 * hb
    n_lat = ql.shape[1]
    n_ctx = qc.shape[1]
    s_ref[...] = jnp.zeros_like(s_ref)
    ol[...] = jnp.zeros_like(ol)
    oc[...] = jnp.zeros_like(oc)
    ii = lax.broadcasted_iota(jnp.int32, (cs, cs), 0)
    jj = lax.broadcasted_iota(jnp.int32, (cs, cs), 1)
    eye = (ii == jj)[None]
    lower = (ii >= jj)[None]
    upper = (ii <= jj)[None]
    eye_f = eye.astype(F32)

    def run_window(qr, kr, vr, gr, orf, t0f, t0b, nb):
        b2 = ns * nb
        bi = lax.broadcasted_iota(jnp.int32, (b2, cs, cs), 0)
        sign = 1 - 2 * ((bi // nb) % 2)
        dij = sign * (lax.broadcasted_iota(jnp.int32, (b2, cs, cs), 1)
                      - lax.broadcasted_iota(jnp.int32, (b2, cs, cs), 2))
        tri = dij >= 0
        stri = dij > 0

        def rev(a):
            return jnp.concatenate([a[c:c + 1] for c in reversed(range(nb))], axis=0)

        def gates(t0, d):
            slab = gr[0, pl.ds(t0, nb * cs), :].reshape(nb, cs, LANES)
            g1 = slab.astype(BF16)
            r1 = slab - g1.astype(F32)
            g2 = r1.astype(BF16)
            g3 = (r1 - g2.astype(F32)).astype(BF16)
            t = jnp.broadcast_to((lower if d == 0 else upper).astype(BF16), (nb, cs, cs))
            gcum = _bdot(t, g1) + (_bdot(t, g2) + _bdot(t, g3))
            gtot = jnp.sum(slab, axis=1, keepdims=True)
            return (slab, gcum, gtot) if d == 0 else (rev(slab), rev(gcum), rev(gtot))

        gate_d = (gates(t0f, 0), gates(t0b, 1))
        gc_l, be_l, gt_l, q_l, k_l, v_l = [], [], [], [], [], []
        for h in range(hb):
            for d in range(2):
                slab, gcum, gtot = gate_d[d]
                lg = d * hb + h
                lb = 2 * hb + lg
                gc_l.append(gcum[:, :, lg:lg + 1])
                be_l.append(slab[:, :, lb:lb + 1])
                gt_l.append(gtot[:, :, lg:lg + 1])
                t0 = t0f if d == 0 else t0b
                cols = slice(h * dk, (h + 1) * dk)
                fix = (lambda a: a) if d == 0 else rev
                q_l.append(fix(qr[0, pl.ds(t0, nb * cs), cols].reshape(nb, cs, dk)))
                k_l.append(fix(kr[0, pl.ds(t0, nb * cs), cols].reshape(nb, cs, dk)))
                v_l.append(fix(vr[0, pl.ds(t0, nb * cs), cols].reshape(nb, cs, dk)))
        gc = jnp.concatenate(gc_l, axis=0)
        be = jnp.concatenate(be_l, axis=0)
        gt = jnp.concatenate(gt_l, axis=0)
        q = jnp.concatenate(q_l, axis=0)
        k = jnp.concatenate(k_l, axis=0)
        v = jnp.concatenate(v_l, axis=0)

        gcl = gc + jnp.zeros((1, 1, cs), F32)
        gcr = jnp.sum(jnp.where(eye, gcl, 0.0), axis=1, keepdims=True)
        decay = jnp.where(tri, jnp.exp(jnp.where(tri, gcl - gcr, 0.0)), 0.0)
        a = jnp.where(stri, _bdot_nt(k, k) * be * decay, 0.0)
        mm = lambda l, r: _bdot(l.astype(BF16), r.astype(BF16))
        x = mm(a, a)
        p = eye_f - a
        p = p + mm(p, x)
        for _ in range(4):
            x = mm(x, x)
            p = p + mm(p, x)
        pb = p.astype(BF16)
        eg = jnp.exp(gc)
        kf = k.astype(F32)
        u_s[:, :nb] = _bdot(pb, (v.astype(F32) * be).astype(BF16)).reshape(ns, nb, cs, dk)
        w_s[:, :nb] = _bdot(pb, (kf * (be * eg)).astype(BF16)).astype(BF16).reshape(ns, nb, cs, dk)
        at_s[:, :nb] = jnp.where(tri, _bdot_nt(q, k) * decay, 0.0).astype(BF16).reshape(ns, nb, cs, cs)
        qd_s[:, :nb] = (q.astype(F32) * eg).astype(BF16).reshape(ns, nb, cs, dk)
        kd_s[:, :nb] = (kf * jnp.exp(gt - gc)).astype(BF16).reshape(ns, nb, cs, dk)
        la_s[:, :nb] = (jnp.exp(gt) + jnp.zeros((1, 1, dk), F32)).reshape(ns, nb, 1, dk)

        for j in range(nb):
            s = s_ref[...]
            sbf = s.astype(BF16)
            v_new = u_s[:, j] - _bdot(w_s[:, j], sbf)
            vb = v_new.astype(BF16)
            o = _bdot(qd_s[:, j], sbf) + _bdot(at_s[:, j], vb)
            s_ref[...] = s * la_s[:, j] + _bdot_tn(kd_s[:, j], vb)
            for h in range(hb):
                cols = slice(h * dk, (h + 1) * dk)
                orf[0, pl.ds(t0f + j * cs, cs), cols] += o[2 * h]
                orf[0, pl.ds(t0b + (nb - 1 - j) * cs, cs), cols] += o[2 * h + 1]

    run_window(qc, kc, vc, gx, oc, 0, 0, n_ctx // cs)
    wrows = GDN_NB * cs
    nwin = n_lat // wrows

    def body(wi, carry):
        t0f = pl.multiple_of(wi * wrows, wrows)
        t0b = pl.multiple_of((nwin - 1 - wi) * wrows, wrows)
        run_window(ql, kl, vl, gl, ol, t0f, t0b, GDN_NB)
        return carry

    lax.fori_loop(0, nwin, body, 0)


def _gdn_scan(pl_, gl_, pc_, gc_):
    bn, n_lat, _ = pl_.shape
    n_ctx = pc_.shape[1]
    hb, dk, cs = GDN_HB, GDN_DK, GDN_CHUNK
    hd = GDN_HEADS * dk
    ngrp = GDN_HEADS // hb
    wcol = hb * dk
    nbm = max(GDN_NB, n_ctx // cs)

    def sec(n, k):
        return pl.BlockSpec((1, n, wcol), lambda b, g: (b, 0, k * ngrp + g))

    gate = lambda n: pl.BlockSpec((1, n, LANES), lambda b, g: (b, 0, g))
    out = lambda n: pl.BlockSpec((1, n, wcol), lambda b, g: (b, 0, g))
    return pl.pallas_call(
        functools.partial(_gdn_scan_kernel, hb=hb),
        out_shape=(jax.ShapeDtypeStruct((bn, n_lat, hd), F32), jax.ShapeDtypeStruct((bn, n_ctx, hd), F32)),
        grid=(bn, ngrp),
        in_specs=[sec(n_lat, 0), sec(n_lat, 1), sec(n_lat, 2), gate(n_lat),
                  sec(n_ctx, 0), sec(n_ctx, 1), sec(n_ctx, 2), gate(n_ctx)],
        out_specs=(out(n_lat), out(n_ctx)),
        scratch_shapes=[pltpu.VMEM((2 * hb, dk, dk), F32),
                        pltpu.VMEM((2 * hb, nbm, cs, dk), F32),
                        pltpu.VMEM((2 * hb, nbm, cs, dk), BF16),
                        pltpu.VMEM((2 * hb, nbm, cs, cs), BF16),
                        pltpu.VMEM((2 * hb, nbm, cs, dk), BF16),
                        pltpu.VMEM((2 * hb, nbm, cs, dk), BF16),
                        pltpu.VMEM((2 * hb, nbm, 1, dk), F32)],
        compiler_params=_cparams(("arbitrary", "arbitrary")),
        name="gdn_scan",
    )(pl_, pl_, pl_, gl_, pc_, pc_, pc_, gc_)


def _gdn_out_kernel(o_ref, z_ref, ng_ref, w_ref, x_ref, g_ref, y_ref):
    dk = GDN_DK
    parts = []
    for h in range(o_ref.shape[1] // dk):
        cols = slice(h * dk, (h + 1) * dk)
        oh = o_ref[:, cols]
        ms = jnp.mean(oh * oh, axis=-1, keepdims=True)
        on = (oh * lax.rsqrt(ms + NORM_EPS)) * ng_ref[...]
        parts.append((on * _silu(z_ref[:, cols].astype(F32))).astype(BF16))
    a = jnp.concatenate(parts, axis=1)
    y_ref[...] = x_ref[...] + g_ref[0] * _dot(a, w_ref[...])


def _gdn_out(o, proj, norm_g, w_out, x, gate, rows_per_seg, tl=512):
    r, hd = o.shape
    d = w_out.shape[1]
    tps = rows_per_seg // tl
    return pl.pallas_call(
        _gdn_out_kernel,
        out_shape=jax.ShapeDtypeStruct((r, d), F32),
        grid=(r // tl,),
        in_specs=[pl.BlockSpec((tl, hd), lambda i: (i, 0)),
                  pl.BlockSpec((tl, hd), lambda i: (i, 3)),
                  pl.BlockSpec((1, GDN_DK), lambda i: (0, 0)),
                  pl.BlockSpec((hd, d), lambda i: (0, 0)),
                  pl.BlockSpec((tl, d), lambda i: (i, 0)),
                  pl.BlockSpec((1, 1, d), lambda i: (i // tps, 0, 0))],
        out_specs=pl.BlockSpec((tl, d), lambda i: (i, 0)),
        compiler_params=_cparams(("arbitrary",)),
        name="gdn_out",
    )(o, proj, norm_g, w_out, x, gate)


def _proj3_kernel(x_ref, m_ref, w0_ref, w1_ref, w2_ref, cw_ref, cb_ref, *rest, hyena):
    j = pl.program_id(1)
    o_refs, h_s = rest[:-1], rest[-1]
    n = x_ref.shape[1]

    @pl.when(j == 0)
    def _():
        h_s[...] = _rms_mod(x_ref[0], m_ref[0]).astype(BF16)

    h = h_s[...]
    row = lax.broadcasted_iota(jnp.int32, (n, 1), 0)
    z0 = _dot(h, w0_ref[...])
    z1 = _dot(h, w1_ref[...])
    z2 = _dot(h, w2_ref[...])
    if hyena:
        cb = cb_ref[...]
        x0 = _seq_conv3(z0, cw_ref[0], row) + cb[0:1]
        x1 = _seq_conv3(z1, cw_ref[1], row) + cb[1:2]
        v = _seq_conv3(z2, cw_ref[2], row) + cb[2:3]
        o_refs[0][0] = (v * x1).astype(BF16)
        o_refs[1][0] = x0.astype(BF16)
    else:
        o_refs[0][0] = (z1 * _seq_conv3(z2 * z0, cw_ref[0], row)).astype(BF16)


def _proj3(x3, mod3, w_in, cw, cb, hyena, tc=256):
    bn, n, d = x3.shape
    segs = mod3.shape[0]
    nj = d // tc
    mi = (lambda b, j: (b, 0, 0)) if segs > 1 else (lambda b, j: (0, 0, 0))
    wspec = lambda k: pl.BlockSpec((d, tc), lambda b, j: (0, k * nj + j))
    seq = pl.BlockSpec((1, n, tc), lambda b, j: (b, 0, j))
    n_out = 2 if hyena else 1
    outs = pl.pallas_call(
        functools.partial(_proj3_kernel, hyena=hyena),
        out_shape=tuple(jax.ShapeDtypeStruct((bn, n, d), BF16) for _ in range(n_out)),
        grid=(bn, nj),
        in_specs=[pl.BlockSpec((1, n, d), lambda b, j: (b, 0, 0)),
                  pl.BlockSpec((1, 3, d), mi),
                  wspec(0), wspec(1), wspec(2),
                  pl.BlockSpec((cw.shape[0], 3, tc), lambda b, j: (0, 0, j)),
                  pl.BlockSpec((3, tc), lambda b, j: (0, j))],
        out_specs=tuple(seq for _ in range(n_out)),
        scratch_shapes=[pltpu.VMEM((n, d), BF16)],
        compiler_params=_cparams(("arbitrary", "arbitrary")),
        name="hyena_proj" if hyena else "shortconv_proj",
    )(x3, mod3, w_in, w_in, w_in, cw, cb)
    return outs


def _final_norm_kernel(x_ref, g_ref, o_ref):
    x = x_ref[...]
    ms = jnp.mean(x * x, axis=-1, keepdims=True)
    o_ref[...] = (x * lax.rsqrt(ms + NORM_EPS)) * g_ref[...]


def _final_norm(x, g, tl=1024):
    r, d = x.shape
    return pl.pallas_call(
        _final_norm_kernel,
        out_shape=jax.ShapeDtypeStruct((r, d), F32),
        grid=(r // tl,),
        in_specs=[pl.BlockSpec((tl, d), lambda i: (i, 0)), pl.BlockSpec((1, d), lambda i: (0, 0))],
        out_specs=pl.BlockSpec((tl, d), lambda i: (i, 0)),
        compiler_params=_cparams(("arbitrary",)),
        name="final_norm",
    )(x, g)


def _hyena_kernel(u_ref, x0_ref, c_ref, s_ref, kr_ref, ki_ref, kn_ref, fb_ref, o_ref):
    ub = u_ref[0]
    l = ub.shape[0]
    cm = c_ref[...]
    sm = s_ref[...]
    a = _dot(cm, ub)
    b = _dot(sm, ub)
    kr = kr_ref[...]
    ki = ki_ref[...]
    zr = (a * kr + b * ki).astype(BF16)
    zi = (b * kr - a * ki).astype(BF16)
    y = _dot(cm, zr) + _dot(sm, zi)
    uf = ub.astype(F32)
    tpar = lax.broadcasted_iota(jnp.int32, (l, 1), 0) & 1
    alt = (1 - 2 * tpar).astype(F32)
    un = jnp.sum(uf * alt, axis=0, keepdims=True)
    y = y + alt * (un * kn_ref[...]) + uf * fb_ref[...]
    o_ref[0] = (y * x0_ref[0].astype(F32)).astype(o_ref.dtype)


def _hyena_conv(u, x0, cmat, smat, kr, ki, kn, fbias, td=256):
    b, l, d = u.shape
    seq = pl.BlockSpec((1, l, td), lambda i, j: (i, 0, j))
    mat = pl.BlockSpec((l, l), lambda i, j: (0, 0), pipeline_mode=pl.Buffered(1))
    spec = pl.BlockSpec((l, td), lambda i, j: (0, j))
    vec = pl.BlockSpec((1, td), lambda i, j: (0, j))
    return pl.pallas_call(
        _hyena_kernel,
        out_shape=jax.ShapeDtypeStruct((b, l, d), BF16),
        grid=(b, d // td),
        in_specs=[seq, seq, mat, mat, spec, spec, vec, vec],
        out_specs=seq,
        compiler_params=_cparams(("arbitrary", "arbitrary")),
        name="hyena_conv",
    )(u, x0, cmat, smat, kr, ki, kn, fbias)


def _mm3_kernel(a_ref, b_ref, o_ref):
    ah, al = _split_bf16(a_ref[...])
    bh, bl = _split_bf16(b_ref[...])
    o_ref[...] = _dot(ah, bh) + (_dot(ah, bl) + _dot(al, bh))


def _mm3(a, b, tm=256, tn=256):
    m, k = a.shape
    n = b.shape[1]
    tm = min(tm, m)
    return pl.pallas_call(
        _mm3_kernel,
        out_shape=jax.ShapeDtypeStruct((m, n), F32),
        grid=(n // tn, m // tm),
        in_specs=[pl.BlockSpec((tm, k), lambda j, i: (i, 0)),
                  pl.BlockSpec((k, tn), lambda j, i: (0, j))],
        out_specs=pl.BlockSpec((tm, tn), lambda j, i: (i, j)),
        compiler_params=_cparams(("arbitrary", "arbitrary")),
        name="matmul_f32x3",
    )(a, b)


def _na_kernel(q_ref, k_ref, v_ref, kc_ref, vc_ref, bias_ref, o_ref, *, rows):
    r = pl.program_id(1)
    rs = jnp.clip(r - NA_WIN_R // 2, 0, rows - NA_WIN_R)
    t0 = pl.multiple_of(rs * GRID_W, GRID_W)
    nk = NA_WIN_R * GRID_W
    scale = NA_DH ** -0.5
    lane = lax.broadcasted_iota(jnp.int32, (GRID_W, LANES), 1)
    first = lane < NA_DH
    scores = []
    for h in range(NA_HEADS):
        cols = slice((h // 2) * LANES, (h // 2 + 1) * LANES)
        q2 = q_ref[0, :, cols]
        qm = jnp.where(first if h % 2 == 0 else jnp.logical_not(first), q2, jnp.zeros_like(q2))
        s_loc = _dot_nt(qm, k_ref[0, pl.ds(t0, nk), cols]) * scale + bias_ref[0, h]
        s_ctx = _dot_nt(qm, kc_ref[0, :, cols]) * scale
        scores.append((s_loc, s_ctx))
    probs = []
    for s_loc, s_ctx in scores:
        m = jnp.maximum(jnp.max(s_loc, axis=1, keepdims=True), jnp.max(s_ctx, axis=1, keepdims=True))
        p_loc = jnp.exp(s_loc - m)
        p_ctx = jnp.exp(s_ctx - m)
        den = jnp.sum(p_loc, axis=1, keepdims=True) + jnp.sum(p_ctx, axis=1, keepdims=True)
        probs.append((p_loc.astype(BF16), p_ctx.astype(BF16), den))
    outs = []
    for h, (p_loc, p_ctx, den) in enumerate(probs):
        cols = slice((h // 2) * LANES, (h // 2 + 1) * LANES)
        o = _dot(p_loc, v_ref[0, pl.ds(t0, nk), cols]) + _dot(p_ctx, vc_ref[0, :, cols])
        outs.append(o / den)
    pairs = [jnp.where(first, outs[2 * hp], outs[2 * hp + 1]).astype(o_ref.dtype) for hp in range(NA_HEADS // 2)]
    o_ref[0] = jnp.concatenate(pairs, axis=1)


def _na_attention(q, k, v, kc, vc, bias_tab):
    b, l, hd = q.shape
    lc = kc.shape[1]
    rows = l // GRID_W
    half = NA_WIN_R // 2
    return pl.pallas_call(
        functools.partial(_na_kernel, rows=rows),
        out_shape=jax.ShapeDtypeStruct((b, l, hd), BF16),
        grid=(b, rows),
        in_specs=[pl.BlockSpec((1, GRID_W, hd), lambda i, r: (i, r, 0)),
                  pl.BlockSpec((1, l, hd), lambda i, r: (i, 0, 0)),
                  pl.BlockSpec((1, l, hd), lambda i, r: (i, 0, 0)),
                  pl.BlockSpec((1, lc, hd), lambda i, r: (i, 0, 0)),
                  pl.BlockSpec((1, lc, hd), lambda i, r: (i, 0, 0)),
                  pl.BlockSpec((1, NA_HEADS, GRID_W, NA_WIN_R * GRID_W),
                               lambda i, r: (r - jnp.clip(r - half, 0, rows - NA_WIN_R), 0, 0, 0))],
        out_specs=pl.BlockSpec((1, GRID_W, hd), lambda i, r: (i, r, 0)),
        compiler_params=_cparams(("arbitrary", "arbitrary")),
        name="na_attention",
    )(q, k, v, kc, vc, bias_tab)


def _na_bias_table(rpb):
    qc = np.arange(GRID_W)
    kc = np.arange(GRID_W)
    cstart = np.clip(qc - NA_WIN_C // 2, 0, GRID_W - NA_WIN_C)
    valid = (kc[None, :] >= cstart[:, None]) & (kc[None, :] < cstart[:, None] + NA_WIN_C)
    dc = np.clip(kc[None, :] - qc[:, None] + NA_WIN_C - 1, 0, 2 * NA_WIN_C - 2)
    off = np.arange(NA_WIN_R)
    j = np.arange(NA_WIN_R)
    dr = j[None, :] - off[:, None] + NA_WIN_R - 1
    sel_r = (dr[:, :, None] == np.arange(2 * NA_WIN_R - 1)).astype(np.float32)
    sel_c = (dc[:, :, None] == np.arange(2 * NA_WIN_C - 1)).astype(np.float32)
    t = jnp.einsum("ojr,hrc,qkc->ohqjk", sel_r, rpb.astype(F32), sel_c, precision=lax.Precision.HIGHEST)
    t = jnp.where(jnp.asarray(valid)[None, None, :, None, :], t, jnp.float32(-1e30))
    return t.reshape(NA_WIN_R, NA_HEADS, GRID_W, NA_WIN_R * GRID_W)


def _rope_tables(n_tok, dh):
    pos = jnp.arange(n_tok)
    row = (pos // GRID_W).astype(F32)
    col = (pos % GRID_W).astype(F32)
    n_freq = dh // 4
    inv = ROPE_BASE ** (-jnp.arange(n_freq, dtype=F32) / n_freq)
    ang = jnp.concatenate([row[:, None] * inv, col[:, None] * inv], axis=-1)
    cos, sin = jnp.cos(ang), jnp.sin(ang)
    return jnp.concatenate([cos, cos], axis=-1), jnp.concatenate([-sin, sin], axis=-1)


def _gdn_gate_lanes():
    ngrp = GDN_HEADS // GDN_HB
    src = -np.ones((ngrp * LANES,), np.int64)
    for grp in range(ngrp):
        for t in range(2):
            for dr in range(2):
                for hh in range(GDN_HB):
                    lane = grp * LANES + t * 2 * GDN_HB + dr * GDN_HB + hh
                    src[lane] = t * 2 * GDN_HEADS + dr * GDN_HEADS + grp * GDN_HB + hh
    return src


_GDN_LANE_SRC = _gdn_gate_lanes()


def _gdn_layer(x, xc, mod_l, mod_c, gate_l, gate_c, bn, w_in, conv_w, a_log, dt_bias, norm_g, w_out, ctx_out):
    d = D_MODEL
    hd = GDN_HEADS * GDN_DK
    n_lat = x.shape[0] // bn
    n_ctx = xc.shape[0] // bn
    w_main = w_in[:, :4 * hd].astype(BF16)
    used = jnp.asarray(_GDN_LANE_SRC >= 0)
    lane_src = jnp.asarray(np.maximum(_GDN_LANE_SRC, 0))
    w_ab = jnp.where(used[None, :], w_in[:, 4 * hd:][:, lane_src], 0.0).astype(BF16)
    decay_lane = used & (lane_src < 2 * GDN_HEADS)
    a_log_l = jnp.where(decay_lane, a_log.reshape(-1)[lane_src % (2 * GDN_HEADS)], 0.0)[None, :].astype(F32)
    dt_l = jnp.where(decay_lane, dt_bias.reshape(-1)[lane_src % (2 * GDN_HEADS)], 0.0)[None, :].astype(F32)
    cosf, sinf = _rope_tables(n_lat, GDN_DK)
    dummy = jnp.zeros((n_ctx, GDN_DK), F32)
    p_l, g_l = _gdn_proj(x.reshape(bn, n_lat, d), mod_l, w_main, w_ab, conv_w, cosf, sinf, a_log_l, dt_l, True)
    p_c, g_c = _gdn_proj(xc.reshape(bn, n_ctx, d), mod_c, w_main, w_ab, conv_w, dummy, dummy, a_log_l, dt_l, False)
    o_l, o_c = _gdn_scan(p_l, g_l, p_c, g_c)
    wo = w_out.astype(BF16)
    ng = norm_g.astype(F32)[None, :]
    x_new = _gdn_out(o_l.reshape(bn * n_lat, hd), p_l.reshape(bn * n_lat, 4 * hd), ng, wo, x, gate_l, n_lat)
    xc_new = None
    if ctx_out:
        xc_new = _gdn_out(o_c.reshape(bn * n_ctx, hd), p_c.reshape(bn * n_ctx, 4 * hd), ng, wo, xc, gate_c,
                          xc.shape[0])
    return x_new, xc_new


def _hyena_filter_taps(n_tok, w1, b1, w2, b2, w3, b3, freq, w4):
    t = jnp.linspace(0.0, 1.0, n_tok, dtype=F32)[:, None]
    bands = (HY_EMB_DIM - 1) // 2
    wpos = 2.0 * math.pi * jnp.arange(n_tok, dtype=F32)[:, None] / n_tok
    fr = jnp.linspace(1e-4, bands - 1, bands, dtype=F32)[None, :]
    z = jnp.concatenate([t, jnp.cos(fr * wpos), -jnp.sin(fr * wpos)], axis=-1)
    freq = freq.astype(F32)
    hdn = jnp.sin(freq[0] * (z @ w1.astype(F32) + b1.astype(F32)))
    hdn = jnp.sin(freq[1] * (hdn @ w2.astype(F32) + b2.astype(F32)))
    hdn = jnp.sin(freq[2] * (hdn @ w3.astype(F32) + b3.astype(F32)))
    filt = (hdn @ w4.astype(F32)).reshape(n_tok, 2, D_MODEL)
    deltas = jnp.abs(jnp.linspace(HY_MIN_DECAY, HY_MAX_DECAY, D_MODEL, dtype=F32))
    filt = filt * jnp.exp(-t * deltas)[:, None, :]
    return filt[:, 0], filt[:, 1]


def _dft_mats(l):
    k = jnp.arange(l, dtype=jnp.int32)
    ks = (k[:, None] * k[None, :]) % (2 * l)
    ang = ks.astype(F32) * (math.pi / l)
    return jnp.cos(ang), jnp.sin(ang)


def _hyena_layer(rows, mod3, gate, rows_per_seg, bn, w_in, short_w, short_b, f_w1, f_b1, f_w2, f_b2, f_w3, f_b3,
                 f_freq, f_w4, f_bias, w_out):
    d = D_MODEL
    n_tok = rows.shape[0] // bn
    cw = short_w.reshape(3, 3, d).transpose(1, 0, 2).astype(F32)
    u, x0 = _proj3(rows.reshape(bn, n_tok, d), mod3, w_in.astype(BF16), cw, short_b.reshape(3, d).astype(F32),
                   hyena=True)
    hf, hb = _hyena_filter_taps(n_tok, f_w1, f_b1, f_w2, f_b2, f_w3, f_b3, f_freq, f_w4)
    hb = hb.at[0].set(0.0)
    cmat, smat = _dft_mats(n_tok)
    n2 = 2 * n_tok
    wk = jnp.full((n_tok, 1), 2.0 / n2, F32).at[0, 0].set(1.0 / n2)
    kr = _mm3(cmat, hf + hb) * wk
    ki = _mm3(smat, hb - hf) * wk
    alt = (1.0 - 2.0 * (jnp.arange(n_tok) % 2)).astype(F32)[:, None]
    kn = jnp.sum((hf + hb) * alt, axis=0, keepdims=True) / n2
    y = _hyena_conv(u, x0, cmat.astype(BF16), smat.astype(BF16), kr, ki, kn, f_bias.astype(F32)[None, :])
    return _mm_res(y.reshape(rows.shape[0], d), w_out.astype(BF16), rows, gate, rows_per_seg)


def _na_layer(x, xc, mod_l, mod_c, gate_l, bn, w_qkv, rpb, w_out):
    hd = NA_HEADS * NA_DH
    n_lat = x.shape[0] // bn
    n_ctx = xc.shape[0] // bn
    wq = w_qkv.astype(BF16)
    z = _modmm(x, mod_l, wq, n_lat, out_dtype=BF16).reshape(bn, n_lat, 3 * hd)
    zc = _modmm(xc, mod_c, wq, xc.shape[0], out_dtype=BF16).reshape(bn, n_ctx, 3 * hd)
    o = _na_attention(z[..., :hd], z[..., hd:2 * hd], z[..., 2 * hd:], zc[..., hd:2 * hd], zc[..., 2 * hd:],
                      _na_bias_table(rpb))
    return _mm_res(o.reshape(x.shape[0], hd), w_out.astype(BF16), x, gate_l, n_lat)


def _shortconv_layer(rows, mod3, gate, rows_per_seg, bn, w_in, conv_w, w_out):
    d = D_MODEL
    n_tok = rows.shape[0] // bn
    (a,) = _proj3(rows.reshape(bn, n_tok, d), mod3, w_in.astype(BF16), conv_w.astype(F32)[None],
                  jnp.zeros((3, d), F32), hyena=False)
    return _mm_res(a.reshape(rows.shape[0], d), w_out.astype(BF16), rows, gate, rows_per_seg)


def kernel(x, c, ctx, c_ctx, ln_g, w_mod, b_mod, final_g, gdn_w_in, gdn_conv_w, gdn_a_log, gdn_dt_bias, gdn_norm_g, gdn_w_out, hy_w_in, hy_short_w, hy_short_b, hy_f_w1, hy_f_b1, hy_f_w2, hy_f_b2, hy_f_w3, hy_f_b3, hy_f_freq, hy_f_w4, hy_f_bias, hy_w_out, na_w_qkv, na_rpb, na_w_out, sc_w_in, sc_conv_w, sc_w_out, moe_w_gr, moe_b_gr, moe_w_er, moe_b_er, moe_w_gate, moe_w_up, moe_w_down):
    bn, n_lat, d = x.shape
    n_ctx = ctx.shape[1]
    rows_ctx = bn * n_ctx
    pad = (-(bn + 1)) % 8
    c_all = jnp.concatenate([c, c_ctx[None, :], jnp.zeros((pad, d), F32)], axis=0)
    mods = _mod_vectors(c_all, w_mod, b_mod)
    xr = x.reshape(bn * n_lat, d)
    xcr = ctx.reshape(rows_ctx, d)
    for i in range(DEPTH):
        m, s = i % N_MIXERS, i // N_MIXERS
        reads_ctx = m in CTX_READING_MIXERS
        upd_ctx = any((j % N_MIXERS) in CTX_READING_MIXERS for j in range(i + 1, DEPTH))
        mv = mods[i].reshape(-1, 6, d)

        def mod3(rows, which, norm_g):
            sh, sc = mv[rows, 3 * which], mv[rows, 3 * which + 1]
            return jnp.stack([jnp.broadcast_to(norm_g, sh.shape), sh, sc], axis=1)

        lat = slice(0, bn)
        cx = slice(bn, bn + 1)
        mod_l1, mod_l2 = mod3(lat, 0, ln_g[i, 0]), mod3(lat, 1, ln_g[i, 1])
        mod_c1, mod_c2 = mod3(cx, 0, ln_g[i, 0]), mod3(cx, 1, ln_g[i, 1])
        g1_l, g2_l = mv[lat, 2][:, None, :], mv[lat, 5][:, None, :]
        g1_c, g2_c = mv[cx, 2][:, None, :], mv[cx, 5][:, None, :]
        xc_new = None
        if m == 0:
            xr, xc_new = _gdn_layer(xr, xcr, mod_l1, mod_c1, g1_l, g1_c, bn, gdn_w_in[s], gdn_conv_w[s],
                                    gdn_a_log[s], gdn_dt_bias[s], gdn_norm_g[s], gdn_w_out[s], upd_ctx)
        elif m == 1:
            hy = (hy_w_in[s], hy_short_w[s], hy_short_b[s], hy_f_w1[s], hy_f_b1[s], hy_f_w2[s], hy_f_b2[s],
                  hy_f_w3[s], hy_f_b3[s], hy_f_freq[s], hy_f_w4[s], hy_f_bias[s], hy_w_out[s])
            xr = _hyena_layer(xr, mod_l1, g1_l, n_lat, bn, *hy)
            if upd_ctx:
                xc_new = _hyena_layer(xcr, mod_c1, g1_c, rows_ctx, bn, *hy)
        elif m == 2:
            xr = _na_layer(xr, xcr, mod_l1, mod_c1, g1_l, bn, na_w_qkv[s], na_rpb[s], na_w_out[s])
            assert not upd_ctx
        else:
            xr = _shortconv_layer(xr, mod_l1, g1_l, n_lat, bn, sc_w_in[s], sc_conv_w[s], sc_w_out[s])
            if upd_ctx:
                xc_new = _shortconv_layer(xcr, mod_c1, g1_c, rows_ctx, bn, sc_w_in[s], sc_conv_w[s], sc_w_out[s])
        moe = (moe_w_gr[i], moe_b_gr[i], moe_w_er[i], moe_b_er[i], moe_w_gate[i], moe_w_up[i], moe_w_down[i])
        xr = _hier_moe_residual(xr, mod_l2, g2_l, n_lat, *moe)
        if upd_ctx:
            xcr = _hier_moe_residual(xc_new, mod_c2, g2_c, rows_ctx, *moe)
    return _final_norm(xr, final_g.astype(F32)[None, :]).reshape(bn, n_lat, d)
```

```python
import functools
import math

import numpy as np
import jax
import jax.numpy as jnp
from jax import lax
from jax.experimental import pallas as pl
from jax.experimental.pallas import tpu as pltpu

F32 = jnp.float32
BF16 = jnp.bfloat16

D_MODEL = 1024
DEPTH = 4
GRID_W = 64
N_MIXERS = 4
CTX_READING_MIXERS = (0, 2)
NORM_EPS = 1e-6

GDN_HEADS = 8
GDN_DK = 128
GDN_CHUNK = 64
ROPE_BASE = 10000.0

HY_EMB_DIM = 33
HY_DECAY_TARGET = 1e-2
HY_MAX_DECAY = math.log(HY_DECAY_TARGET) / 0.3
HY_MIN_DECAY = math.log(HY_DECAY_TARGET) / 1.5

NA_HEADS = 16
NA_DH = 64
NA_WIN_R = 8
NA_WIN_C = 16

MOE_GROUPS = 4
MOE_EPG = 8
MOE_FF = 256
MOE_PAIRS = MOE_EPG * (MOE_EPG - 1) // 2
MOE_CLASSES = MOE_GROUPS * MOE_PAIRS
MOE_TM = 128

LANES = 128
VMEM_LIMIT = 56 << 20


def _cparams(sem, vmem=VMEM_LIMIT):
    return pltpu.CompilerParams(dimension_semantics=sem, vmem_limit_bytes=vmem)


def _rms_mod(x, m):
    ms = jnp.mean(x * x, axis=-1, keepdims=True)
    xn = (x * lax.rsqrt(ms + NORM_EPS)) * m[0:1]
    return xn * (1.0 + m[2:3]) + m[1:2]


def _split_bf16(x):
    hi = x.astype(BF16)
    lo = (x - hi.astype(F32)).astype(BF16)
    return hi, lo


def _dot(a, b):
    return jnp.dot(a, b, preferred_element_type=F32)


def _dot_nt(a, b):
    return lax.dot_general(a, b, (((1,), (1,)), ((), ())), preferred_element_type=F32)


def _bdot(a, b):
    return lax.dot_general(a, b, (((2,), (1,)), ((0,), (0,))), preferred_element_type=F32)


def _bdot_nt(a, b):
    return lax.dot_general(a, b, (((2,), (2,)), ((0,), (0,))), preferred_element_type=F32)


def _bdot_tn(a, b):
    return lax.dot_general(a, b, (((1,), (1,)), ((0,), (0,))), preferred_element_type=F32)


def _bdot3(a, b):
    ah, al = _split_bf16(a)
    bh, bl = _split_bf16(b)
    return _bdot(ah, bh) + (_bdot(ah, bl) + _bdot(al, bh))


def _silu(x):
    return x * (1.0 / (1.0 + jnp.exp(-x)))


def _modvec_kernel(c_ref, w_ref, b_ref, o_ref):
    c = c_ref[...]
    a = _silu(c)
    ah, al = _split_bf16(a)
    wh, wl = _split_bf16(w_ref[0])
    o_ref[0] = _dot(ah, wh) + (_dot(ah, wl) + _dot(al, wh)) + b_ref[0]


def _mod_vectors(c_all, w_mod, b_mod):
    rows, d = c_all.shape
    depth, _, n = w_mod.shape
    tn = 1536
    return pl.pallas_call(
        _modvec_kernel,
        out_shape=jax.ShapeDtypeStruct((depth, rows, n), F32),
        grid=(depth, n // tn),
        in_specs=[pl.BlockSpec((rows, d), lambda i, j: (0, 0)),
                  pl.BlockSpec((1, d, tn), lambda i, j: (i, 0, j)),
                  pl.BlockSpec((1, 1, tn), lambda i, j: (i, 0, j))],
        out_specs=pl.BlockSpec((1, rows, tn), lambda i, j: (i, 0, j)),
        compiler_params=_cparams(("arbitrary", "arbitrary")),
        name="mod_vectors",
    )(c_all, w_mod, b_mod.reshape(depth, 1, n))


def _modmm_kernel(x_ref, m_ref, w_ref, o_ref, *, ncol):
    h = _rms_mod(x_ref[...], m_ref[0]).astype(BF16)
    n = w_ref.shape[1]
    for n0 in range(0, n, ncol):
        o_ref[:, n0:n0 + ncol] = _dot(h, w_ref[:, n0:n0 + ncol]).astype(o_ref.dtype)


def _modmm(x, mod3, w, rows_per_seg, out_dtype=F32, tl=512):
    r, d = x.shape
    n = w.shape[1]
    tps = rows_per_seg // tl
    ncol = 512 if n % 512 == 0 else n
    return pl.pallas_call(
        functools.partial(_modmm_kernel, ncol=ncol),
        out_shape=jax.ShapeDtypeStruct((r, n), out_dtype),
        grid=(r // tl,),
        in_specs=[pl.BlockSpec((tl, d), lambda i: (i, 0)),
                  pl.BlockSpec((1, 3, d), lambda i: (i // tps, 0, 0)),
                  pl.BlockSpec((d, n), lambda i: (0, 0))],
        out_specs=pl.BlockSpec((tl, n), lambda i: (i, 0)),
        compiler_params=_cparams(("arbitrary",)),
        name="mod_matmul",
    )(x, mod3, w)


def _mmres_kernel(a_ref, w_ref, x_ref, g_ref, o_ref):
    o_ref[...] = x_ref[...] + g_ref[0] * _dot(a_ref[...], w_ref[...])


def _mm_res(a, w, x, gate, rows_per_seg, tl=512):
    r, k = a.shape
    d = w.shape[1]
    tps = rows_per_seg // tl
    return pl.pallas_call(
        _mmres_kernel,
        out_shape=jax.ShapeDtypeStruct((r, d), F32),
        grid=(r // tl,),
        in_specs=[pl.BlockSpec((tl, k), lambda i: (i, 0)),
                  pl.BlockSpec((k, d), lambda i: (0, 0)),
                  pl.BlockSpec((tl, d), lambda i: (i, 0)),
                  pl.BlockSpec((1, 1, d), lambda i: (i // tps, 0, 0))],
        out_specs=pl.BlockSpec((tl, d), lambda i: (i, 0)),
        compiler_params=_cparams(("arbitrary",)),
        name="matmul_residual",
    )(a, w, x, gate)


def _router_kernel(xa_ref, xb_ref, m_ref, wh_ref, wl_ref, b_ref, h_ref, route_ref, cnt_ref, carry_ref, *, na):
    i = pl.program_id(0)

    @pl.when(i == 0)
    def _():
        carry_ref[...] = jnp.zeros_like(carry_ref)

    h = _rms_mod(jnp.where(i < na, xa_ref[...], xb_ref[...]), m_ref[0])
    hh, hl = _split_bf16(h)
    h_ref[...] = hh
    wh = wh_ref[...]
    logits = _dot(hh, wh) + (_dot(hl, wh) + _dot(hh, wl_ref[...])) + b_ref[...]
    t = logits.shape[0]
    lane = lax.broadcasted_iota(jnp.int32, logits.shape, 1).astype(F32)
    neg = -jnp.inf
    big = 1e9

    def first_argmax(vals):
        m = jnp.max(vals, axis=1, keepdims=True)
        idx = jnp.min(jnp.where(vals == m, lane, big), axis=1, keepdims=True)
        return m, idx

    lg = jnp.where(lane < MOE_GROUPS, logits, neg)
    mg, gsel = first_argmax(lg)
    wg = 1.0 / jnp.sum(jnp.exp(lg - mg), axis=1, keepdims=True)
    e_first = MOE_GROUPS + gsel * MOE_EPG
    in_grp = (lane >= e_first) & (lane < e_first + MOE_EPG)
    le = jnp.where(in_grp, logits, neg)
    m1, i1 = first_argmax(le)
    m2, i2 = first_argmax(jnp.where(lane == i1, neg, le))
    p = jnp.exp(m2 - m1)
    w1 = wg / (1.0 + p)
    w2 = wg * p / (1.0 + p)
    l1 = i1 - e_first
    l2 = i2 - e_first
    lo = jnp.minimum(l1, l2)
    hi = jnp.maximum(l1, l2)
    first_is_lo = l1 < l2
    wlo = jnp.where(first_is_lo, w1, w2)
    whi = jnp.where(first_is_lo, w2, w1)
    cls = gsel * MOE_PAIRS + lo * (2 * MOE_EPG - 1 - lo) * 0.5 + (hi - lo - 1.0)

    onehot = lane == cls
    ri = lax.broadcasted_iota(jnp.int32, (t, t), 0)
    ci = lax.broadcasted_iota(jnp.int32, (t, t), 1)
    before = (ci < ri).astype(BF16)
    prefix = _dot(before, onehot.astype(BF16)) + carry_ref[...]
    rank = jnp.sum(jnp.where(onehot, prefix, 0.0), axis=1, keepdims=True)
    carry = carry_ref[...] + jnp.sum(onehot.astype(F32), axis=0, keepdims=True)
    carry_ref[...] = carry
    cnt_ref[...] = carry
    route_ref[...] = jnp.where(lane == 0, cls, jnp.where(lane == 1, rank, jnp.where(
        lane == 2, wlo, jnp.where(lane == 3, whi, 0.0))))


def _router(xa, xb, mod3, w_r, b_r, rows_per_seg_a, tl=512):
    ra, d = xa.shape
    rb = 0 if xb is None else xb.shape[0]
    xb = xa if xb is None else xb
    r = ra + rb
    na = ra // tl
    tps = rows_per_seg_a // tl
    nseg = mod3.shape[0]
    wh, wl = _split_bf16(w_r)
    return pl.pallas_call(
        functools.partial(_router_kernel, na=na),
        out_shape=(jax.ShapeDtypeStruct((r, d), BF16),
                   jax.ShapeDtypeStruct((r, LANES), F32),
                   jax.ShapeDtypeStruct((1, LANES), F32)),
        grid=(r // tl,),
        in_specs=[pl.BlockSpec((tl, d), lambda i: (jnp.minimum(i, na - 1), 0)),
                  pl.BlockSpec((tl, d), lambda i: (jnp.maximum(i - na, 0), 0)),
                  pl.BlockSpec((1, 3, d), lambda i: (jnp.minimum(i // tps, nseg - 1), 0, 0)),
                  pl.BlockSpec((d, LANES), lambda i: (0, 0)),
                  pl.BlockSpec((d, LANES), lambda i: (0, 0)),
                  pl.BlockSpec((1, LANES), lambda i: (0, 0))],
        out_specs=(pl.BlockSpec((tl, d), lambda i: (i, 0)),
                   pl.BlockSpec((tl, LANES), lambda i: (i, 0)),
                   pl.BlockSpec((1, LANES), lambda i: (0, 0))),
        scratch_shapes=[pltpu.VMEM((1, LANES), F32)],
        compiler_params=_cparams(("arbitrary",)),
        name="moe_router",
    )(xa, xb, mod3, wh, wl, b_r)


def _moe_kernel(elo_ref, ehi_ref, val_ref, x_ref, wt_ref, wgl, wul, wdl, wgh, wuh, wdh, o_ref):
    t = pl.program_id(0)

    @pl.when(val_ref[t] != 0)
    def _():
        x = x_ref[...]
        wt = wt_ref[...]
        w16 = lambda ref: ref[0].astype(BF16)
        gl, ul, gh, uh = _dot(x, w16(wgl)), _dot(x, w16(wul)), _dot(x, w16(wgh)), _dot(x, w16(wuh))
        hl = (_silu(gl) * ul) * wt[:, 0:1]
        hh = (_silu(gh) * uh) * wt[:, 1:2]
        o_ref[...] = (_dot(hl.astype(BF16), w16(wdl)) + _dot(hh.astype(BF16), w16(wdh))).astype(o_ref.dtype)

    @pl.when(val_ref[t] == 0)
    def _():
        o_ref[...] = jnp.zeros_like(o_ref)


def _moe_experts(xs, wts, tile_elo, tile_ehi, tile_val, w_gate, w_up, w_down):
    p, d = xs.shape
    ff = w_gate.shape[2]
    nt = p // MOE_TM
    lo = lambda t, elo, ehi, val: (elo[t], 0, 0)
    hi = lambda t, elo, ehi, val: (ehi[t], 0, 0)
    row = lambda t, elo, ehi, val: (t, 0)
    grid_spec = pltpu.PrefetchScalarGridSpec(
        num_scalar_prefetch=3,
        grid=(nt,),
        in_specs=[pl.BlockSpec((MOE_TM, d), row),
                  pl.BlockSpec((MOE_TM, 2), row),
                  pl.BlockSpec((1, d, ff), lo), pl.BlockSpec((1, d, ff), lo), pl.BlockSpec((1, ff, d), lo),
                  pl.BlockSpec((1, d, ff), hi), pl.BlockSpec((1, d, ff), hi), pl.BlockSpec((1, ff, d), hi)],
        out_specs=pl.BlockSpec((MOE_TM, d), row),
    )
    return pl.pallas_call(
        _moe_kernel,
        out_shape=jax.ShapeDtypeStruct((p, d), BF16),
        grid_spec=grid_spec,
        compiler_params=_cparams(("arbitrary",)),
        name="moe_experts",
    )(tile_elo, tile_ehi, tile_val, xs, wts, w_gate, w_up, w_down, w_gate, w_up, w_down)


def _gated_add_kernel(x_ref, y_ref, g_ref, o_ref):
    o_ref[...] = x_ref[...] + g_ref[0] * y_ref[...].astype(F32)


def _gated_add(x, y, gate, rows_per_seg, tl=1024):
    r, d = x.shape
    tps = rows_per_seg // tl
    row = pl.BlockSpec((tl, d), lambda i: (i, 0))
    return pl.pallas_call(
        _gated_add_kernel,
        out_shape=jax.ShapeDtypeStruct((r, d), F32),
        grid=(r // tl,),
        in_specs=[row, row, pl.BlockSpec((1, 1, d), lambda i: (i // tps, 0, 0))],
        out_specs=row,
        compiler_params=_cparams(("arbitrary",)),
        name="moe_combine",
    )(x, y, gate)


def _class_tables():
    elo = np.zeros((MOE_CLASSES,), np.int32)
    ehi = np.zeros((MOE_CLASSES,), np.int32)
    for g in range(MOE_GROUPS):
        for lo in range(MOE_EPG):
            for hi in range(lo + 1, MOE_EPG):
                c = g * MOE_PAIRS + lo * (2 * MOE_EPG - 1 - lo) // 2 + (hi - lo - 1)
                elo[c] = g * MOE_EPG + lo
                ehi[c] = g * MOE_EPG + hi
    return elo, ehi


_CLASS_ELO, _CLASS_EHI = _class_tables()


def _hier_moe_residual(x, mod3, gate, rows_per_seg, w_gr, b_gr, w_er, b_er, w_gate, w_up, w_down,
                       xc=None, mod3c=None, gatec=None):
    ra, d = x.shape
    r = ra + (0 if xc is None else xc.shape[0])
    w_r = jnp.zeros((d, LANES), F32).at[:, :MOE_GROUPS].set(w_gr)
    w_r = w_r.at[:, MOE_GROUPS:MOE_GROUPS + MOE_GROUPS * MOE_EPG].set(w_er)
    b_r = jnp.zeros((1, LANES), F32).at[0, :MOE_GROUPS].set(b_gr)
    b_r = b_r.at[0, MOE_GROUPS:MOE_GROUPS + MOE_GROUPS * MOE_EPG].set(b_er.reshape(-1))
    mods = mod3 if xc is None else jnp.concatenate([mod3, mod3c], axis=0)
    h, route, cnt = _router(x, xc, mods, w_r, b_r, rows_per_seg)

    cls = route[:, 0:1].astype(jnp.int32)
    rank = route[:, 1].astype(jnp.int32)
    counts = cnt[0].astype(jnp.int32)
    pc = ((counts + MOE_TM - 1) // MOE_TM) * MOE_TM
    pend = jnp.cumsum(pc)[:MOE_CLASSES]
    pstart = jnp.cumsum(pc) - pc
    dest = rank + jnp.sum(jnp.where(cls == jnp.arange(LANES, dtype=jnp.int32)[None, :], pstart[None, :], 0), axis=1)
    p_rows = ((r + MOE_CLASSES * (MOE_TM - 1)) // MOE_TM + 1) * MOE_TM
    nt = p_rows // MOE_TM
    vals = jnp.stack([jnp.arange(r, dtype=F32), route[:, 2], route[:, 3]], axis=1)
    pad_src = (jnp.arange(p_rows, dtype=jnp.int32) % r).astype(F32)
    tab0 = jnp.stack([pad_src, jnp.zeros((p_rows,), F32), jnp.zeros((p_rows,), F32)], axis=1)
    tab = tab0.at[dest].set(vals)
    src = tab[:, 0].astype(jnp.int32)
    wts = tab[:, 1:3]
    tile_p0 = jnp.arange(nt, dtype=jnp.int32) * MOE_TM
    tile_val = (tile_p0 < pend[-1]).astype(jnp.int32)
    last_cls = jnp.searchsorted(pend, pend[-1] - 1, side="right")
    tile_cls = jnp.where(tile_val != 0, jnp.searchsorted(pend, tile_p0, side="right"), last_cls)
    tile_cls = jnp.clip(tile_cls, 0, MOE_CLASSES - 1).astype(jnp.int32)
    tile_elo = jnp.asarray(_CLASS_ELO)[tile_cls]
    tile_ehi = jnp.asarray(_CLASS_EHI)[tile_cls]

    xs = jnp.take(h, src, axis=0)
    ys = _moe_experts(xs, wts, tile_elo, tile_ehi, tile_val, w_gate, w_up, w_down)
    y = jnp.take(ys, dest, axis=0)
    x_new = _gated_add(x, y, gate, rows_per_seg)
    if xc is None:
        return x_new
    return x_new, xc + gatec[0] * y[ra:]


GDN_HB = 4
GDN_NB = 4


def _seq_conv3(z, w, row):
    n = z.shape[0]
    zp = jnp.where(row == 0, 0.0, pltpu.roll(z, 1, 0))
    zn = jnp.where(row == n - 1, 0.0, pltpu.roll(z, n - 1, 0))
    return w[0:1] * zp + w[1:2] * z + w[2:3] * zn


def _gdn_proj_kernel(x_ref, m_ref, w_ref, wab_ref, cw_ref, cos_ref, sin_ref, al_ref, dt_ref,
                     o_ref, g_ref, h_s, *, rope):
    j = pl.program_id(1)
    n = x_ref.shape[1]
    dk = GDN_DK

    @pl.when(j == 0)
    def _():
        h_s[...] = _rms_mod(x_ref[0], m_ref[0]).astype(BF16)

    row = lax.broadcasted_iota(jnp.int32, (n, 1), 0)

    def conv_silu():
        return _silu(_seq_conv3(_dot(h_s[...], w_ref[...]), cw_ref[...], row))

    @pl.when(j < 4)
    def _():
        z = conv_silu()
        qscale = jnp.where(j < 2, dk ** -0.5, 1.0).astype(F32)
        for hh in range(z.shape[1] // dk):
            zh = z[:, hh * dk:(hh + 1) * dk]
            t = (zh * lax.rsqrt(jnp.sum(zh * zh, axis=-1, keepdims=True) + NORM_EPS)) * qscale
            if rope:
                t = t * cos_ref[...] + pltpu.roll(t, dk // 2, 1) * sin_ref[...]
            o_ref[0, :, hh * dk:(hh + 1) * dk] = t.astype(o_ref.dtype)

    @pl.when((j >= 4) & (j < 6))
    def _():
        o_ref[0] = conv_silu().astype(o_ref.dtype)

    @pl.when((j >= 6) & (j < 8))
    def _():
        o_ref[0] = _dot(h_s[...], w_ref[...]).astype(o_ref.dtype)

    @pl.when(j == 8)
    def _():
        ab = _dot(h_s[...], wab_ref[...])
        lane = lax.broadcasted_iota(jnp.int32, ab.shape, 1) % LANES
        xs = ab + dt_ref[...]
        softplus = jnp.maximum(xs, 0.0) + jnp.log(1.0 + jnp.exp(-jnp.abs(xs)))
        g = -jnp.exp(al_ref[...]) * softplus
        beta = 1.0 / (1.0 + jnp.exp(-ab))
        g_ref[0] = jnp.where(lane < 2 * GDN_HB, g, jnp.where(lane < 4 * GDN_HB, beta, 0.0))


def _gdn_proj(x3, mod3, w_main, w_ab, conv_w, cosf, sinf, a_log_l, dt_l, rope):
    bn, n, d = x3.shape
    segs = mod3.shape[0]
    tc = 512
    nq = w_main.shape[1] // tc
    gl = w_ab.shape[1]
    mi = (lambda b, j: (b, 0, 0)) if segs > 1 else (lambda b, j: (0, 0, 0))
    const = lambda b, j: (0, 0)
    return pl.pallas_call(
        functools.partial(_gdn_proj_kernel, rope=rope),
        out_shape=(jax.ShapeDtypeStruct((bn, n, w_main.shape[1]), BF16),
                   jax.ShapeDtypeStruct((bn, n, gl), F32)),
        grid=(bn, nq + 1),
        in_specs=[pl.BlockSpec((1, n, d), lambda b, j: (b, 0, 0)),
                  pl.BlockSpec((1, 3, d), mi),
                  pl.BlockSpec((d, tc), lambda b, j: (0, jnp.minimum(j, nq - 1))),
                  pl.BlockSpec((d, gl), const),
                  pl.BlockSpec((3, tc), lambda b, j: (0, jnp.minimum(j, 5))),
                  pl.BlockSpec((n, GDN_DK), const), pl.BlockSpec((n, GDN_DK), const),
                  pl.BlockSpec((1, gl), const), pl.BlockSpec((1, gl), const)],
        out_specs=(pl.BlockSpec((1, n, tc), lambda b, j: (b, 0, jnp.minimum(j, nq - 1))),
                   pl.BlockSpec((1, n, gl), lambda b, j: (b, 0, 0))),
        scratch_shapes=[pltpu.VMEM((n, d), BF16)],
        compiler_params=_cparams(("arbitrary", "arbitrary")),
        name="gdn_proj",
    )(x3, mod3, w_main, w_ab, conv_w, cosf, sinf, a_log_l, dt_l)


def _gdn_scan_kernel(ql, kl, vl, gl, qc, kc, vc, gx, ol, oc, s_ref, u_s, w_s, at_s, qd_s, kd_s, la_s, *, hb):
    cs = GDN_CHUNK
    dk = GDN_DK
    ns = 2 * hb
    n_lat = ql.shape[1]
    n_ctx = qc.shape[1]
    s_ref[...] = jnp.zeros_like(s_ref)
    ol[...] = jnp.zeros_like(ol)
    oc[...] = jnp.zeros_like(oc)
    ii = lax.broadcasted_iota(jnp.int32, (cs, cs), 0)
    jj = lax.broadcasted_iota(jnp.int32, (cs, cs), 1)
    eye = (ii == jj)[None]
    lower = (ii >= jj)[None]
    upper = (ii <= jj)[None]
    eye_f = eye.astype(F32)

    def run_window(qr, kr, vr, gr, orf, t0f, t0b, nb):
        b2 = ns * nb
        bi = lax.broadcasted_iota(jnp.int32, (b2, cs, cs), 0)
        sign = 1 - 2 * ((bi // nb) % 2)
        dij = sign * (lax.broadcasted_iota(jnp.int32, (b2, cs, cs), 1)
                      - lax.broadcasted_iota(jnp.int32, (b2, cs, cs), 2))
        tri = dij >= 0
        stri = dij > 0

        def rev(a):
            return jnp.concatenate([a[c:c + 1] for c in reversed(range(nb))], axis=0)

        def gates(t0, d):
            slab = gr[0, pl.ds(t0, nb * cs), :].reshape(nb, cs, LANES)
            g1 = slab.astype(BF16)
            r1 = slab - g1.astype(F32)
            g2 = r1.astype(BF16)
            g3 = (r1 - g2.astype(F32)).astype(BF16)
            t = jnp.broadcast_to((lower if d == 0 else upper).astype(BF16), (nb, cs, cs))
            gcum = _bdot(t, g1) + (_bdot(t, g2) + _bdot(t, g3))
            gtot = jnp.sum(slab, axis=1, keepdims=True)
            return (slab, gcum, gtot) if d == 0 else (rev(slab), rev(gcum), rev(gtot))

        gate_d = (gates(t0f, 0), gates(t0b, 1))
        gc_l, be_l, gt_l, q_l, k_l, v_l = [], [], [], [], [], []
        for h in range(hb):
            for d in range(2):
                slab, gcum, gtot = gate_d[d]
                lg = d * hb + h
                lb = 2 * hb + lg
                gc_l.append(gcum[:, :, lg:lg + 1])
                be_l.append(slab[:, :, lb:lb + 1])
                gt_l.append(gtot[:, :, lg:lg + 1])
                t0 = t0f if d == 0 else t0b
                cols = slice(h * dk, (h + 1) * dk)
                fix = (lambda a: a) if d == 0 else rev
                q_l.append(fix(qr[0, pl.ds(t0, nb * cs), cols].reshape(nb, cs, dk)))
                k_l.append(fix(kr[0, pl.ds(t0, nb * cs), cols].reshape(nb, cs, dk)))
                v_l.append(fix(vr[0, pl.ds(t0, nb * cs), cols].reshape(nb, cs, dk)))
        gc = jnp.concatenate(gc_l, axis=0)
        be = jnp.concatenate(be_l, axis=0)
        gt = jnp.concatenate(gt_l, axis=0)
        q = jnp.concatenate(q_l, axis=0)
        k = jnp.concatenate(k_l, axis=0)
        v = jnp.concatenate(v_l, axis=0)

        gcl = gc + jnp.zeros((1, 1, cs), F32)
        gcr = jnp.sum(jnp.where(eye, gcl, 0.0), axis=1, keepdims=True)
        decay = jnp.where(tri, jnp.exp(jnp.where(tri, gcl - gcr, 0.0)), 0.0)
        a = jnp.where(stri, _bdot_nt(k, k) * be * decay, 0.0)
        mm = lambda l, r: _bdot(l.astype(BF16), r.astype(BF16))
        x = mm(a, a)
        p = eye_f - a
        p = p + mm(p, x)
        for _ in range(4):
            x = mm(x, x)
            p = p + mm(p, x)
        pb = p.astype(BF16)
        eg = jnp.exp(gc)
        kf = k.astype(F32)
        u_s[:, :nb] = _bdot(pb, (v.astype(F32) * be).astype(BF16)).reshape(ns, nb, cs, dk)
        w_s[:, :nb] = _bdot(pb, (kf * (be * eg)).astype(BF16)).astype(BF16).reshape(ns, nb, cs, dk)
        at_s[:, :nb] = jnp.where(tri, _bdot_nt(q, k) * decay, 0.0).astype(BF16).reshape(ns, nb, cs, cs)
        qd_s[:, :nb] = (q.astype(F32) * eg).astype(BF16).reshape(ns, nb, cs, dk)
        kd_s[:, :nb] = (kf * jnp.exp(gt - gc)).astype(BF16).reshape(ns, nb, cs, dk)
        la_s[:, :nb] = (jnp.exp(gt) + jnp.zeros((1, 1, dk), F32)).reshape(ns, nb, 1, dk)

        for j in range(nb):
            s = s_ref[...]
            sbf = s.astype(BF16)
            v_new = u_s[:, j] - _bdot(w_s[:, j], sbf)
            vb = v_new.astype(BF16)
            o = _bdot(qd_s[:, j], sbf) + _bdot(at_s[:, j], vb)
            s_ref[...] = s * la_s[:, j] + _bdot_tn(kd_s[:, j], vb)
            for h in range(hb):
                cols = slice(h * dk, (h + 1) * dk)
                orf[0, pl.ds(t0f + j * cs, cs), cols] += o[2 * h]
                orf[0, pl.ds(t0b + (nb - 1 - j) * cs, cs), cols] += o[2 * h + 1]

    run_window(qc, kc, vc, gx, oc, 0, 0, n_ctx // cs)
    wrows = GDN_NB * cs
    nwin = n_lat // wrows

    def body(wi, carry):
        t0f = pl.multiple_of(wi * wrows, wrows)
        t0b = pl.multiple_of((nwin - 1 - wi) * wrows, wrows)
        run_window(ql, kl, vl, gl, ol, t0f, t0b, GDN_NB)
        return carry

    lax.fori_loop(0, nwin, body, 0)


def _gdn_scan(pl_, gl_, pc_, gc_):
    bn, n_lat, _ = pl_.shape
    n_ctx = pc_.shape[1]
    hb, dk, cs = GDN_HB, GDN_DK, GDN_CHUNK
    hd = GDN_HEADS * dk
    ngrp = GDN_HEADS // hb
    wcol = hb * dk
    nbm = max(GDN_NB, n_ctx // cs)

    def sec(n, k):
        return pl.BlockSpec((1, n, wcol), lambda b, g: (b, 0, k * ngrp + g))

    gate = lambda n: pl.BlockSpec((1, n, LANES), lambda b, g: (b, 0, g))
    out = lambda n: pl.BlockSpec((1, n, wcol), lambda b, g: (b, 0, g))
    return pl.pallas_call(
        functools.partial(_gdn_scan_kernel, hb=hb),
        out_shape=(jax.ShapeDtypeStruct((bn, n_lat, hd), F32), jax.ShapeDtypeStruct((bn, n_ctx, hd), F32)),
        grid=(bn, ngrp),
        in_specs=[sec(n_lat, 0), sec(n_lat, 1), sec(n_lat, 2), gate(n_lat),
                  sec(n_ctx, 0), sec(n_ctx, 1), sec(n_ctx, 2), gate(n_ctx)],
        out_specs=(out(n_lat), out(n_ctx)),
        scratch_shapes=[pltpu.VMEM((2 * hb, dk, dk), F32),
                        pltpu.VMEM((2 * hb, nbm, cs, dk), F32),
                        pltpu.VMEM((2 * hb, nbm, cs, dk), BF16),
                        pltpu.VMEM((2 * hb, nbm, cs, cs), BF16),
                        pltpu.VMEM((2 * hb, nbm, cs, dk), BF16),
                        pltpu.VMEM((2 * hb, nbm, cs, dk), BF16),
                        pltpu.VMEM((2 * hb, nbm, 1, dk), F32)],
        compiler_params=_cparams(("arbitrary", "arbitrary")),
        name="gdn_scan",
    )(pl_, pl_, pl_, gl_, pc_, pc_, pc_, gc_)


def _gdn_out_kernel(o_ref, z_ref, ng_ref, w_ref, x_ref, g_ref, y_ref):
    dk = GDN_DK
    parts = []
    for h in range(o_ref.shape[1] // dk):
        cols = slice(h * dk, (h + 1) * dk)
        oh = o_ref[:, cols]
        ms = jnp.mean(oh * oh, axis=-1, keepdims=True)
        on = (oh * lax.rsqrt(ms + NORM_EPS)) * ng_ref[...]
        parts.append((on * _silu(z_ref[:, cols].astype(F32))).astype(BF16))
    a = jnp.concatenate(parts, axis=1)
    y_ref[...] = x_ref[...] + g_ref[0] * _dot(a, w_ref[...])


def _gdn_out(o, proj, norm_g, w_out, x, gate, rows_per_seg, tl=512):
    r, hd = o.shape
    d = w_out.shape[1]
    tps = rows_per_seg // tl
    return pl.pallas_call(
        _gdn_out_kernel,
        out_shape=jax.ShapeDtypeStruct((r, d), F32),
        grid=(r // tl,),
        in_specs=[pl.BlockSpec((tl, hd), lambda i: (i, 0)),
                  pl.BlockSpec((tl, hd), lambda i: (i, 3)),
                  pl.BlockSpec((1, GDN_DK), lambda i: (0, 0)),
                  pl.BlockSpec((hd, d), lambda i: (0, 0)),
                  pl.BlockSpec((tl, d), lambda i: (i, 0)),
                  pl.BlockSpec((1, 1, d), lambda i: (i // tps, 0, 0))],
        out_specs=pl.BlockSpec((tl, d), lambda i: (i, 0)),
        compiler_params=_cparams(("arbitrary",)),
        name="gdn_out",
    )(o, proj, norm_g, w_out, x, gate)


def _proj3_kernel(x_ref, m_ref, w0_ref, w1_ref, w2_ref, cw_ref, cb_ref, *rest, hyena):
    j = pl.program_id(1)
    o_refs, h_s = rest[:-1], rest[-1]
    n = x_ref.shape[1]

    @pl.when(j == 0)
    def _():
        h_s[...] = _rms_mod(x_ref[0], m_ref[0]).astype(BF16)

    h = h_s[...]
    row = lax.broadcasted_iota(jnp.int32, (n, 1), 0)
    z0 = _dot(h, w0_ref[...])
    z1 = _dot(h, w1_ref[...])
    z2 = _dot(h, w2_ref[...])
    if hyena:
        cb = cb_ref[...]
        x0 = _seq_conv3(z0, cw_ref[0], row) + cb[0:1]
        x1 = _seq_conv3(z1, cw_ref[1], row) + cb[1:2]
        v = _seq_conv3(z2, cw_ref[2], row) + cb[2:3]
        o_refs[0][0] = (v * x1).astype(BF16)
        o_refs[1][0] = x0.astype(BF16)
    else:
        o_refs[0][0] = (z1 * _seq_conv3(z2 * z0, cw_ref[0], row)).astype(BF16)


def _proj3(x3, mod3, w_in, cw, cb, hyena, tc=256):
    bn, n, d = x3.shape
    segs = mod3.shape[0]
    nj = d // tc
    mi = (lambda b, j: (b, 0, 0)) if segs > 1 else (lambda b, j: (0, 0, 0))
    wspec = lambda k: pl.BlockSpec((d, tc), lambda b, j: (0, k * nj + j))
    seq = pl.BlockSpec((1, n, tc), lambda b, j: (b, 0, j))
    n_out = 2 if hyena else 1
    outs = pl.pallas_call(
        functools.partial(_proj3_kernel, hyena=hyena),
        out_shape=tuple(jax.ShapeDtypeStruct((bn, n, d), BF16) for _ in range(n_out)),
        grid=(bn, nj),
        in_specs=[pl.BlockSpec((1, n, d), lambda b, j: (b, 0, 0)),
                  pl.BlockSpec((1, 3, d), mi),
                  wspec(0), wspec(1), wspec(2),
                  pl.BlockSpec((cw.shape[0], 3, tc), lambda b, j: (0, 0, j)),
                  pl.BlockSpec((3, tc), lambda b, j: (0, j))],
        out_specs=tuple(seq for _ in range(n_out)),
        scratch_shapes=[pltpu.VMEM((n, d), BF16)],
        compiler_params=_cparams(("arbitrary", "arbitrary")),
        name="hyena_proj" if hyena else "shortconv_proj",
    )(x3, mod3, w_in, w_in, w_in, cw, cb)
    return outs


def _final_norm_kernel(x_ref, g_ref, o_ref):
    x = x_ref[...]
    ms = jnp.mean(x * x, axis=-1, keepdims=True)
    o_ref[...] = (x * lax.rsqrt(ms + NORM_EPS)) * g_ref[...]


def _final_norm(x, g, tl=1024):
    r, d = x.shape
    return pl.pallas_call(
        _final_norm_kernel,
        out_shape=jax.ShapeDtypeStruct((r, d), F32),
        grid=(r // tl,),
        in_specs=[pl.BlockSpec((tl, d), lambda i: (i, 0)), pl.BlockSpec((1, d), lambda i: (0, 0))],
        out_specs=pl.BlockSpec((tl, d), lambda i: (i, 0)),
        compiler_params=_cparams(("arbitrary",)),
        name="final_norm",
    )(x, g)


def _hyena_kernel(u_ref, x0_ref, c_ref, s_ref, kr_ref, ki_ref, kn_ref, fb_ref, o_ref):
    ub = u_ref[0]
    l = ub.shape[0]
    cm = c_ref[...]
    sm = s_ref[...]
    a = _dot(cm, ub)
    b = _dot(sm, ub)
    kr = kr_ref[...]
    ki = ki_ref[...]
    zr = (a * kr + b * ki).astype(BF16)
    zi = (b * kr - a * ki).astype(BF16)
    y = _dot(cm, zr) + _dot(sm, zi)
    uf = ub.astype(F32)
    tpar = lax.broadcasted_iota(jnp.int32, (l, 1), 0) & 1
    alt = (1 - 2 * tpar).astype(F32)
    un = jnp.sum(uf * alt, axis=0, keepdims=True)
    y = y + alt * (un * kn_ref[...]) + uf * fb_ref[...]
    o_ref[0] = (y * x0_ref[0].astype(F32)).astype(o_ref.dtype)


def _hyena_conv(u, x0, cmat, smat, kr, ki, kn, fbias, td=256):
    b, l, d = u.shape
    seq = pl.BlockSpec((1, l, td), lambda i, j: (i, 0, j))
    mat = pl.BlockSpec((l, l), lambda i, j: (0, 0), pipeline_mode=pl.Buffered(1))
    spec = pl.BlockSpec((l, td), lambda i, j: (0, j))
    vec = pl.BlockSpec((1, td), lambda i, j: (0, j))
    return pl.pallas_call(
        _hyena_kernel,
        out_shape=jax.ShapeDtypeStruct((b, l, d), BF16),
        grid=(b, d // td),
        in_specs=[seq, seq, mat, mat, spec, spec, vec, vec],
        out_specs=seq,
        compiler_params=_cparams(("arbitrary", "arbitrary")),
        name="hyena_conv",
    )(u, x0, cmat, smat, kr, ki, kn, fbias)


def _mm3_kernel(a_ref, b_ref, o_ref):
    ah, al = _split_bf16(a_ref[...])
    bh, bl = _split_bf16(b_ref[...])
    o_ref[...] = _dot(ah, bh) + (_dot(ah, bl) + _dot(al, bh))


def _mm3(a, b, tm=256, tn=256):
    m, k = a.shape
    n = b.shape[1]
    tm = min(tm, m)
    return pl.pallas_call(
        _mm3_kernel,
        out_shape=jax.ShapeDtypeStruct((m, n), F32),
        grid=(n // tn, m // tm),
        in_specs=[pl.BlockSpec((tm, k), lambda j, i: (i, 0)),
                  pl.BlockSpec((k, tn), lambda j, i: (0, j))],
        out_specs=pl.BlockSpec((tm, tn), lambda j, i: (i, j)),
        compiler_params=_cparams(("arbitrary", "arbitrary")),
        name="matmul_f32x3",
    )(a, b)


def _na_kernel(q_ref, k_ref, v_ref, kc_ref, vc_ref, bias_ref, o_ref, *, rows):
    r = pl.program_id(1)
    rs = jnp.clip(r - NA_WIN_R // 2, 0, rows - NA_WIN_R)
    t0 = pl.multiple_of(rs * GRID_W, GRID_W)
    nk = NA_WIN_R * GRID_W
    scale = NA_DH ** -0.5
    lane = lax.broadcasted_iota(jnp.int32, (GRID_W, LANES), 1)
    first = lane < NA_DH
    scores = []
    for h in range(NA_HEADS):
        cols = slice((h // 2) * LANES, (h // 2 + 1) * LANES)
        q2 = q_ref[0, :, cols]
        qm = jnp.where(first if h % 2 == 0 else jnp.logical_not(first), q2, jnp.zeros_like(q2))
        s_loc = _dot_nt(qm, k_ref[0, pl.ds(t0, nk), cols]) * scale + bias_ref[0, h]
        s_ctx = _dot_nt(qm, kc_ref[0, :, cols]) * scale
        scores.append((s_loc, s_ctx))
    probs = []
    for s_loc, s_ctx in scores:
        m = jnp.maximum(jnp.max(s_loc, axis=1, keepdims=True), jnp.max(s_ctx, axis=1, keepdims=True))
        p_loc = jnp.exp(s_loc - m)
        p_ctx = jnp.exp(s_ctx - m)
        den = jnp.sum(p_loc, axis=1, keepdims=True) + jnp.sum(p_ctx, axis=1, keepdims=True)
        probs.append((p_loc.astype(BF16), p_ctx.astype(BF16), den))
    outs = []
    for h, (p_loc, p_ctx, den) in enumerate(probs):
        cols = slice((h // 2) * LANES, (h // 2 + 1) * LANES)
        o = _dot(p_loc, v_ref[0, pl.ds(t0, nk), cols]) + _dot(p_ctx, vc_ref[0, :, cols])
        outs.append(o / den)
    pairs = [jnp.where(first, outs[2 * hp], outs[2 * hp + 1]).astype(o_ref.dtype) for hp in range(NA_HEADS // 2)]
    o_ref[0] = jnp.concatenate(pairs, axis=1)


def _na_attention(q, k, v, kc, vc, bias_tab):
    b, l, hd = q.shape
    lc = kc.shape[1]
    rows = l // GRID_W
    half = NA_WIN_R // 2
    return pl.pallas_call(
        functools.partial(_na_kernel, rows=rows),
        out_shape=jax.ShapeDtypeStruct((b, l, hd), BF16),
        grid=(b, rows),
        in_specs=[pl.BlockSpec((1, GRID_W, hd), lambda i, r: (i, r, 0)),
                  pl.BlockSpec((1, l, hd), lambda i, r: (i, 0, 0)),
                  pl.BlockSpec((1, l, hd), lambda i, r: (i, 0, 0)),
                  pl.BlockSpec((1, lc, hd), lambda i, r: (i, 0, 0)),
                  pl.BlockSpec((1, lc, hd), lambda i, r: (i, 0, 0)),
                  pl.BlockSpec((1, NA_HEADS, GRID_W, NA_WIN_R * GRID_W),
                               lambda i, r: (r - jnp.clip(r - half, 0, rows - NA_WIN_R), 0, 0, 0))],
        out_specs=pl.BlockSpec((1, GRID_W, hd), lambda i, r: (i, r, 0)),
        compiler_params=_cparams(("arbitrary", "arbitrary")),
        name="na_attention",
    )(q, k, v, kc, vc, bias_tab)


def _na_bias_table(rpb):
    qc = np.arange(GRID_W)
    kc = np.arange(GRID_W)
    cstart = np.clip(qc - NA_WIN_C // 2, 0, GRID_W - NA_WIN_C)
    valid = (kc[None, :] >= cstart[:, None]) & (kc[None, :] < cstart[:, None] + NA_WIN_C)
    dc = np.clip(kc[None, :] - qc[:, None] + NA_WIN_C - 1, 0, 2 * NA_WIN_C - 2)
    off = np.arange(NA_WIN_R)
    j = np.arange(NA_WIN_R)
    dr = j[None, :] - off[:, None] + NA_WIN_R - 1
    sel_r = (dr[:, :, None] == np.arange(2 * NA_WIN_R - 1)).astype(np.float32)
    sel_c = (dc[:, :, None] == np.arange(2 * NA_WIN_C - 1)).astype(np.float32)
    t = jnp.einsum("ojr,hrc,qkc->ohqjk", sel_r, rpb.astype(F32), sel_c, precision=lax.Precision.HIGHEST)
    t = jnp.where(jnp.asarray(valid)[None, None, :, None, :], t, jnp.float32(-1e30))
    return t.reshape(NA_WIN_R, NA_HEADS, GRID_W, NA_WIN_R * GRID_W)


def _rope_tables(n_tok, dh):
    pos = jnp.arange(n_tok)
    row = (pos // GRID_W).astype(F32)
    col = (pos % GRID_W).astype(F32)
    n_freq = dh // 4
    inv = ROPE_BASE ** (-jnp.arange(n_freq, dtype=F32) / n_freq)
    ang = jnp.concatenate([row[:, None] * inv, col[:, None] * inv], axis=-1)
    cos, sin = jnp.cos(ang), jnp.sin(ang)
    return jnp.concatenate([cos, cos], axis=-1), jnp.concatenate([-sin, sin], axis=-1)


def _gdn_gate_lanes():
    ngrp = GDN_HEADS // GDN_HB
    src = -np.ones((ngrp * LANES,), np.int64)
    for grp in range(ngrp):
        for t in range(2):
            for dr in range(2):
                for hh in range(GDN_HB):
                    lane = grp * LANES + t * 2 * GDN_HB + dr * GDN_HB + hh
                    src[lane] = t * 2 * GDN_HEADS + dr * GDN_HEADS + grp * GDN_HB + hh
    return src


_GDN_LANE_SRC = _gdn_gate_lanes()


def _gdn_layer(x, xc, mod_l, mod_c, gate_l, gate_c, bn, w_in, conv_w, a_log, dt_bias, norm_g, w_out, ctx_out):
    d = D_MODEL
    hd = GDN_HEADS * GDN_DK
    n_lat = x.shape[0] // bn
    n_ctx = xc.shape[0] // bn
    w_main = w_in[:, :4 * hd].astype(BF16)
    used = jnp.asarray(_GDN_LANE_SRC >= 0)
    lane_src = jnp.asarray(np.maximum(_GDN_LANE_SRC, 0))
    w_ab = jnp.where(used[None, :], w_in[:, 4 * hd:][:, lane_src], 0.0).astype(BF16)
    decay_lane = used & (lane_src < 2 * GDN_HEADS)
    a_log_l = jnp.where(decay_lane, a_log.reshape(-1)[lane_src % (2 * GDN_HEADS)], 0.0)[None, :].astype(F32)
    dt_l = jnp.where(decay_lane, dt_bias.reshape(-1)[lane_src % (2 * GDN_HEADS)], 0.0)[None, :].astype(F32)
    cosf, sinf = _rope_tables(n_lat, GDN_DK)
    dummy = jnp.zeros((n_ctx, GDN_DK), F32)
    p_l, g_l = _gdn_proj(x.reshape(bn, n_lat, d), mod_l, w_main, w_ab, conv_w, cosf, sinf, a_log_l, dt_l, True)
    p_c, g_c = _gdn_proj(xc.reshape(bn, n_ctx, d), mod_c, w_main, w_ab, conv_w, dummy, dummy, a_log_l, dt_l, False)
    o_l, o_c = _gdn_scan(p_l, g_l, p_c, g_c)
    wo = w_out.astype(BF16)
    ng = norm_g.astype(F32)[None, :]
    x_new = _gdn_out(o_l.reshape(bn * n_lat, hd), p_l.reshape(bn * n_lat, 4 * hd), ng, wo, x, gate_l, n_lat)
    xc_new = None
    if ctx_out:
        xc_new = _gdn_out(o_c.reshape(bn * n_ctx, hd), p_c.reshape(bn * n_ctx, 4 * hd), ng, wo, xc, gate_c,
                          xc.shape[0])
    return x_new, xc_new


def _hyena_filter_taps(n_tok, w1, b1, w2, b2, w3, b3, freq, w4):
    t = jnp.linspace(0.0, 1.0, n_tok, dtype=F32)[:, None]
    bands = (HY_EMB_DIM - 1) // 2
    wpos = 2.0 * math.pi * jnp.arange(n_tok, dtype=F32)[:, None] / n_tok
    fr = jnp.linspace(1e-4, bands - 1, bands, dtype=F32)[None, :]
    z = jnp.concatenate([t, jnp.cos(fr * wpos), -jnp.sin(fr * wpos)], axis=-1)
    freq = freq.astype(F32)
    hdn = jnp.sin(freq[0] * (z @ w1.astype(F32) + b1.astype(F32)))
    hdn = jnp.sin(freq[1] * (hdn @ w2.astype(F32) + b2.astype(F32)))
    hdn = jnp.sin(freq[2] * (hdn @ w3.astype(F32) + b3.astype(F32)))
    filt = (hdn @ w4.astype(F32)).reshape(n_tok, 2, D_MODEL)
    deltas = jnp.abs(jnp.linspace(HY_MIN_DECAY, HY_MAX_DECAY, D_MODEL, dtype=F32))
    filt = filt * jnp.exp(-t * deltas)[:, None, :]
    return filt[:, 0], filt[:, 1]


def _dft_mats(l):
    k = jnp.arange(l, dtype=jnp.int32)
    ks = (k[:, None] * k[None, :]) % (2 * l)
    ang = ks.astype(F32) * (math.pi / l)
    return jnp.cos(ang), jnp.sin(ang)


def _hyena_layer(rows, mod3, gate, rows_per_seg, bn, w_in, short_w, short_b, f_w1, f_b1, f_w2, f_b2, f_w3, f_b3,
                 f_freq, f_w4, f_bias, w_out):
    d = D_MODEL
    n_tok = rows.shape[0] // bn
    cw = short_w.reshape(3, 3, d).transpose(1, 0, 2).astype(F32)
    u, x0 = _proj3(rows.reshape(bn, n_tok, d), mod3, w_in.astype(BF16), cw, short_b.reshape(3, d).astype(F32),
                   hyena=True)
    hf, hb = _hyena_filter_taps(n_tok, f_w1, f_b1, f_w2, f_b2, f_w3, f_b3, f_freq, f_w4)
    hb = hb.at[0].set(0.0)
    cmat, smat = _dft_mats(n_tok)
    n2 = 2 * n_tok
    wk = jnp.full((n_tok, 1), 2.0 / n2, F32).at[0, 0].set(1.0 / n2)
    kr = _mm3(cmat, hf + hb) * wk
    ki = _mm3(smat, hb - hf) * wk
    alt = (1.0 - 2.0 * (jnp.arange(n_tok) % 2)).astype(F32)[:, None]
    kn = jnp.sum((hf + hb) * alt, axis=0, keepdims=True) / n2
    y = _hyena_conv(u, x0, cmat.astype(BF16), smat.astype(BF16), kr, ki, kn, f_bias.astype(F32)[None, :])
    return _mm_res(y.reshape(rows.shape[0], d), w_out.astype(BF16), rows, gate, rows_per_seg)


def _na_layer(x, xc, mod_l, mod_c, gate_l, bn, w_qkv, rpb, w_out):
    hd = NA_HEADS * NA_DH
    n_lat = x.shape[0] // bn
    n_ctx = xc.shape[0] // bn
    wq = w_qkv.astype(BF16)
    z = _modmm(x, mod_l, wq, n_lat, out_dtype=BF16).reshape(bn, n_lat, 3 * hd)
    zc = _modmm(xc, mod_c, wq, xc.shape[0], out_dtype=BF16).reshape(bn, n_ctx, 3 * hd)
    o = _na_attention(z[..., :hd], z[..., hd:2 * hd], z[..., 2 * hd:], zc[..., hd:2 * hd], zc[..., 2 * hd:],
                      _na_bias_table(rpb))
    return _mm_res(o.reshape(x.shape[0], hd), w_out.astype(BF16), x, gate_l, n_lat)


def _shortconv_layer(rows, mod3, gate, rows_per_seg, bn, w_in, conv_w, w_out):
    d = D_MODEL
    n_tok = rows.shape[0] // bn
    (a,) = _proj3(rows.reshape(bn, n_tok, d), mod3, w_in.astype(BF16), conv_w.astype(F32)[None],
                  jnp.zeros((3, d), F32), hyena=False)
    return _mm_res(a.reshape(rows.shape[0], d), w_out.astype(BF16), rows, gate, rows_per_seg)


def kernel(x, c, ctx, c_ctx, ln_g, w_mod, b_mod, final_g, gdn_w_in, gdn_conv_w, gdn_a_log, gdn_dt_bias, gdn_norm_g, gdn_w_out, hy_w_in, hy_short_w, hy_short_b, hy_f_w1, hy_f_b1, hy_f_w2, hy_f_b2, hy_f_w3, hy_f_b3, hy_f_freq, hy_f_w4, hy_f_bias, hy_w_out, na_w_qkv, na_rpb, na_w_out, sc_w_in, sc_conv_w, sc_w_out, moe_w_gr, moe_b_gr, moe_w_er, moe_b_er, moe_w_gate, moe_w_up, moe_w_down):
    bn, n_lat, d = x.shape
    n_ctx = ctx.shape[1]
    rows_ctx = bn * n_ctx
    pad = (-(bn + 1)) % 8
    c_all = jnp.concatenate([c, c_ctx[None, :], jnp.zeros((pad, d), F32)], axis=0)
    mods = _mod_vectors(c_all, w_mod, b_mod)
    xr = x.reshape(bn * n_lat, d)
    xcr = ctx.reshape(rows_ctx, d)
    for i in range(DEPTH):
        m, s = i % N_MIXERS, i // N_MIXERS
        reads_ctx = m in CTX_READING_MIXERS
        upd_ctx = any((j % N_MIXERS) in CTX_READING_MIXERS for j in range(i + 1, DEPTH))
        mv = mods[i].reshape(-1, 6, d)

        def mod3(rows, which, norm_g):
            sh, sc = mv[rows, 3 * which], mv[rows, 3 * which + 1]
            return jnp.stack([jnp.broadcast_to(norm_g, sh.shape), sh, sc], axis=1)

        lat = slice(0, bn)
        cx = slice(bn, bn + 1)
        mod_l1, mod_l2 = mod3(lat, 0, ln_g[i, 0]), mod3(lat, 1, ln_g[i, 1])
        mod_c1, mod_c2 = mod3(cx, 0, ln_g[i, 0]), mod3(cx, 1, ln_g[i, 1])
        g1_l, g2_l = mv[lat, 2][:, None, :], mv[lat, 5][:, None, :]
        g1_c, g2_c = mv[cx, 2][:, None, :], mv[cx, 5][:, None, :]
        xc_new = None
        if m == 0:
            xr, xc_new = _gdn_layer(xr, xcr, mod_l1, mod_c1, g1_l, g1_c, bn, gdn_w_in[s], gdn_conv_w[s],
                                    gdn_a_log[s], gdn_dt_bias[s], gdn_norm_g[s], gdn_w_out[s], upd_ctx)
        elif m == 1:
            hy = (hy_w_in[s], hy_short_w[s], hy_short_b[s], hy_f_w1[s], hy_f_b1[s], hy_f_w2[s], hy_f_b2[s],
                  hy_f_w3[s], hy_f_b3[s], hy_f_freq[s], hy_f_w4[s], hy_f_bias[s], hy_w_out[s])
            xr = _hyena_layer(xr, mod_l1, g1_l, n_lat, bn, *hy)
            if upd_ctx:
                xc_new = _hyena_layer(xcr, mod_c1, g1_c, rows_ctx, bn, *hy)
        elif m == 2:
            xr = _na_layer(xr, xcr, mod_l1, mod_c1, g1_l, bn, na_w_qkv[s], na_rpb[s], na_w_out[s])
            assert not upd_ctx
        else:
            xr = _shortconv_layer(xr, mod_l1, g1_l, n_lat, bn, sc_w_in[s], sc_conv_w[s], sc_w_out[s])
            if upd_ctx:
                xc_new = _shortconv_layer(xcr, mod_c1, g1_c, rows_ctx, bn, sc_w_in[s], sc_conv_w[s], sc_w_out[s])
        moe = (moe_w_gr[i], moe_b_gr[i], moe_w_er[i], moe_b_er[i], moe_w_gate[i], moe_w_up[i], moe_w_down[i])
        if upd_ctx:
            xr, xcr = _hier_moe_residual(xr, mod_l2, g2_l, n_lat, *moe, xc=xc_new, mod3c=mod_c2, gatec=g2_c)
        else:
            xr = _hier_moe_residual(xr, mod_l2, g2_l, n_lat, *moe)
    return _final_norm(xr, final_g.astype(F32)[None, :]).reshape(bn, n_lat, d)
```

```python
import functools
import math

import numpy as np
import jax
import jax.numpy as jnp
from jax import lax
from jax.experimental import pallas as pl
from jax.experimental.pallas import tpu as pltpu

F32 = jnp.float32
BF16 = jnp.bfloat16

D_MODEL = 1024
DEPTH = 4
GRID_W = 64
N_MIXERS = 4
CTX_READING_MIXERS = (0, 2)
NORM_EPS = 1e-6

GDN_HEADS = 8
GDN_DK = 128
GDN_CHUNK = 64
ROPE_BASE = 10000.0

HY_EMB_DIM = 33
HY_DECAY_TARGET = 1e-2
HY_MAX_DECAY = math.log(HY_DECAY_TARGET) / 0.3
HY_MIN_DECAY = math.log(HY_DECAY_TARGET) / 1.5

NA_HEADS = 16
NA_DH = 64
NA_WIN_R = 8
NA_WIN_C = 16

MOE_GROUPS = 4
MOE_EPG = 8
MOE_FF = 256
MOE_PAIRS = MOE_EPG * (MOE_EPG - 1) // 2
MOE_CLASSES = MOE_GROUPS * MOE_PAIRS
MOE_TM = 256

LANES = 128
VMEM_LIMIT = 56 << 20


def _cparams(sem, vmem=VMEM_LIMIT):
    return pltpu.CompilerParams(dimension_semantics=sem, vmem_limit_bytes=vmem)


def _rms_mod(x, m):
    ms = jnp.mean(x * x, axis=-1, keepdims=True)
    xn = (x * lax.rsqrt(ms + NORM_EPS)) * m[0:1]
    return xn * (1.0 + m[2:3]) + m[1:2]


def _split_bf16(x):
    hi = x.astype(BF16)
    lo = (x - hi.astype(F32)).astype(BF16)
    return hi, lo


def _dot(a, b):
    return jnp.dot(a, b, preferred_element_type=F32)


MM_ROWS = 256


def _dot_rows(a, b):
    m = a.shape[0]
    if m <= MM_ROWS:
        return _dot(a[...], b)
    return jnp.concatenate([_dot(a[i:i + MM_ROWS], b) for i in range(0, m, MM_ROWS)], axis=0)


def _dot_nt(a, b):
    return lax.dot_general(a, b, (((1,), (1,)), ((), ())), preferred_element_type=F32)


def _bdot(a, b):
    return lax.dot_general(a, b, (((2,), (1,)), ((0,), (0,))), preferred_element_type=F32)


def _bdot_nt(a, b):
    return lax.dot_general(a, b, (((2,), (2,)), ((0,), (0,))), preferred_element_type=F32)


def _bdot_tn(a, b):
    return lax.dot_general(a, b, (((1,), (1,)), ((0,), (0,))), preferred_element_type=F32)


def _bdot3(a, b):
    ah, al = _split_bf16(a)
    bh, bl = _split_bf16(b)
    return _bdot(ah, bh) + (_bdot(ah, bl) + _bdot(al, bh))


def _silu(x):
    return x * (1.0 / (1.0 + jnp.exp(-x)))


def _modvec_kernel(c_ref, w_ref, b_ref, o_ref):
    c = c_ref[...]
    a = _silu(c)
    ah, al = _split_bf16(a)
    wh, wl = _split_bf16(w_ref[0])
    o_ref[0] = _dot(ah, wh) + (_dot(ah, wl) + _dot(al, wh)) + b_ref[0]


def _mod_vectors(c_all, w_mod, b_mod):
    rows, d = c_all.shape
    depth, _, n = w_mod.shape
    tn = 1536
    return pl.pallas_call(
        _modvec_kernel,
        out_shape=jax.ShapeDtypeStruct((depth, rows, n), F32),
        grid=(depth, n // tn),
        in_specs=[pl.BlockSpec((rows, d), lambda i, j: (0, 0)),
                  pl.BlockSpec((1, d, tn), lambda i, j: (i, 0, j)),
                  pl.BlockSpec((1, 1, tn), lambda i, j: (i, 0, j))],
        out_specs=pl.BlockSpec((1, rows, tn), lambda i, j: (i, 0, j)),
        compiler_params=_cparams(("arbitrary", "arbitrary")),
        name="mod_vectors",
    )(c_all, w_mod, b_mod.reshape(depth, 1, n))


def _modmm_kernel(x_ref, m_ref, w_ref, o_ref, *, ncol):
    h = _rms_mod(x_ref[...], m_ref[0]).astype(BF16)
    n = w_ref.shape[1]
    for n0 in range(0, n, ncol):
        o_ref[:, n0:n0 + ncol] = _dot_rows(h, w_ref[:, n0:n0 + ncol]).astype(o_ref.dtype)


def _modmm(x, mod3, w, rows_per_seg, out_dtype=F32, tl=512):
    r, d = x.shape
    n = w.shape[1]
    tps = rows_per_seg // tl
    ncol = 512 if n % 512 == 0 else n
    return pl.pallas_call(
        functools.partial(_modmm_kernel, ncol=ncol),
        out_shape=jax.ShapeDtypeStruct((r, n), out_dtype),
        grid=(r // tl,),
        in_specs=[pl.BlockSpec((tl, d), lambda i: (i, 0)),
                  pl.BlockSpec((1, 3, d), lambda i: (i // tps, 0, 0)),
                  pl.BlockSpec((d, n), lambda i: (0, 0))],
        out_specs=pl.BlockSpec((tl, n), lambda i: (i, 0)),
        compiler_params=_cparams(("arbitrary",)),
        name="mod_matmul",
    )(x, mod3, w)


def _mmres_kernel(a_ref, w_ref, x_ref, g_ref, o_ref):
    o_ref[...] = x_ref[...] + g_ref[0] * _dot_rows(a_ref, w_ref[...])


def _mm_res(a, w, x, gate, rows_per_seg, tl=512):
    r, k = a.shape
    d = w.shape[1]
    tps = rows_per_seg // tl
    return pl.pallas_call(
        _mmres_kernel,
        out_shape=jax.ShapeDtypeStruct((r, d), F32),
        grid=(r // tl,),
        in_specs=[pl.BlockSpec((tl, k), lambda i: (i, 0)),
                  pl.BlockSpec((k, d), lambda i: (0, 0)),
                  pl.BlockSpec((tl, d), lambda i: (i, 0)),
                  pl.BlockSpec((1, 1, d), lambda i: (i // tps, 0, 0))],
        out_specs=pl.BlockSpec((tl, d), lambda i: (i, 0)),
        compiler_params=_cparams(("arbitrary",)),
        name="matmul_residual",
    )(a, w, x, gate)


def _router_kernel(xa_ref, xb_ref, m_ref, wh_ref, wl_ref, b_ref, h_ref, route_ref, cnt_ref, carry_ref, *, na):
    i = pl.program_id(0)

    @pl.when(i == 0)
    def _():
        carry_ref[...] = jnp.zeros_like(carry_ref)

    h = _rms_mod(jnp.where(i < na, xa_ref[...], xb_ref[...]), m_ref[0])
    hh, hl = _split_bf16(h)
    h_ref[...] = hh
    wh = wh_ref[...]
    logits = _dot_rows(hh, wh) + (_dot_rows(hl, wh) + _dot_rows(hh, wl_ref[...])) + b_ref[...]
    t = logits.shape[0]
    lane = lax.broadcasted_iota(jnp.int32, logits.shape, 1).astype(F32)
    neg = -jnp.inf
    big = 1e9

    def first_argmax(vals):
        m = jnp.max(vals, axis=1, keepdims=True)
        idx = jnp.min(jnp.where(vals == m, lane, big), axis=1, keepdims=True)
        return m, idx

    lg = jnp.where(lane < MOE_GROUPS, logits, neg)
    mg, gsel = first_argmax(lg)
    wg = 1.0 / jnp.sum(jnp.exp(lg - mg), axis=1, keepdims=True)
    e_first = MOE_GROUPS + gsel * MOE_EPG
    in_grp = (lane >= e_first) & (lane < e_first + MOE_EPG)
    le = jnp.where(in_grp, logits, neg)
    m1, i1 = first_argmax(le)
    m2, i2 = first_argmax(jnp.where(lane == i1, neg, le))
    p = jnp.exp(m2 - m1)
    w1 = wg / (1.0 + p)
    w2 = wg * p / (1.0 + p)
    l1 = i1 - e_first
    l2 = i2 - e_first
    lo = jnp.minimum(l1, l2)
    hi = jnp.maximum(l1, l2)
    first_is_lo = l1 < l2
    wlo = jnp.where(first_is_lo, w1, w2)
    whi = jnp.where(first_is_lo, w2, w1)
    cls = gsel * MOE_PAIRS + lo * (2 * MOE_EPG - 1 - lo) * 0.5 + (hi - lo - 1.0)

    onehot = lane == cls
    ri = lax.broadcasted_iota(jnp.int32, (t, t), 0)
    ci = lax.broadcasted_iota(jnp.int32, (t, t), 1)
    before = (ci < ri).astype(BF16)
    prefix = _dot(before, onehot.astype(BF16)) + carry_ref[...]
    rank = jnp.sum(jnp.where(onehot, prefix, 0.0), axis=1, keepdims=True)
    carry = carry_ref[...] + jnp.sum(onehot.astype(F32), axis=0, keepdims=True)
    carry_ref[...] = carry
    cnt_ref[...] = carry
    route_ref[...] = jnp.where(lane == 0, cls, jnp.where(lane == 1, rank, jnp.where(
        lane == 2, wlo, jnp.where(lane == 3, whi, 0.0))))


def _router(xa, xb, mod3, w_r, b_r, rows_per_seg_a, tl=512):
    ra, d = xa.shape
    rb = 0 if xb is None else xb.shape[0]
    xb = xa if xb is None else xb
    r = ra + rb
    na = ra // tl
    tps = rows_per_seg_a // tl
    nseg = mod3.shape[0]
    wh, wl = _split_bf16(w_r)
    return pl.pallas_call(
        functools.partial(_router_kernel, na=na),
        out_shape=(jax.ShapeDtypeStruct((r, d), BF16),
                   jax.ShapeDtypeStruct((r, LANES), F32),
                   jax.ShapeDtypeStruct((1, LANES), F32)),
        grid=(r // tl,),
        in_specs=[pl.BlockSpec((tl, d), lambda i: (jnp.minimum(i, na - 1), 0)),
                  pl.BlockSpec((tl, d), lambda i: (jnp.maximum(i - na, 0), 0)),
                  pl.BlockSpec((1, 3, d), lambda i: (jnp.minimum(i // tps, nseg - 1), 0, 0)),
                  pl.BlockSpec((d, LANES), lambda i: (0, 0)),
                  pl.BlockSpec((d, LANES), lambda i: (0, 0)),
                  pl.BlockSpec((1, LANES), lambda i: (0, 0))],
        out_specs=(pl.BlockSpec((tl, d), lambda i: (i, 0)),
                   pl.BlockSpec((tl, LANES), lambda i: (i, 0)),
                   pl.BlockSpec((1, LANES), lambda i: (0, 0))),
        scratch_shapes=[pltpu.VMEM((1, LANES), F32)],
        compiler_params=_cparams(("arbitrary",)),
        name="moe_router",
    )(xa, xb, mod3, wh, wl, b_r)


def _moe_kernel(elo_ref, ehi_ref, val_ref, x_ref, wt_ref, wgl, wul, wdl, wgh, wuh, wdh, o_ref):
    t = pl.program_id(0)

    @pl.when(val_ref[t] != 0)
    def _():
        x = x_ref[...]
        wt = wt_ref[...]
        w16 = lambda ref: ref[0].astype(BF16)
        gl, ul, gh, uh = _dot(x, w16(wgl)), _dot(x, w16(wul)), _dot(x, w16(wgh)), _dot(x, w16(wuh))
        hl = (_silu(gl) * ul) * wt[:, 0:1]
        hh = (_silu(gh) * uh) * wt[:, 1:2]
        o_ref[...] = (_dot(hl.astype(BF16), w16(wdl)) + _dot(hh.astype(BF16), w16(wdh))).astype(o_ref.dtype)

    @pl.when(val_ref[t] == 0)
    def _():
        o_ref[...] = jnp.zeros_like(o_ref)


def _moe_experts(xs, wts, tile_elo, tile_ehi, tile_val, w_gate, w_up, w_down):
    p, d = xs.shape
    ff = w_gate.shape[2]
    nt = p // MOE_TM
    lo = lambda t, elo, ehi, val: (elo[t], 0, 0)
    hi = lambda t, elo, ehi, val: (ehi[t], 0, 0)
    row = lambda t, elo, ehi, val: (t, 0)
    grid_spec = pltpu.PrefetchScalarGridSpec(
        num_scalar_prefetch=3,
        grid=(nt,),
        in_specs=[pl.BlockSpec((MOE_TM, d), row),
                  pl.BlockSpec((MOE_TM, 2), row),
                  pl.BlockSpec((1, d, ff), lo), pl.BlockSpec((1, d, ff), lo), pl.BlockSpec((1, ff, d), lo),
                  pl.BlockSpec((1, d, ff), hi), pl.BlockSpec((1, d, ff), hi), pl.BlockSpec((1, ff, d), hi)],
        out_specs=pl.BlockSpec((MOE_TM, d), row),
    )
    return pl.pallas_call(
        _moe_kernel,
        out_shape=jax.ShapeDtypeStruct((p, d), BF16),
        grid_spec=grid_spec,
        compiler_params=_cparams(("arbitrary",)),
        name="moe_experts",
    )(tile_elo, tile_ehi, tile_val, xs, wts, w_gate, w_up, w_down, w_gate, w_up, w_down)


def _gated_add_kernel(x_ref, y_ref, g_ref, o_ref):
    o_ref[...] = x_ref[...] + g_ref[0] * y_ref[...].astype(F32)


def _gated_add(x, y, gate, rows_per_seg, tl=1024):
    r, d = x.shape
    tps = rows_per_seg // tl
    row = pl.BlockSpec((tl, d), lambda i: (i, 0))
    return pl.pallas_call(
        _gated_add_kernel,
        out_shape=jax.ShapeDtypeStruct((r, d), F32),
        grid=(r // tl,),
        in_specs=[row, row, pl.BlockSpec((1, 1, d), lambda i: (i // tps, 0, 0))],
        out_specs=row,
        compiler_params=_cparams(("arbitrary",)),
        name="moe_combine",
    )(x, y, gate)


def _class_tables():
    elo = np.zeros((MOE_CLASSES,), np.int32)
    ehi = np.zeros((MOE_CLASSES,), np.int32)
    for g in range(MOE_GROUPS):
        for lo in range(MOE_EPG):
            for hi in range(lo + 1, MOE_EPG):
                c = g * MOE_PAIRS + lo * (2 * MOE_EPG - 1 - lo) // 2 + (hi - lo - 1)
                elo[c] = g * MOE_EPG + lo
                ehi[c] = g * MOE_EPG + hi
    return elo, ehi


_CLASS_ELO, _CLASS_EHI = _class_tables()


def _hier_moe_residual(x, mod3, gate, rows_per_seg, w_gr, b_gr, w_er, b_er, w_gate, w_up, w_down,
                       xc=None, mod3c=None, gatec=None):
    ra, d = x.shape
    r = ra + (0 if xc is None else xc.shape[0])
    w_r = jnp.zeros((d, LANES), F32).at[:, :MOE_GROUPS].set(w_gr)
    w_r = w_r.at[:, MOE_GROUPS:MOE_GROUPS + MOE_GROUPS * MOE_EPG].set(w_er)
    b_r = jnp.zeros((1, LANES), F32).at[0, :MOE_GROUPS].set(b_gr)
    b_r = b_r.at[0, MOE_GROUPS:MOE_GROUPS + MOE_GROUPS * MOE_EPG].set(b_er.reshape(-1))
    mods = mod3 if xc is None else jnp.concatenate([mod3, mod3c], axis=0)
    h, route, cnt = _router(x, xc, mods, w_r, b_r, rows_per_seg)

    cls = route[:, 0:1].astype(jnp.int32)
    rank = route[:, 1].astype(jnp.int32)
    counts = cnt[0].astype(jnp.int32)
    pc = ((counts + MOE_TM - 1) // MOE_TM) * MOE_TM
    pend = jnp.cumsum(pc)[:MOE_CLASSES]
    pstart = jnp.cumsum(pc) - pc
    dest = rank + jnp.sum(jnp.where(cls == jnp.arange(LANES, dtype=jnp.int32)[None, :], pstart[None, :], 0), axis=1)
    p_rows = ((r + MOE_CLASSES * (MOE_TM - 1)) // MOE_TM + 1) * MOE_TM
    nt = p_rows // MOE_TM
    vals = jnp.stack([jnp.arange(r, dtype=F32), route[:, 2], route[:, 3]], axis=1)
    pad_src = (jnp.arange(p_rows, dtype=jnp.int32) % r).astype(F32)
    tab0 = jnp.stack([pad_src, jnp.zeros((p_rows,), F32), jnp.zeros((p_rows,), F32)], axis=1)
    tab = tab0.at[dest].set(vals)
    src = tab[:, 0].astype(jnp.int32)
    wts = tab[:, 1:3]
    tile_p0 = jnp.arange(nt, dtype=jnp.int32) * MOE_TM
    tile_val = (tile_p0 < pend[-1]).astype(jnp.int32)
    first_row = jnp.minimum(tile_p0, pend[-1] - 1)
    tile_cls = jnp.sum((pend[None, :] <= first_row[:, None]).astype(jnp.int32), axis=1)
    tile_cls = jnp.clip(tile_cls, 0, MOE_CLASSES - 1)
    tile_elo = jnp.asarray(_CLASS_ELO)[tile_cls]
    tile_ehi = jnp.asarray(_CLASS_EHI)[tile_cls]

    xs = jnp.take(h, src, axis=0)
    ys = _moe_experts(xs, wts, tile_elo, tile_ehi, tile_val, w_gate, w_up, w_down)
    y = jnp.take(ys, dest, axis=0)
    x_new = _gated_add(x, y, gate, rows_per_seg)
    if xc is None:
        return x_new
    return x_new, xc + gatec[0] * y[ra:]


GDN_HB = 4
GDN_NB = 4


def _seq_conv3(z, w, row):
    n = z.shape[0]
    zp = jnp.where(row == 0, 0.0, pltpu.roll(z, 1, 0))
    zn = jnp.where(row == n - 1, 0.0, pltpu.roll(z, n - 1, 0))
    return w[0:1] * zp + w[1:2] * z + w[2:3] * zn


def _gdn_proj_kernel(x_ref, m_ref, w_ref, wab_ref, cw_ref, cos_ref, sin_ref, al_ref, dt_ref,
                     o_ref, g_ref, h_s, *, rope):
    j = pl.program_id(1)
    n = x_ref.shape[1]
    dk = GDN_DK

    @pl.when(j == 0)
    def _():
        h_s[...] = _rms_mod(x_ref[0], m_ref[0]).astype(BF16)

    row = lax.broadcasted_iota(jnp.int32, (n, 1), 0)

    def conv_silu():
        return _silu(_seq_conv3(_dot_rows(h_s, w_ref[...]), cw_ref[...], row))

    @pl.when(j < 4)
    def _():
        z = conv_silu()
        qscale = jnp.where(j < 2, dk ** -0.5, 1.0).astype(F32)
        for hh in range(z.shape[1] // dk):
            zh = z[:, hh * dk:(hh + 1) * dk]
            t = (zh * lax.rsqrt(jnp.sum(zh * zh, axis=-1, keepdims=True) + NORM_EPS)) * qscale
            if rope:
                t = t * cos_ref[...] + pltpu.roll(t, dk // 2, 1) * sin_ref[...]
            o_ref[0, :, hh * dk:(hh + 1) * dk] = t.astype(o_ref.dtype)

    @pl.when((j >= 4) & (j < 6))
    def _():
        o_ref[0] = conv_silu().astype(o_ref.dtype)

    @pl.when((j >= 6) & (j < 8))
    def _():
        o_ref[0] = _dot_rows(h_s, w_ref[...]).astype(o_ref.dtype)

    @pl.when(j == 8)
    def _():
        ab = _dot_rows(h_s, wab_ref[...])
        lane = lax.broadcasted_iota(jnp.int32, ab.shape, 1) % LANES
        xs = ab + dt_ref[...]
        softplus = jnp.maximum(xs, 0.0) + jnp.log(1.0 + jnp.exp(-jnp.abs(xs)))
        g = -jnp.exp(al_ref[...]) * softplus
        beta = 1.0 / (1.0 + jnp.exp(-ab))
        g_ref[0] = jnp.where(lane < 2 * GDN_HB, g, jnp.where(lane < 4 * GDN_HB, beta, 0.0))


def _gdn_proj(x3, mod3, w_main, w_ab, conv_w, cosf, sinf, a_log_l, dt_l, rope):
    bn, n, d = x3.shape
    segs = mod3.shape[0]
    tc = 512
    nq = w_main.shape[1] // tc
    gl = w_ab.shape[1]
    mi = (lambda b, j: (b, 0, 0)) if segs > 1 else (lambda b, j: (0, 0, 0))
    const = lambda b, j: (0, 0)
    return pl.pallas_call(
        functools.partial(_gdn_proj_kernel, rope=rope),
        out_shape=(jax.ShapeDtypeStruct((bn, n, w_main.shape[1]), BF16),
                   jax.ShapeDtypeStruct((bn, n, gl), F32)),
        grid=(bn, nq + 1),
        in_specs=[pl.BlockSpec((1, n, d), lambda b, j: (b, 0, 0)),
                  pl.BlockSpec((1, 3, d), mi),
                  pl.BlockSpec((d, tc), lambda b, j: (0, jnp.minimum(j, nq - 1))),
                  pl.BlockSpec((d, gl), const),
                  pl.BlockSpec((3, tc), lambda b, j: (0, jnp.minimum(j, 5))),
                  pl.BlockSpec((n, GDN_DK), const), pl.BlockSpec((n, GDN_DK), const),
                  pl.BlockSpec((1, gl), const), pl.BlockSpec((1, gl), const)],
        out_specs=(pl.BlockSpec((1, n, tc), lambda b, j: (b, 0, jnp.minimum(j, nq - 1))),
                   pl.BlockSpec((1, n, gl), lambda b, j: (b, 0, 0))),
        scratch_shapes=[pltpu.VMEM((n, d), BF16)],
        compiler_params=_cparams(("arbitrary", "arbitrary")),
        name="gdn_proj",
    )(x3, mod3, w_main, w_ab, conv_w, cosf, sinf, a_log_l, dt_l)


def _gdn_scan_kernel(ql, kl, vl, gl, qc, kc, vc, gx, ol, oc, s_ref, u_s, wq_s, at_s, kd_s, la_s, *, hb):
    cs = GDN_CHUNK
    dk = GDN_DK
    ns = 2 * hb
    n_lat = ql.shape[1]
    n_ctx = qc.shape[1]
    s_ref[...] = jnp.zeros_like(s_ref)
    ol[...] = jnp.zeros_like(ol)
    oc[...] = jnp.zeros_like(oc)
    ii = lax.broadcasted_iota(jnp.int32, (cs, cs), 0)
    jj = lax.broadcasted_iota(jnp.int32, (cs, cs), 1)
    eye = (ii == jj)[None]
    lower = (ii >= jj)[None]
    upper = (ii <= jj)[None]
    eye_f = eye.astype(F32)

    def run_window(qr, kr, vr, gr, orf, t0f, t0b, nb):
        b2 = ns * nb
        bi = lax.broadcasted_iota(jnp.int32, (b2, cs, cs), 0)
        sign = 1 - 2 * ((bi // nb) % 2)
        dij = sign * (lax.broadcasted_iota(jnp.int32, (b2, cs, cs), 1)
                      - lax.broadcasted_iota(jnp.int32, (b2, cs, cs), 2))
        tri = dij >= 0
        stri = dij > 0

        def rev(a):
            return jnp.concatenate([a[c:c + 1] for c in reversed(range(nb))], axis=0)

        def gates(t0, d):
            slab = gr[0, pl.ds(t0, nb * cs), :].reshape(nb, cs, LANES)
            g1 = slab.astype(BF16)
            r1 = slab - g1.astype(F32)
            g2 = r1.astype(BF16)
            g3 = (r1 - g2.astype(F32)).astype(BF16)
            t = jnp.broadcast_to((lower if d == 0 else upper).astype(BF16), (nb, cs, cs))
            gcum = _bdot(t, g1) + (_bdot(t, g2) + _bdot(t, g3))
            gtot = jnp.sum(slab, axis=1, keepdims=True)
            return (slab, gcum, gtot) if d == 0 else (rev(slab), rev(gcum), rev(gtot))

        gate_d = (gates(t0f, 0), gates(t0b, 1))
        gc_l, be_l, gt_l, q_l, k_l, v_l = [], [], [], [], [], []
        for h in range(hb):
            for d in range(2):
                slab, gcum, gtot = gate_d[d]
                lg = d * hb + h
                lb = 2 * hb + lg
                gc_l.append(gcum[:, :, lg:lg + 1])
                be_l.append(slab[:, :, lb:lb + 1])
                gt_l.append(gtot[:, :, lg:lg + 1])
                t0 = t0f if d == 0 else t0b
                cols = slice(h * dk, (h + 1) * dk)
                fix = (lambda a: a) if d == 0 else rev
                q_l.append(fix(qr[0, pl.ds(t0, nb * cs), cols].reshape(nb, cs, dk)))
                k_l.append(fix(kr[0, pl.ds(t0, nb * cs), cols].reshape(nb, cs, dk)))
                v_l.append(fix(vr[0, pl.ds(t0, nb * cs), cols].reshape(nb, cs, dk)))
        gc = jnp.concatenate(gc_l, axis=0)
        be = jnp.concatenate(be_l, axis=0)
        gt = jnp.concatenate(gt_l, axis=0)
        q = jnp.concatenate(q_l, axis=0)
        k = jnp.concatenate(k_l, axis=0)
        v = jnp.concatenate(v_l, axis=0)

        gcl = gc + jnp.zeros((1, 1, cs), F32)
        gcr = jnp.sum(jnp.where(eye, gcl, 0.0), axis=1, keepdims=True)
        decay = jnp.where(tri, jnp.exp(jnp.where(tri, gcl - gcr, 0.0)), 0.0)
        kq = _bdot_nt(jnp.concatenate([k, q], axis=1), k)
        a = jnp.where(stri, kq[:, :cs] * be * decay, 0.0)
        mm = lambda l, r: _bdot(l.astype(BF16), r.astype(BF16))
        x = mm(a, a)
        p = eye_f - a
        for _ in range(4):
            px = mm(jnp.concatenate([p, x], axis=1), x)
            p = p + px[:, :cs]
            x = px[:, cs:]
        p = p + mm(p, x)
        pb = p.astype(BF16)
        eg = jnp.exp(gc)
        kf = k.astype(F32)
        uw = _bdot(pb, jnp.concatenate([(v.astype(F32) * be).astype(BF16), (kf * (be * eg)).astype(BF16)], axis=2))
        u_s[:, :nb] = uw[:, :, :dk].reshape(ns, nb, cs, dk)
        qd = (q.astype(F32) * eg).astype(BF16)
        wq_s[:, :nb] = jnp.concatenate([uw[:, :, dk:].astype(BF16), qd], axis=1).reshape(ns, nb, 2 * cs, dk)
        at_s[:, :nb] = jnp.where(tri, kq[:, cs:] * decay, 0.0).astype(BF16).reshape(ns, nb, cs, cs)
        kd_s[:, :nb] = (kf * jnp.exp(gt - gc)).astype(BF16).reshape(ns, nb, cs, dk)
        la_s[:, :nb] = (jnp.exp(gt) + jnp.zeros((1, 1, dk), F32)).reshape(ns, nb, 1, dk)

        for j in range(nb):
            s = s_ref[...]
            sbf = s.astype(BF16)
            ws_qs = _bdot(wq_s[:, j], sbf)
            v_new = u_s[:, j] - ws_qs[:, :cs]
            vb = v_new.astype(BF16)
            o = ws_qs[:, cs:] + _bdot(at_s[:, j], vb)
            s_ref[...] = s * la_s[:, j] + _bdot_tn(kd_s[:, j], vb)
            for h in range(hb):
                cols = slice(h * dk, (h + 1) * dk)
                orf[0, pl.ds(t0f + j * cs, cs), cols] += o[2 * h]
                orf[0, pl.ds(t0b + (nb - 1 - j) * cs, cs), cols] += o[2 * h + 1]

    run_window(qc, kc, vc, gx, oc, 0, 0, n_ctx // cs)
    wrows = GDN_NB * cs
    nwin = n_lat // wrows

    def body(wi, carry):
        t0f = pl.multiple_of(wi * wrows, wrows)
        t0b = pl.multiple_of((nwin - 1 - wi) * wrows, wrows)
        run_window(ql, kl, vl, gl, ol, t0f, t0b, GDN_NB)
        return carry

    lax.fori_loop(0, nwin, body, 0)


def _gdn_scan(pl_, gl_, pc_, gc_):
    bn, n_lat, _ = pl_.shape
    n_ctx = pc_.shape[1]
    hb, dk, cs = GDN_HB, GDN_DK, GDN_CHUNK
    hd = GDN_HEADS * dk
    ngrp = GDN_HEADS // hb
    wcol = hb * dk
    nbm = max(GDN_NB, n_ctx // cs)

    def sec(n, k):
        return pl.BlockSpec((1, n, wcol), lambda b, g: (b, 0, k * ngrp + g))

    gate = lambda n: pl.BlockSpec((1, n, LANES), lambda b, g: (b, 0, g))
    out = lambda n: pl.BlockSpec((1, n, wcol), lambda b, g: (b, 0, g))
    return pl.pallas_call(
        functools.partial(_gdn_scan_kernel, hb=hb),
        out_shape=(jax.ShapeDtypeStruct((bn, n_lat, hd), F32), jax.ShapeDtypeStruct((bn, n_ctx, hd), F32)),
        grid=(bn, ngrp),
        in_specs=[sec(n_lat, 0), sec(n_lat, 1), sec(n_lat, 2), gate(n_lat),
                  sec(n_ctx, 0), sec(n_ctx, 1), sec(n_ctx, 2), gate(n_ctx)],
        out_specs=(out(n_lat), out(n_ctx)),
        scratch_shapes=[pltpu.VMEM((2 * hb, dk, dk), F32),
                        pltpu.VMEM((2 * hb, nbm, cs, dk), F32),
                        pltpu.VMEM((2 * hb, nbm, 2 * cs, dk), BF16),
                        pltpu.VMEM((2 * hb, nbm, cs, cs), BF16),
                        pltpu.VMEM((2 * hb, nbm, cs, dk), BF16),
                        pltpu.VMEM((2 * hb, nbm, 1, dk), F32)],
        compiler_params=_cparams(("arbitrary", "arbitrary")),
        name="gdn_scan",
    )(pl_, pl_, pl_, gl_, pc_, pc_, pc_, gc_)


def _gdn_out_kernel(o_ref, z_ref, ng_ref, w_ref, x_ref, g_ref, y_ref):
    dk = GDN_DK
    parts = []
    for h in range(o_ref.shape[1] // dk):
        cols = slice(h * dk, (h + 1) * dk)
        oh = o_ref[:, cols]
        ms = jnp.mean(oh * oh, axis=-1, keepdims=True)
        on = (oh * lax.rsqrt(ms + NORM_EPS)) * ng_ref[...]
        parts.append((on * _silu(z_ref[:, cols].astype(F32))).astype(BF16))
    a = jnp.concatenate(parts, axis=1)
    y_ref[...] = x_ref[...] + g_ref[0] * _dot_rows(a, w_ref[...])


def _gdn_out(o, proj, norm_g, w_out, x, gate, rows_per_seg, tl=512):
    r, hd = o.shape
    d = w_out.shape[1]
    tps = rows_per_seg // tl
    return pl.pallas_call(
        _gdn_out_kernel,
        out_shape=jax.ShapeDtypeStruct((r, d), F32),
        grid=(r // tl,),
        in_specs=[pl.BlockSpec((tl, hd), lambda i: (i, 0)),
                  pl.BlockSpec((tl, hd), lambda i: (i, 3)),
                  pl.BlockSpec((1, GDN_DK), lambda i: (0, 0)),
                  pl.BlockSpec((hd, d), lambda i: (0, 0)),
                  pl.BlockSpec((tl, d), lambda i: (i, 0)),
                  pl.BlockSpec((1, 1, d), lambda i: (i // tps, 0, 0))],
        out_specs=pl.BlockSpec((tl, d), lambda i: (i, 0)),
        compiler_params=_cparams(("arbitrary",)),
        name="gdn_out",
    )(o, proj, norm_g, w_out, x, gate)


def _proj3_kernel(x_ref, m_ref, w0_ref, w1_ref, w2_ref, cw_ref, cb_ref, *rest, hyena):
    j = pl.program_id(1)
    o_refs, h_s = rest[:-1], rest[-1]
    n = x_ref.shape[1]

    @pl.when(j == 0)
    def _():
        h_s[...] = _rms_mod(x_ref[0], m_ref[0]).astype(BF16)

    row = lax.broadcasted_iota(jnp.int32, (n, 1), 0)
    z0 = _dot_rows(h_s, w0_ref[...])
    z1 = _dot_rows(h_s, w1_ref[...])
    z2 = _dot_rows(h_s, w2_ref[...])
    if hyena:
        cb = cb_ref[...]
        x0 = _seq_conv3(z0, cw_ref[0], row) + cb[0:1]
        x1 = _seq_conv3(z1, cw_ref[1], row) + cb[1:2]
        v = _seq_conv3(z2, cw_ref[2], row) + cb[2:3]
        o_refs[0][0] = (v * x1).astype(BF16)
        o_refs[1][0] = x0.astype(BF16)
    else:
        o_refs[0][0] = (z1 * _seq_conv3(z2 * z0, cw_ref[0], row)).astype(BF16)


def _proj3(x3, mod3, w_in, cw, cb, hyena, tc=256):
    bn, n, d = x3.shape
    segs = mod3.shape[0]
    nj = d // tc
    mi = (lambda b, j: (b, 0, 0)) if segs > 1 else (lambda b, j: (0, 0, 0))
    wspec = lambda k: pl.BlockSpec((d, tc), lambda b, j: (0, k * nj + j))
    seq = pl.BlockSpec((1, n, tc), lambda b, j: (b, 0, j))
    n_out = 2 if hyena else 1
    outs = pl.pallas_call(
        functools.partial(_proj3_kernel, hyena=hyena),
        out_shape=tuple(jax.ShapeDtypeStruct((bn, n, d), BF16) for _ in range(n_out)),
        grid=(bn, nj),
        in_specs=[pl.BlockSpec((1, n, d), lambda b, j: (b, 0, 0)),
                  pl.BlockSpec((1, 3, d), mi),
                  wspec(0), wspec(1), wspec(2),
                  pl.BlockSpec((cw.shape[0], 3, tc), lambda b, j: (0, 0, j)),
                  pl.BlockSpec((3, tc), lambda b, j: (0, j))],
        out_specs=tuple(seq for _ in range(n_out)),
        scratch_shapes=[pltpu.VMEM((n, d), BF16)],
        compiler_params=_cparams(("arbitrary", "arbitrary")),
        name="hyena_proj" if hyena else "shortconv_proj",
    )(x3, mod3, w_in, w_in, w_in, cw, cb)
    return outs


def _final_norm_kernel(x_ref, g_ref, o_ref):
    x = x_ref[...]
    ms = jnp.mean(x * x, axis=-1, keepdims=True)
    o_ref[...] = (x * lax.rsqrt(ms + NORM_EPS)) * g_ref[...]


def _final_norm(x, g, tl=1024):
    r, d = x.shape
    return pl.pallas_call(
        _final_norm_kernel,
        out_shape=jax.ShapeDtypeStruct((r, d), F32),
        grid=(r // tl,),
        in_specs=[pl.BlockSpec((tl, d), lambda i: (i, 0)), pl.BlockSpec((1, d), lambda i: (0, 0))],
        out_specs=pl.BlockSpec((tl, d), lambda i: (i, 0)),
        compiler_params=_cparams(("arbitrary",)),
        name="final_norm",
    )(x, g)


def _hyena_kernel(u_ref, x0_ref, c_ref, s_ref, kr_ref, ki_ref, kn_ref, fb_ref, o_ref):
    ub = u_ref[0]
    l = ub.shape[0]
    mb = min(l, MM_ROWS)

    def rowblocks(fn):
        return jnp.concatenate([fn(slice(i, i + mb)) for i in range(0, l, mb)], axis=0)

    a = rowblocks(lambda r: _dot(c_ref[r, :], ub))
    b = rowblocks(lambda r: _dot(s_ref[r, :], ub))
    kr = kr_ref[...]
    ki = ki_ref[...]
    zr = (a * kr + b * ki).astype(BF16)
    zi = (b * kr - a * ki).astype(BF16)
    y = rowblocks(lambda r: _dot(c_ref[r, :], zr) + _dot(s_ref[r, :], zi))
    uf = ub.astype(F32)
    tpar = lax.broadcasted_iota(jnp.int32, (l, 1), 0) & 1
    alt = (1 - 2 * tpar).astype(F32)
    un = jnp.sum(uf * alt, axis=0, keepdims=True)
    y = y + alt * (un * kn_ref[...]) + uf * fb_ref[...]
    o_ref[0] = (y * x0_ref[0].astype(F32)).astype(o_ref.dtype)


def _hyena_conv(u, x0, cmat, smat, kr, ki, kn, fbias, td=256):
    b, l, d = u.shape
    seq = pl.BlockSpec((1, l, td), lambda i, j: (i, 0, j))
    mat = pl.BlockSpec((l, l), lambda i, j: (0, 0), pipeline_mode=pl.Buffered(1))
    spec = pl.BlockSpec((l, td), lambda i, j: (0, j))
    vec = pl.BlockSpec((1, td), lambda i, j: (0, j))
    return pl.pallas_call(
        _hyena_kernel,
        out_shape=jax.ShapeDtypeStruct((b, l, d), BF16),
        grid=(b, d // td),
        in_specs=[seq, seq, mat, mat, spec, spec, vec, vec],
        out_specs=seq,
        compiler_params=_cparams(("arbitrary", "arbitrary")),
        name="hyena_conv",
    )(u, x0, cmat, smat, kr, ki, kn, fbias)


def _mm3_kernel(a_ref, b_ref, o_ref):
    ah, al = _split_bf16(a_ref[...])
    bh, bl = _split_bf16(b_ref[...])
    o_ref[...] = _dot(ah, bh) + (_dot(ah, bl) + _dot(al, bh))


def _mm3(a, b, tm=256, tn=256):
    m, k = a.shape
    n = b.shape[1]
    tm = min(tm, m)
    return pl.pallas_call(
        _mm3_kernel,
        out_shape=jax.ShapeDtypeStruct((m, n), F32),
        grid=(n // tn, m // tm),
        in_specs=[pl.BlockSpec((tm, k), lambda j, i: (i, 0)),
                  pl.BlockSpec((k, tn), lambda j, i: (0, j))],
        out_specs=pl.BlockSpec((tm, tn), lambda j, i: (i, j)),
        compiler_params=_cparams(("arbitrary", "arbitrary")),
        name="matmul_f32x3",
    )(a, b)


def _na_kernel(q_ref, k_ref, v_ref, kc_ref, vc_ref, bias_ref, o_ref, *, rows):
    r = pl.program_id(1)
    rs = jnp.clip(r - NA_WIN_R // 2, 0, rows - NA_WIN_R)
    t0 = pl.multiple_of(rs * GRID_W, GRID_W)
    nk = NA_WIN_R * GRID_W
    scale = NA_DH ** -0.5
    lane = lax.broadcasted_iota(jnp.int32, (GRID_W, LANES), 1)
    first = lane < NA_DH
    nq = GRID_W
    scores = []
    for hp in range(NA_HEADS // 2):
        cols = slice(hp * LANES, (hp + 1) * LANES)
        q2 = q_ref[0, :, cols]
        zero = jnp.zeros_like(q2)
        qm = jnp.concatenate([jnp.where(first, q2, zero), jnp.where(first, zero, q2)], axis=0)
        bias = bias_ref[0, 2 * hp:2 * hp + 2].reshape(2 * nq, nk)
        s_loc = _dot_nt(qm, k_ref[0, pl.ds(t0, nk), cols]) * scale + bias
        s_ctx = _dot_nt(qm, kc_ref[0, :, cols]) * scale
        scores.append((s_loc, s_ctx))
    probs = []
    for s_loc, s_ctx in scores:
        m = jnp.maximum(jnp.max(s_loc, axis=1, keepdims=True), jnp.max(s_ctx, axis=1, keepdims=True))
        p_loc = jnp.exp(s_loc - m)
        p_ctx = jnp.exp(s_ctx - m)
        den = jnp.sum(p_loc, axis=1, keepdims=True) + jnp.sum(p_ctx, axis=1, keepdims=True)
        probs.append((p_loc.astype(BF16), p_ctx.astype(BF16), den))
    pairs = []
    for hp, (p_loc, p_ctx, den) in enumerate(probs):
        cols = slice(hp * LANES, (hp + 1) * LANES)
        o = (_dot(p_loc, v_ref[0, pl.ds(t0, nk), cols]) + _dot(p_ctx, vc_ref[0, :, cols])) / den
        pairs.append(jnp.where(first, o[:nq], o[nq:]).astype(o_ref.dtype))
    o_ref[0] = jnp.concatenate(pairs, axis=1)


def _na_attention(q, k, v, kc, vc, bias_tab):
    b, l, hd = q.shape
    lc = kc.shape[1]
    rows = l // GRID_W
    half = NA_WIN_R // 2
    return pl.pallas_call(
        functools.partial(_na_kernel, rows=rows),
        out_shape=jax.ShapeDtypeStruct((b, l, hd), BF16),
        grid=(b, rows),
        in_specs=[pl.BlockSpec((1, GRID_W, hd), lambda i, r: (i, r, 0)),
                  pl.BlockSpec((1, l, hd), lambda i, r: (i, 0, 0)),
                  pl.BlockSpec((1, l, hd), lambda i, r: (i, 0, 0)),
                  pl.BlockSpec((1, lc, hd), lambda i, r: (i, 0, 0)),
                  pl.BlockSpec((1, lc, hd), lambda i, r: (i, 0, 0)),
                  pl.BlockSpec((1, NA_HEADS, GRID_W, NA_WIN_R * GRID_W),
                               lambda i, r: (r - jnp.clip(r - half, 0, rows - NA_WIN_R), 0, 0, 0))],
        out_specs=pl.BlockSpec((1, GRID_W, hd), lambda i, r: (i, r, 0)),
        compiler_params=_cparams(("arbitrary", "arbitrary")),
        name="na_attention",
    )(q, k, v, kc, vc, bias_tab)


def _na_bias_table(rpb):
    qc = np.arange(GRID_W)
    kc = np.arange(GRID_W)
    cstart = np.clip(qc - NA_WIN_C // 2, 0, GRID_W - NA_WIN_C)
    valid = (kc[None, :] >= cstart[:, None]) & (kc[None, :] < cstart[:, None] + NA_WIN_C)
    dc = np.clip(kc[None, :] - qc[:, None] + NA_WIN_C - 1, 0, 2 * NA_WIN_C - 2)
    off = np.arange(NA_WIN_R)
    j = np.arange(NA_WIN_R)
    dr = j[None, :] - off[:, None] + NA_WIN_R - 1
    sel_r = (dr[:, :, None] == np.arange(2 * NA_WIN_R - 1)).astype(np.float32)
    sel_c = (dc[:, :, None] == np.arange(2 * NA_WIN_C - 1)).astype(np.float32)
    t = jnp.einsum("ojr,hrc,qkc->ohqjk", sel_r, rpb.astype(F32), sel_c, precision=lax.Precision.HIGHEST)
    t = jnp.where(jnp.asarray(valid)[None, None, :, None, :], t, jnp.float32(-1e30))
    return t.reshape(NA_WIN_R, NA_HEADS, GRID_W, NA_WIN_R * GRID_W)


def _rope_tables(n_tok, dh):
    pos = jnp.arange(n_tok)
    row = (pos // GRID_W).astype(F32)
    col = (pos % GRID_W).astype(F32)
    n_freq = dh // 4
    inv = ROPE_BASE ** (-jnp.arange(n_freq, dtype=F32) / n_freq)
    ang = jnp.concatenate([row[:, None] * inv, col[:, None] * inv], axis=-1)
    cos, sin = jnp.cos(ang), jnp.sin(ang)
    return jnp.concatenate([cos, cos], axis=-1), jnp.concatenate([-sin, sin], axis=-1)


def _gdn_gate_lanes():
    ngrp = GDN_HEADS // GDN_HB
    src = -np.ones((ngrp * LANES,), np.int64)
    for grp in range(ngrp):
        for t in range(2):
            for dr in range(2):
                for hh in range(GDN_HB):
                    lane = grp * LANES + t * 2 * GDN_HB + dr * GDN_HB + hh
                    src[lane] = t * 2 * GDN_HEADS + dr * GDN_HEADS + grp * GDN_HB + hh
    return src


_GDN_LANE_SRC = _gdn_gate_lanes()


def _gdn_layer(x, xc, mod_l, mod_c, gate_l, gate_c, bn, w_in, conv_w, a_log, dt_bias, norm_g, w_out, ctx_out):
    d = D_MODEL
    hd = GDN_HEADS * GDN_DK
    n_lat = x.shape[0] // bn
    n_ctx = xc.shape[0] // bn
    w_main = w_in[:, :4 * hd].astype(BF16)
    used = jnp.asarray(_GDN_LANE_SRC >= 0)
    lane_src = jnp.asarray(np.maximum(_GDN_LANE_SRC, 0))
    w_ab = jnp.where(used[None, :], w_in[:, 4 * hd:][:, lane_src], 0.0).astype(BF16)
    decay_lane = used & (lane_src < 2 * GDN_HEADS)
    a_log_l = jnp.where(decay_lane, a_log.reshape(-1)[lane_src % (2 * GDN_HEADS)], 0.0)[None, :].astype(F32)
    dt_l = jnp.where(decay_lane, dt_bias.reshape(-1)[lane_src % (2 * GDN_HEADS)], 0.0)[None, :].astype(F32)
    cosf, sinf = _rope_tables(n_lat, GDN_DK)
    dummy = jnp.zeros((n_ctx, GDN_DK), F32)
    p_l, g_l = _gdn_proj(x.reshape(bn, n_lat, d), mod_l, w_main, w_ab, conv_w, cosf, sinf, a_log_l, dt_l, True)
    p_c, g_c = _gdn_proj(xc.reshape(bn, n_ctx, d), mod_c, w_main, w_ab, conv_w, dummy, dummy, a_log_l, dt_l, False)
    o_l, o_c = _gdn_scan(p_l, g_l, p_c, g_c)
    wo = w_out.astype(BF16)
    ng = norm_g.astype(F32)[None, :]
    x_new = _gdn_out(o_l.reshape(bn * n_lat, hd), p_l.reshape(bn * n_lat, 4 * hd), ng, wo, x, gate_l, n_lat)
    xc_new = None
    if ctx_out:
        xc_new = _gdn_out(o_c.reshape(bn * n_ctx, hd), p_c.reshape(bn * n_ctx, 4 * hd), ng, wo, xc, gate_c,
                          xc.shape[0])
    return x_new, xc_new


def _hyena_filter_taps(n_tok, w1, b1, w2, b2, w3, b3, freq, w4):
    t = jnp.linspace(0.0, 1.0, n_tok, dtype=F32)[:, None]
    bands = (HY_EMB_DIM - 1) // 2
    wpos = 2.0 * math.pi * jnp.arange(n_tok, dtype=F32)[:, None] / n_tok
    fr = jnp.linspace(1e-4, bands - 1, bands, dtype=F32)[None, :]
    z = jnp.concatenate([t, jnp.cos(fr * wpos), -jnp.sin(fr * wpos)], axis=-1)
    freq = freq.astype(F32)
    hdn = jnp.sin(freq[0] * (z @ w1.astype(F32) + b1.astype(F32)))
    hdn = jnp.sin(freq[1] * (hdn @ w2.astype(F32) + b2.astype(F32)))
    hdn = jnp.sin(freq[2] * (hdn @ w3.astype(F32) + b3.astype(F32)))
    filt = (hdn @ w4.astype(F32)).reshape(n_tok, 2, D_MODEL)
    deltas = jnp.abs(jnp.linspace(HY_MIN_DECAY, HY_MAX_DECAY, D_MODEL, dtype=F32))
    filt = filt * jnp.exp(-t * deltas)[:, None, :]
    return filt[:, 0], filt[:, 1]


def _dft_mats(l):
    k = jnp.arange(l, dtype=jnp.int32)
    ks = (k[:, None] * k[None, :]) % (2 * l)
    ang = ks.astype(F32) * (math.pi / l)
    return jnp.cos(ang), jnp.sin(ang)


def _hyena_layer(rows, mod3, gate, rows_per_seg, bn, w_in, short_w, short_b, f_w1, f_b1, f_w2, f_b2, f_w3, f_b3,
                 f_freq, f_w4, f_bias, w_out):
    d = D_MODEL
    n_tok = rows.shape[0] // bn
    cw = short_w.reshape(3, 3, d).transpose(1, 0, 2).astype(F32)
    u, x0 = _proj3(rows.reshape(bn, n_tok, d), mod3, w_in.astype(BF16), cw, short_b.reshape(3, d).astype(F32),
                   hyena=True)
    hf, hb = _hyena_filter_taps(n_tok, f_w1, f_b1, f_w2, f_b2, f_w3, f_b3, f_freq, f_w4)
    hb = hb.at[0].set(0.0)
    cmat, smat = _dft_mats(n_tok)
    n2 = 2 * n_tok
    wk = jnp.full((n_tok, 1), 2.0 / n2, F32).at[0, 0].set(1.0 / n2)
    kr = _mm3(cmat, hf + hb) * wk
    ki = _mm3(smat, hb - hf) * wk
    alt = (1.0 - 2.0 * (jnp.arange(n_tok) % 2)).astype(F32)[:, None]
    kn = jnp.sum((hf + hb) * alt, axis=0, keepdims=True) / n2
    y = _hyena_conv(u, x0, cmat.astype(BF16), smat.astype(BF16), kr, ki, kn, f_bias.astype(F32)[None, :])
    return _mm_res(y.reshape(rows.shape[0], d), w_out.astype(BF16), rows, gate, rows_per_seg)


def _na_layer(x, xc, mod_l, mod_c, gate_l, bn, w_qkv, rpb, w_out):
    hd = NA_HEADS * NA_DH
    n_lat = x.shape[0] // bn
    n_ctx = xc.shape[0] // bn
    wq = w_qkv.astype(BF16)
    z = _modmm(x, mod_l, wq, n_lat, out_dtype=BF16).reshape(bn, n_lat, 3 * hd)
    zc = _modmm(xc, mod_c, wq, xc.shape[0], out_dtype=BF16).reshape(bn, n_ctx, 3 * hd)
    o = _na_attention(z[..., :hd], z[..., hd:2 * hd], z[..., 2 * hd:], zc[..., hd:2 * hd], zc[..., 2 * hd:],
                      _na_bias_table(rpb))
    return _mm_res(o.reshape(x.shape[0], hd), w_out.astype(BF16), x, gate_l, n_lat)


def _shortconv_layer(rows, mod3, gate, rows_per_seg, bn, w_in, conv_w, w_out):
    d = D_MODEL
    n_tok = rows.shape[0] // bn
    (a,) = _proj3(rows.reshape(bn, n_tok, d), mod3, w_in.astype(BF16), conv_w.astype(F32)[None],
                  jnp.zeros((3, d), F32), hyena=False)
    return _mm_res(a.reshape(rows.shape[0], d), w_out.astype(BF16), rows, gate, rows_per_seg)


def kernel(x, c, ctx, c_ctx, ln_g, w_mod, b_mod, final_g, gdn_w_in, gdn_conv_w, gdn_a_log, gdn_dt_bias, gdn_norm_g, gdn_w_out, hy_w_in, hy_short_w, hy_short_b, hy_f_w1, hy_f_b1, hy_f_w2, hy_f_b2, hy_f_w3, hy_f_b3, hy_f_freq, hy_f_w4, hy_f_bias, hy_w_out, na_w_qkv, na_rpb, na_w_out, sc_w_in, sc_conv_w, sc_w_out, moe_w_gr, moe_b_gr, moe_w_er, moe_b_er, moe_w_gate, moe_w_up, moe_w_down):
    bn, n_lat, d = x.shape
    n_ctx = ctx.shape[1]
    rows_ctx = bn * n_ctx
    pad = (-(bn + 1)) % 8
    c_all = jnp.concatenate([c, c_ctx[None, :], jnp.zeros((pad, d), F32)], axis=0)
    mods = _mod_vectors(c_all, w_mod, b_mod)
    xr = x.reshape(bn * n_lat, d)
    xcr = ctx.reshape(rows_ctx, d)
    for i in range(DEPTH):
        m, s = i % N_MIXERS, i // N_MIXERS
        reads_ctx = m in CTX_READING_MIXERS
        upd_ctx = any((j % N_MIXERS) in CTX_READING_MIXERS for j in range(i + 1, DEPTH))
        mv = mods[i].reshape(-1, 6, d)

        def mod3(rows, which, norm_g):
            sh, sc = mv[rows, 3 * which], mv[rows, 3 * which + 1]
            return jnp.stack([jnp.broadcast_to(norm_g, sh.shape), sh, sc], axis=1)

        lat = slice(0, bn)
        cx = slice(bn, bn + 1)
        mod_l1, mod_l2 = mod3(lat, 0, ln_g[i, 0]), mod3(lat, 1, ln_g[i, 1])
        mod_c1, mod_c2 = mod3(cx, 0, ln_g[i, 0]), mod3(cx, 1, ln_g[i, 1])
        g1_l, g2_l = mv[lat, 2][:, None, :], mv[lat, 5][:, None, :]
        g1_c, g2_c = mv[cx, 2][:, None, :], mv[cx, 5][:, None, :]
        xc_new = None
        if m == 0:
            xr, xc_new = _gdn_layer(xr, xcr, mod_l1, mod_c1, g1_l, g1_c, bn, gdn_w_in[s], gdn_conv_w[s],
                                    gdn_a_log[s], gdn_dt_bias[s], gdn_norm_g[s], gdn_w_out[s], upd_ctx)
        elif m == 1:
            hy = (hy_w_in[s], hy_short_w[s], hy_short_b[s], hy_f_w1[s], hy_f_b1[s], hy_f_w2[s], hy_f_b2[s],
                  hy_f_w3[s], hy_f_b3[s], hy_f_freq[s], hy_f_w4[s], hy_f_bias[s], hy_w_out[s])
            xr = _hyena_layer(xr, mod_l1, g1_l, n_lat, bn, *hy)
            if upd_ctx:
                xc_new = _hyena_layer(xcr, mod_c1, g1_c, rows_ctx, bn, *hy)
        elif m == 2:
            xr = _na_layer(xr, xcr, mod_l1, mod_c1, g1_l, bn, na_w_qkv[s], na_rpb[s], na_w_out[s])
            assert not upd_ctx
        else:
            xr = _shortconv_layer(xr, mod_l1, g1_l, n_lat, bn, sc_w_in[s], sc_conv_w[s], sc_w_out[s])
            if upd_ctx:
                xc_new = _shortconv_layer(xcr, mod_c1, g1_c, rows_ctx, bn, sc_w_in[s], sc_conv_w[s], sc_w_out[s])
        moe = (moe_w_gr[i], moe_b_gr[i], moe_w_er[i], moe_b_er[i], moe_w_gate[i], moe_w_up[i], moe_w_down[i])
        if upd_ctx:
            xr, xcr = _hier_moe_residual(xr, mod_l2, g2_l, n_lat, *moe, xc=xc_new, mod3c=mod_c2, gatec=g2_c)
        else:
            xr = _hier_moe_residual(xr, mod_l2, g2_l, n_lat, *moe)
    return _final_norm(xr, final_g.astype(F32)[None, :]).reshape(bn, n_lat, d)
```

```python
import functools
import math

import numpy as np
import jax
import jax.numpy as jnp
from jax import lax
from jax.experimental import pallas as pl
from jax.experimental.pallas import tpu as pltpu

F32 = jnp.float32
BF16 = jnp.bfloat16

D_MODEL = 1024
DEPTH = 4
GRID_W = 64
N_MIXERS = 4
CTX_READING_MIXERS = (0, 2)
NORM_EPS = 1e-6

GDN_HEADS = 8
GDN_DK = 128
GDN_CHUNK = 64
ROPE_BASE = 10000.0

HY_EMB_DIM = 33
HY_DECAY_TARGET = 1e-2
HY_MAX_DECAY = math.log(HY_DECAY_TARGET) / 0.3
HY_MIN_DECAY = math.log(HY_DECAY_TARGET) / 1.5

NA_HEADS = 16
NA_DH = 64
NA_WIN_R = 8
NA_WIN_C = 16

MOE_GROUPS = 4
MOE_EPG = 8
MOE_FF = 256
MOE_PAIRS = MOE_EPG * (MOE_EPG - 1) // 2
MOE_CLASSES = MOE_GROUPS * MOE_PAIRS
MOE_TM = 256

LANES = 128
VMEM_LIMIT = 56 << 20


def _cparams(sem, vmem=VMEM_LIMIT):
    return pltpu.CompilerParams(dimension_semantics=sem, vmem_limit_bytes=vmem)


def _rms_mod(x, m):
    ms = jnp.mean(x * x, axis=-1, keepdims=True)
    xn = (x * lax.rsqrt(ms + NORM_EPS)) * m[0:1]
    return xn * (1.0 + m[2:3]) + m[1:2]


def _split_bf16(x):
    hi = x.astype(BF16)
    lo = (x - hi.astype(F32)).astype(BF16)
    return hi, lo


def _dot(a, b):
    return jnp.dot(a, b, preferred_element_type=F32)


MM_ROWS = 256


def _dot_rows(a, b):
    m = a.shape[0]
    if m <= MM_ROWS:
        return _dot(a[...], b)
    return jnp.concatenate([_dot(a[i:i + MM_ROWS], b) for i in range(0, m, MM_ROWS)], axis=0)


def _dot_nt(a, b):
    return lax.dot_general(a, b, (((1,), (1,)), ((), ())), preferred_element_type=F32)


def _bdot(a, b):
    return lax.dot_general(a, b, (((2,), (1,)), ((0,), (0,))), preferred_element_type=F32)


def _bdot_nt(a, b):
    return lax.dot_general(a, b, (((2,), (2,)), ((0,), (0,))), preferred_element_type=F32)


def _bdot_tn(a, b):
    return lax.dot_general(a, b, (((1,), (1,)), ((0,), (0,))), preferred_element_type=F32)


def _bdot3(a, b):
    ah, al = _split_bf16(a)
    bh, bl = _split_bf16(b)
    return _bdot(ah, bh) + (_bdot(ah, bl) + _bdot(al, bh))


def _silu(x):
    return x * (1.0 / (1.0 + jnp.exp(-x)))


def _modvec_kernel(c_ref, w_ref, b_ref, o_ref):
    c = c_ref[...]
    a = _silu(c)
    ah, al = _split_bf16(a)
    wh, wl = _split_bf16(w_ref[0])
    o_ref[0] = _dot(ah, wh) + (_dot(ah, wl) + _dot(al, wh)) + b_ref[0]


def _mod_vectors(c_all, w_mod, b_mod):
    rows, d = c_all.shape
    depth, _, n = w_mod.shape
    tn = 1536
    return pl.pallas_call(
        _modvec_kernel,
        out_shape=jax.ShapeDtypeStruct((depth, rows, n), F32),
        grid=(depth, n // tn),
        in_specs=[pl.BlockSpec((rows, d), lambda i, j: (0, 0)),
                  pl.BlockSpec((1, d, tn), lambda i, j: (i, 0, j)),
                  pl.BlockSpec((1, 1, tn), lambda i, j: (i, 0, j))],
        out_specs=pl.BlockSpec((1, rows, tn), lambda i, j: (i, 0, j)),
        compiler_params=_cparams(("arbitrary", "arbitrary")),
        name="mod_vectors",
    )(c_all, w_mod, b_mod.reshape(depth, 1, n))


def _modmm_kernel(x_ref, m_ref, w_ref, o_ref, *, ncol):
    h = _rms_mod(x_ref[...], m_ref[0]).astype(BF16)
    n = w_ref.shape[1]
    for n0 in range(0, n, ncol):
        o_ref[:, n0:n0 + ncol] = _dot_rows(h, w_ref[:, n0:n0 + ncol]).astype(o_ref.dtype)


def _modmm(x, mod3, w, rows_per_seg, out_dtype=F32, tl=512):
    r, d = x.shape
    n = w.shape[1]
    tps = rows_per_seg // tl
    ncol = 512 if n % 512 == 0 else n
    return pl.pallas_call(
        functools.partial(_modmm_kernel, ncol=ncol),
        out_shape=jax.ShapeDtypeStruct((r, n), out_dtype),
        grid=(r // tl,),
        in_specs=[pl.BlockSpec((tl, d), lambda i: (i, 0)),
                  pl.BlockSpec((1, 3, d), lambda i: (i // tps, 0, 0)),
                  pl.BlockSpec((d, n), lambda i: (0, 0))],
        out_specs=pl.BlockSpec((tl, n), lambda i: (i, 0)),
        compiler_params=_cparams(("arbitrary",)),
        name="mod_matmul",
    )(x, mod3, w)


def _mmres_kernel(a_ref, w_ref, x_ref, g_ref, o_ref):
    o_ref[...] = x_ref[...] + g_ref[0] * _dot_rows(a_ref, w_ref[...])


def _mm_res(a, w, x, gate, rows_per_seg, tl=512):
    r, k = a.shape
    d = w.shape[1]
    tps = rows_per_seg // tl
    return pl.pallas_call(
        _mmres_kernel,
        out_shape=jax.ShapeDtypeStruct((r, d), F32),
        grid=(r // tl,),
        in_specs=[pl.BlockSpec((tl, k), lambda i: (i, 0)),
                  pl.BlockSpec((k, d), lambda i: (0, 0)),
                  pl.BlockSpec((tl, d), lambda i: (i, 0)),
                  pl.BlockSpec((1, 1, d), lambda i: (i // tps, 0, 0))],
        out_specs=pl.BlockSpec((tl, d), lambda i: (i, 0)),
        compiler_params=_cparams(("arbitrary",)),
        name="matmul_residual",
    )(a, w, x, gate)


def _router_kernel(xa_ref, xb_ref, m_ref, wh_ref, wl_ref, b_ref, h_ref, route_ref, cnt_ref, carry_ref, *, na):
    i = pl.program_id(0)

    @pl.when(i == 0)
    def _():
        carry_ref[...] = jnp.zeros_like(carry_ref)

    h = _rms_mod(jnp.where(i < na, xa_ref[...], xb_ref[...]), m_ref[0])
    hh, hl = _split_bf16(h)
    h_ref[...] = hh
    wh = wh_ref[...]
    logits = _dot_rows(hh, wh) + (_dot_rows(hl, wh) + _dot_rows(hh, wl_ref[...])) + b_ref[...]
    t = logits.shape[0]
    lane = lax.broadcasted_iota(jnp.int32, logits.shape, 1).astype(F32)
    neg = -jnp.inf
    big = 1e9

    def first_argmax(vals):
        m = jnp.max(vals, axis=1, keepdims=True)
        idx = jnp.min(jnp.where(vals == m, lane, big), axis=1, keepdims=True)
        return m, idx

    lg = jnp.where(lane < MOE_GROUPS, logits, neg)
    mg, gsel = first_argmax(lg)
    wg = 1.0 / jnp.sum(jnp.exp(lg - mg), axis=1, keepdims=True)
    e_first = MOE_GROUPS + gsel * MOE_EPG
    in_grp = (lane >= e_first) & (lane < e_first + MOE_EPG)
    le = jnp.where(in_grp, logits, neg)
    m1, i1 = first_argmax(le)
    m2, i2 = first_argmax(jnp.where(lane == i1, neg, le))
    p = jnp.exp(m2 - m1)
    w1 = wg / (1.0 + p)
    w2 = wg * p / (1.0 + p)
    l1 = i1 - e_first
    l2 = i2 - e_first
    lo = jnp.minimum(l1, l2)
    hi = jnp.maximum(l1, l2)
    first_is_lo = l1 < l2
    wlo = jnp.where(first_is_lo, w1, w2)
    whi = jnp.where(first_is_lo, w2, w1)
    cls = gsel * MOE_PAIRS + lo * (2 * MOE_EPG - 1 - lo) * 0.5 + (hi - lo - 1.0)

    onehot = lane == cls
    ri = lax.broadcasted_iota(jnp.int32, (t, t), 0)
    ci = lax.broadcasted_iota(jnp.int32, (t, t), 1)
    before = (ci < ri).astype(BF16)
    prefix = _dot(before, onehot.astype(BF16)) + carry_ref[...]
    rank = jnp.sum(jnp.where(onehot, prefix, 0.0), axis=1, keepdims=True)
    carry = carry_ref[...] + jnp.sum(onehot.astype(F32), axis=0, keepdims=True)
    carry_ref[...] = carry
    cnt_ref[...] = carry
    route_ref[...] = jnp.where(lane == 0, cls, jnp.where(lane == 1, rank, jnp.where(
        lane == 2, wlo, jnp.where(lane == 3, whi, 0.0))))


def _router(xa, xb, mod3, w_r, b_r, rows_per_seg_a, tl=512):
    ra, d = xa.shape
    rb = 0 if xb is None else xb.shape[0]
    xb = xa if xb is None else xb
    r = ra + rb
    na = ra // tl
    tps = rows_per_seg_a // tl
    nseg = mod3.shape[0]
    wh, wl = _split_bf16(w_r)
    return pl.pallas_call(
        functools.partial(_router_kernel, na=na),
        out_shape=(jax.ShapeDtypeStruct((r, d), BF16),
                   jax.ShapeDtypeStruct((r, LANES), F32),
                   jax.ShapeDtypeStruct((1, LANES), F32)),
        grid=(r // tl,),
        in_specs=[pl.BlockSpec((tl, d), lambda i: (jnp.minimum(i, na - 1), 0)),
                  pl.BlockSpec((tl, d), lambda i: (jnp.maximum(i - na, 0), 0)),
                  pl.BlockSpec((1, 3, d), lambda i: (jnp.minimum(i // tps, nseg - 1), 0, 0)),
                  pl.BlockSpec((d, LANES), lambda i: (0, 0)),
                  pl.BlockSpec((d, LANES), lambda i: (0, 0)),
                  pl.BlockSpec((1, LANES), lambda i: (0, 0))],
        out_specs=(pl.BlockSpec((tl, d), lambda i: (i, 0)),
                   pl.BlockSpec((tl, LANES), lambda i: (i, 0)),
                   pl.BlockSpec((1, LANES), lambda i: (0, 0))),
        scratch_shapes=[pltpu.VMEM((1, LANES), F32)],
        compiler_params=_cparams(("arbitrary",)),
        name="moe_router",
    )(xa, xb, mod3, wh, wl, b_r)


def _moe_kernel(elo_ref, ehi_ref, val_ref, x_ref, wt_ref, wgl, wul, wdl, wgh, wuh, wdh, o_ref):
    t = pl.program_id(0)

    @pl.when(val_ref[t] != 0)
    def _():
        x = x_ref[...]
        wt = wt_ref[...]
        w16 = lambda ref: ref[0].astype(BF16)
        gl, ul, gh, uh = _dot(x, w16(wgl)), _dot(x, w16(wul)), _dot(x, w16(wgh)), _dot(x, w16(wuh))
        hl = (_silu(gl) * ul) * wt[:, 0:1]
        hh = (_silu(gh) * uh) * wt[:, 1:2]
        o_ref[...] = (_dot(hl.astype(BF16), w16(wdl)) + _dot(hh.astype(BF16), w16(wdh))).astype(o_ref.dtype)

    @pl.when(val_ref[t] == 0)
    def _():
        o_ref[...] = jnp.zeros_like(o_ref)


def _moe_experts(xs, wts, tile_elo, tile_ehi, tile_val, w_gate, w_up, w_down):
    p, d = xs.shape
    ff = w_gate.shape[2]
    nt = p // MOE_TM
    lo = lambda t, elo, ehi, val: (elo[t], 0, 0)
    hi = lambda t, elo, ehi, val: (ehi[t], 0, 0)
    row = lambda t, elo, ehi, val: (t, 0)
    grid_spec = pltpu.PrefetchScalarGridSpec(
        num_scalar_prefetch=3,
        grid=(nt,),
        in_specs=[pl.BlockSpec((MOE_TM, d), row),
                  pl.BlockSpec((MOE_TM, 2), row),
                  pl.BlockSpec((1, d, ff), lo), pl.BlockSpec((1, d, ff), lo), pl.BlockSpec((1, ff, d), lo),
                  pl.BlockSpec((1, d, ff), hi), pl.BlockSpec((1, d, ff), hi), pl.BlockSpec((1, ff, d), hi)],
        out_specs=pl.BlockSpec((MOE_TM, d), row),
    )
    return pl.pallas_call(
        _moe_kernel,
        out_shape=jax.ShapeDtypeStruct((p, d), BF16),
        grid_spec=grid_spec,
        compiler_params=_cparams(("arbitrary",)),
        name="moe_experts",
    )(tile_elo, tile_ehi, tile_val, xs, wts, w_gate, w_up, w_down, w_gate, w_up, w_down)


def _gated_add_kernel(x_ref, y_ref, g_ref, o_ref):
    o_ref[...] = x_ref[...] + g_ref[0] * y_ref[...].astype(F32)


def _gated_add(x, y, gate, rows_per_seg, tl=1024):
    r, d = x.shape
    tps = rows_per_seg // tl
    row = pl.BlockSpec((tl, d), lambda i: (i, 0))
    return pl.pallas_call(
        _gated_add_kernel,
        out_shape=jax.ShapeDtypeStruct((r, d), F32),
        grid=(r // tl,),
        in_specs=[row, row, pl.BlockSpec((1, 1, d), lambda i: (i // tps, 0, 0))],
        out_specs=row,
        compiler_params=_cparams(("arbitrary",)),
        name="moe_combine",
    )(x, y, gate)


def _class_tables():
    elo = np.zeros((MOE_CLASSES,), np.int32)
    ehi = np.zeros((MOE_CLASSES,), np.int32)
    for g in range(MOE_GROUPS):
        for lo in range(MOE_EPG):
            for hi in range(lo + 1, MOE_EPG):
                c = g * MOE_PAIRS + lo * (2 * MOE_EPG - 1 - lo) // 2 + (hi - lo - 1)
                elo[c] = g * MOE_EPG + lo
                ehi[c] = g * MOE_EPG + hi
    return elo, ehi


_CLASS_ELO, _CLASS_EHI = _class_tables()


def _hier_moe_residual(x, mod3, gate, rows_per_seg, w_gr, b_gr, w_er, b_er, w_gate, w_up, w_down,
                       xc=None, mod3c=None, gatec=None):
    ra, d = x.shape
    r = ra + (0 if xc is None else xc.shape[0])
    w_r = jnp.zeros((d, LANES), F32).at[:, :MOE_GROUPS].set(w_gr)
    w_r = w_r.at[:, MOE_GROUPS:MOE_GROUPS + MOE_GROUPS * MOE_EPG].set(w_er)
    b_r = jnp.zeros((1, LANES), F32).at[0, :MOE_GROUPS].set(b_gr)
    b_r = b_r.at[0, MOE_GROUPS:MOE_GROUPS + MOE_GROUPS * MOE_EPG].set(b_er.reshape(-1))
    mods = mod3 if xc is None else jnp.concatenate([mod3, mod3c], axis=0)
    h, route, cnt = _router(x, xc, mods, w_r, b_r, rows_per_seg)

    cls = route[:, 0:1].astype(jnp.int32)
    rank = route[:, 1].astype(jnp.int32)
    counts = cnt[0].astype(jnp.int32)
    pc = ((counts + MOE_TM - 1) // MOE_TM) * MOE_TM
    pend = jnp.cumsum(pc)[:MOE_CLASSES]
    pstart = jnp.cumsum(pc) - pc
    dest = rank + jnp.sum(jnp.where(cls == jnp.arange(LANES, dtype=jnp.int32)[None, :], pstart[None, :], 0), axis=1)
    p_rows = ((r + MOE_CLASSES * (MOE_TM - 1)) // MOE_TM + 1) * MOE_TM
    nt = p_rows // MOE_TM
    vals = jnp.stack([jnp.arange(r, dtype=F32), route[:, 2], route[:, 3]], axis=1)
    pad_src = (jnp.arange(p_rows, dtype=jnp.int32) % r).astype(F32)
    tab0 = jnp.stack([pad_src, jnp.zeros((p_rows,), F32), jnp.zeros((p_rows,), F32)], axis=1)
    tab = tab0.at[dest].set(vals, unique_indices=True, mode="promise_in_bounds")
    src = tab[:, 0].astype(jnp.int32)
    wts = tab[:, 1:3]
    tile_p0 = jnp.arange(nt, dtype=jnp.int32) * MOE_TM
    tile_val = (tile_p0 < pend[-1]).astype(jnp.int32)
    first_row = jnp.minimum(tile_p0, pend[-1] - 1)
    tile_cls = jnp.sum((pend[None, :] <= first_row[:, None]).astype(jnp.int32), axis=1)
    tile_cls = jnp.clip(tile_cls, 0, MOE_CLASSES - 1)
    tile_elo = jnp.asarray(_CLASS_ELO)[tile_cls]
    tile_ehi = jnp.asarray(_CLASS_EHI)[tile_cls]

    xs = h.at[src].get(mode="promise_in_bounds")
    ys = _moe_experts(xs, wts, tile_elo, tile_ehi, tile_val, w_gate, w_up, w_down)
    y = ys.at[dest].get(mode="promise_in_bounds", unique_indices=True)
    x_new = _gated_add(x, y, gate, rows_per_seg)
    if xc is None:
        return x_new
    return x_new, xc + gatec[0] * y[ra:]


GDN_HB = 4
GDN_NB = 4


def _seq_conv3(z, w, row):
    n = z.shape[0]
    zp = jnp.where(row == 0, 0.0, pltpu.roll(z, 1, 0))
    zn = jnp.where(row == n - 1, 0.0, pltpu.roll(z, n - 1, 0))
    return w[0:1] * zp + w[1:2] * z + w[2:3] * zn


def _gdn_proj_kernel(x_ref, m_ref, w_ref, wab_ref, cw_ref, cos_ref, sin_ref, al_ref, dt_ref,
                     o_ref, g_ref, h_s, *, rope):
    j = pl.program_id(1)
    n = x_ref.shape[1]
    dk = GDN_DK

    @pl.when(j == 0)
    def _():
        h_s[...] = _rms_mod(x_ref[0], m_ref[0]).astype(BF16)

    row = lax.broadcasted_iota(jnp.int32, (n, 1), 0)

    def conv_silu():
        return _silu(_seq_conv3(_dot_rows(h_s, w_ref[...]), cw_ref[...], row))

    @pl.when(j < 4)
    def _():
        z = conv_silu()
        qscale = jnp.where(j < 2, dk ** -0.5, 1.0).astype(F32)
        for hh in range(z.shape[1] // dk):
            zh = z[:, hh * dk:(hh + 1) * dk]
            t = (zh * lax.rsqrt(jnp.sum(zh * zh, axis=-1, keepdims=True) + NORM_EPS)) * qscale
            if rope:
                t = t * cos_ref[...] + pltpu.roll(t, dk // 2, 1) * sin_ref[...]
            o_ref[0, :, hh * dk:(hh + 1) * dk] = t.astype(o_ref.dtype)

    @pl.when((j >= 4) & (j < 6))
    def _():
        o_ref[0] = conv_silu().astype(o_ref.dtype)

    @pl.when((j >= 6) & (j < 8))
    def _():
        o_ref[0] = _dot_rows(h_s, w_ref[...]).astype(o_ref.dtype)

    @pl.when(j == 8)
    def _():
        ab = _dot_rows(h_s, wab_ref[...])
        lane = lax.broadcasted_iota(jnp.int32, ab.shape, 1) % LANES
        xs = ab + dt_ref[...]
        softplus = jnp.maximum(xs, 0.0) + jnp.log(1.0 + jnp.exp(-jnp.abs(xs)))
        g = -jnp.exp(al_ref[...]) * softplus
        beta = 1.0 / (1.0 + jnp.exp(-ab))
        g_ref[0] = jnp.where(lane < 2 * GDN_HB, g, jnp.where(lane < 4 * GDN_HB, beta, 0.0))


def _gdn_proj(x3, mod3, w_main, w_ab, conv_w, cosf, sinf, a_log_l, dt_l, rope):
    bn, n, d = x3.shape
    segs = mod3.shape[0]
    tc = 512
    nq = w_main.shape[1] // tc
    gl = w_ab.shape[1]
    mi = (lambda b, j: (b, 0, 0)) if segs > 1 else (lambda b, j: (0, 0, 0))
    const = lambda b, j: (0, 0)
    return pl.pallas_call(
        functools.partial(_gdn_proj_kernel, rope=rope),
        out_shape=(jax.ShapeDtypeStruct((bn, n, w_main.shape[1]), BF16),
                   jax.ShapeDtypeStruct((bn, n, gl), F32)),
        grid=(bn, nq + 1),
        in_specs=[pl.BlockSpec((1, n, d), lambda b, j: (b, 0, 0)),
                  pl.BlockSpec((1, 3, d), mi),
                  pl.BlockSpec((d, tc), lambda b, j: (0, jnp.minimum(j, nq - 1))),
                  pl.BlockSpec((d, gl), const),
                  pl.BlockSpec((3, tc), lambda b, j: (0, jnp.minimum(j, 5))),
                  pl.BlockSpec((n, GDN_DK), const), pl.BlockSpec((n, GDN_DK), const),
                  pl.BlockSpec((1, gl), const), pl.BlockSpec((1, gl), const)],
        out_specs=(pl.BlockSpec((1, n, tc), lambda b, j: (b, 0, jnp.minimum(j, nq - 1))),
                   pl.BlockSpec((1, n, gl), lambda b, j: (b, 0, 0))),
        scratch_shapes=[pltpu.VMEM((n, d), BF16)],
        compiler_params=_cparams(("arbitrary", "arbitrary")),
        name="gdn_proj",
    )(x3, mod3, w_main, w_ab, conv_w, cosf, sinf, a_log_l, dt_l)


def _gdn_scan_kernel(ql, kl, vl, gl, qc, kc, vc, gx, ol, oc, s_ref, u_s, wq_s, at_s, kd_s, la_s, *, hb):
    cs = GDN_CHUNK
    dk = GDN_DK
    ns = 2 * hb
    n_lat = ql.shape[1]
    n_ctx = qc.shape[1]
    s_ref[...] = jnp.zeros_like(s_ref)
    ol[...] = jnp.zeros_like(ol)
    oc[...] = jnp.zeros_like(oc)
    c2 = 2 * cs
    ii = lax.broadcasted_iota(jnp.int32, (c2, c2), 0)
    jj = lax.broadcasted_iota(jnp.int32, (c2, c2), 1)
    same = (ii // cs) == (jj // cs)
    eye = (ii == jj)[None]
    lower = (same & (ii >= jj))[None]
    upper = (same & (ii <= jj))[None]
    eye_f = eye.astype(F32)
    first_chunk = lax.broadcasted_iota(jnp.int32, (1, c2, 1), 1) < cs

    def run_window(qr, kr, vr, gr, orf, t0f, t0b, nb):
        npair = nb // 2
        bp = ns * npair
        bi = lax.broadcasted_iota(jnp.int32, (bp, c2, c2), 0)
        i3 = lax.broadcasted_iota(jnp.int32, (bp, c2, c2), 1)
        j3 = lax.broadcasted_iota(jnp.int32, (bp, c2, c2), 2)
        dij = (1 - 2 * ((bi // npair) % 2)) * (i3 - j3)
        same3 = (i3 // cs) == (j3 // cs)
        tri = same3 & (dij >= 0)
        stri = same3 & (dij > 0)

        def gates(t0, d):
            slab = gr[0, pl.ds(t0, nb * cs), :].reshape(npair, c2, LANES)
            g1 = slab.astype(BF16)
            r1 = slab - g1.astype(F32)
            g2 = r1.astype(BF16)
            g3 = (r1 - g2.astype(F32)).astype(BF16)
            t = jnp.broadcast_to((lower if d == 0 else upper).astype(BF16), (npair, c2, c2))
            gcum = _bdot(t, g1) + (_bdot(t, g2) + _bdot(t, g3))
            tot0 = jnp.sum(slab[:, :cs], axis=1, keepdims=True)
            tot1 = jnp.sum(slab[:, cs:], axis=1, keepdims=True)
            return slab, gcum, jnp.where(first_chunk, tot0, tot1)

        gate_d = (gates(t0f, 0), gates(t0b, 1))
        gc_l, be_l, gt_l, q_l, k_l, v_l = [], [], [], [], [], []
        for h in range(hb):
            for d in range(2):
                slab, gcum, gtot = gate_d[d]
                lg = d * hb + h
                lb = 2 * hb + lg
                gc_l.append(gcum[:, :, lg:lg + 1])
                be_l.append(slab[:, :, lb:lb + 1])
                gt_l.append(gtot[:, :, lg:lg + 1])
                t0 = t0f if d == 0 else t0b
                cols = slice(h * dk, (h + 1) * dk)
                q_l.append(qr[0, pl.ds(t0, nb * cs), cols].reshape(npair, c2, dk))
                k_l.append(kr[0, pl.ds(t0, nb * cs), cols].reshape(npair, c2, dk))
                v_l.append(vr[0, pl.ds(t0, nb * cs), cols].reshape(npair, c2, dk))
        gc = jnp.concatenate(gc_l, axis=0)
        be = jnp.concatenate(be_l, axis=0)
        gt = jnp.concatenate(gt_l, axis=0)
        q = jnp.concatenate(q_l, axis=0)
        k = jnp.concatenate(k_l, axis=0)
        v = jnp.concatenate(v_l, axis=0)

        gcl = gc + jnp.zeros((1, 1, c2), F32)
        gcr = jnp.sum(jnp.where(eye, gcl, 0.0), axis=1, keepdims=True)
        decay = jnp.where(tri, jnp.exp(jnp.where(tri, gcl - gcr, 0.0)), 0.0)
        kq = _bdot_nt(jnp.concatenate([k, q], axis=1), k)
        a = jnp.where(stri, kq[:, :c2] * be * decay, 0.0)
        mm = lambda l, r: _bdot(l.astype(BF16), r.astype(BF16))
        x = mm(a, a)
        p = eye_f - a
        for _ in range(4):
            px = mm(jnp.concatenate([p, x], axis=1), x)
            p = p + px[:, :c2]
            x = px[:, c2:]
        p = p + mm(p, x)
        pb = p.astype(BF16)
        eg = jnp.exp(gc)
        kf = k.astype(F32)
        uw = _bdot(pb, jnp.concatenate([(v.astype(F32) * be).astype(BF16), (kf * (be * eg)).astype(BF16)], axis=2))
        u_s[:, :nb] = uw[:, :, :dk].reshape(ns, nb, cs, dk)
        w4 = uw[:, :, dk:].astype(BF16).reshape(ns, nb, cs, dk)
        qd4 = (q.astype(F32) * eg).astype(BF16).reshape(ns, nb, cs, dk)
        wq_s[:, :nb] = jnp.concatenate([w4, qd4], axis=2)
        attn = jnp.where(tri, kq[:, c2:] * decay, 0.0).astype(BF16)
        at_s[:, :nb] = jnp.concatenate([attn[:, None, :cs, :cs], attn[:, None, cs:, cs:]],
                                       axis=1).reshape(ns, nb, cs, cs)
        kd_s[:, :nb] = (kf * jnp.exp(gt - gc)).astype(BF16).reshape(ns, nb, cs, dk)
        la = jnp.exp(gt) + jnp.zeros((1, 1, dk), F32)
        la_s[:, :nb] = jnp.concatenate([la[:, None, 0:1], la[:, None, cs:cs + 1]], axis=1).reshape(ns, nb, 1, dk)

        for j in range(nb):
            pick = lambda ref: jnp.stack([ref[sidx, j if sidx % 2 == 0 else nb - 1 - j] for sidx in range(ns)], axis=0)
            s = s_ref[...]
            sbf = s.astype(BF16)
            ws_qs = _bdot(pick(wq_s), sbf)
            v_new = pick(u_s) - ws_qs[:, :cs]
            vb = v_new.astype(BF16)
            o = ws_qs[:, cs:] + _bdot(pick(at_s), vb)
            s_ref[...] = s * pick(la_s) + _bdot_tn(pick(kd_s), vb)
            for h in range(hb):
                cols = slice(h * dk, (h + 1) * dk)
                orf[0, pl.ds(t0f + j * cs, cs), cols] += o[2 * h]
                orf[0, pl.ds(t0b + (nb - 1 - j) * cs, cs), cols] += o[2 * h + 1]

    run_window(qc, kc, vc, gx, oc, 0, 0, n_ctx // cs)
    wrows = GDN_NB * cs
    nwin = n_lat // wrows

    def body(wi, carry):
        t0f = pl.multiple_of(wi * wrows, wrows)
        t0b = pl.multiple_of((nwin - 1 - wi) * wrows, wrows)
        run_window(ql, kl, vl, gl, ol, t0f, t0b, GDN_NB)
        return carry

    lax.fori_loop(0, nwin, body, 0)


def _gdn_scan(pl_, gl_, pc_, gc_):
    bn, n_lat, _ = pl_.shape
    n_ctx = pc_.shape[1]
    hb, dk, cs = GDN_HB, GDN_DK, GDN_CHUNK
    hd = GDN_HEADS * dk
    ngrp = GDN_HEADS // hb
    wcol = hb * dk
    nbm = max(GDN_NB, n_ctx // cs)

    def sec(n, k):
        return pl.BlockSpec((1, n, wcol), lambda b, g: (b, 0, k * ngrp + g))

    gate = lambda n: pl.BlockSpec((1, n, LANES), lambda b, g: (b, 0, g))
    out = lambda n: pl.BlockSpec((1, n, wcol), lambda b, g: (b, 0, g))
    return pl.pallas_call(
        functools.partial(_gdn_scan_kernel, hb=hb),
        out_shape=(jax.ShapeDtypeStruct((bn, n_lat, hd), F32), jax.ShapeDtypeStruct((bn, n_ctx, hd), F32)),
        grid=(bn, ngrp),
        in_specs=[sec(n_lat, 0), sec(n_lat, 1), sec(n_lat, 2), gate(n_lat),
                  sec(n_ctx, 0), sec(n_ctx, 1), sec(n_ctx, 2), gate(n_ctx)],
        out_specs=(out(n_lat), out(n_ctx)),
        scratch_shapes=[pltpu.VMEM((2 * hb, dk, dk), F32),
                        pltpu.VMEM((2 * hb, nbm, cs, dk), F32),
                        pltpu.VMEM((2 * hb, nbm, 2 * cs, dk), BF16),
                        pltpu.VMEM((2 * hb, nbm, cs, cs), BF16),
                        pltpu.VMEM((2 * hb, nbm, cs, dk), BF16),
                        pltpu.VMEM((2 * hb, nbm, 1, dk), F32)],
        compiler_params=_cparams(("arbitrary", "arbitrary")),
        name="gdn_scan",
    )(pl_, pl_, pl_, gl_, pc_, pc_, pc_, gc_)


def _gdn_out_kernel(o_ref, z_ref, ng_ref, w_ref, x_ref, g_ref, y_ref):
    dk = GDN_DK
    parts = []
    for h in range(o_ref.shape[1] // dk):
        cols = slice(h * dk, (h + 1) * dk)
        oh = o_ref[:, cols]
        ms = jnp.mean(oh * oh, axis=-1, keepdims=True)
        on = (oh * lax.rsqrt(ms + NORM_EPS)) * ng_ref[...]
        parts.append((on * _silu(z_ref[:, cols].astype(F32))).astype(BF16))
    a = jnp.concatenate(parts, axis=1)
    y_ref[...] = x_ref[...] + g_ref[0] * _dot_rows(a, w_ref[...])


def _gdn_out(o, proj, norm_g, w_out, x, gate, rows_per_seg, tl=512):
    r, hd = o.shape
    d = w_out.shape[1]
    tps = rows_per_seg // tl
    return pl.pallas_call(
        _gdn_out_kernel,
        out_shape=jax.ShapeDtypeStruct((r, d), F32),
        grid=(r // tl,),
        in_specs=[pl.BlockSpec((tl, hd), lambda i: (i, 0)),
                  pl.BlockSpec((tl, hd), lambda i: (i, 3)),
                  pl.BlockSpec((1, GDN_DK), lambda i: (0, 0)),
                  pl.BlockSpec((hd, d), lambda i: (0, 0)),
                  pl.BlockSpec((tl, d), lambda i: (i, 0)),
                  pl.BlockSpec((1, 1, d), lambda i: (i // tps, 0, 0))],
        out_specs=pl.BlockSpec((tl, d), lambda i: (i, 0)),
        compiler_params=_cparams(("arbitrary",)),
        name="gdn_out",
    )(o, proj, norm_g, w_out, x, gate)


def _proj3_kernel(x_ref, m_ref, w0_ref, w1_ref, w2_ref, cw_ref, cb_ref, *rest, hyena):
    j = pl.program_id(1)
    o_refs, h_s = rest[:-1], rest[-1]
    n = x_ref.shape[1]

    @pl.when(j == 0)
    def _():
        h_s[...] = _rms_mod(x_ref[0], m_ref[0]).astype(BF16)

    row = lax.broadcasted_iota(jnp.int32, (n, 1), 0)
    z0 = _dot_rows(h_s, w0_ref[...])
    z1 = _dot_rows(h_s, w1_ref[...])
    z2 = _dot_rows(h_s, w2_ref[...])
    if hyena:
        cb = cb_ref[...]
        x0 = _seq_conv3(z0, cw_ref[0], row) + cb[0:1]
        x1 = _seq_conv3(z1, cw_ref[1], row) + cb[1:2]
        v = _seq_conv3(z2, cw_ref[2], row) + cb[2:3]
        o_refs[0][0] = (v * x1).astype(BF16)
        o_refs[1][0] = x0.astype(BF16)
    else:
        o_refs[0][0] = (z1 * _seq_conv3(z2 * z0, cw_ref[0], row)).astype(BF16)


def _proj3(x3, mod3, w_in, cw, cb, hyena, tc=256):
    bn, n, d = x3.shape
    segs = mod3.shape[0]
    nj = d // tc
    mi = (lambda b, j: (b, 0, 0)) if segs > 1 else (lambda b, j: (0, 0, 0))
    wspec = lambda k: pl.BlockSpec((d, tc), lambda b, j: (0, k * nj + j))
    seq = pl.BlockSpec((1, n, tc), lambda b, j: (b, 0, j))
    n_out = 2 if hyena else 1
    outs = pl.pallas_call(
        functools.partial(_proj3_kernel, hyena=hyena),
        out_shape=tuple(jax.ShapeDtypeStruct((bn, n, d), BF16) for _ in range(n_out)),
        grid=(bn, nj),
        in_specs=[pl.BlockSpec((1, n, d), lambda b, j: (b, 0, 0)),
                  pl.BlockSpec((1, 3, d), mi),
                  wspec(0), wspec(1), wspec(2),
                  pl.BlockSpec((cw.shape[0], 3, tc), lambda b, j: (0, 0, j)),
                  pl.BlockSpec((3, tc), lambda b, j: (0, j))],
        out_specs=tuple(seq for _ in range(n_out)),
        scratch_shapes=[pltpu.VMEM((n, d), BF16)],
        compiler_params=_cparams(("arbitrary", "arbitrary")),
        name="hyena_proj" if hyena else "shortconv_proj",
    )(x3, mod3, w_in, w_in, w_in, cw, cb)
    return outs


def _final_norm_kernel(x_ref, g_ref, o_ref):
    x = x_ref[...]
    ms = jnp.mean(x * x, axis=-1, keepdims=True)
    o_ref[...] = (x * lax.rsqrt(ms + NORM_EPS)) * g_ref[...]


def _final_norm(x, g, tl=1024):
    r, d = x.shape
    return pl.pallas_call(
        _final_norm_kernel,
        out_shape=jax.ShapeDtypeStruct((r, d), F32),
        grid=(r // tl,),
        in_specs=[pl.BlockSpec((tl, d), lambda i: (i, 0)), pl.BlockSpec((1, d), lambda i: (0, 0))],
        out_specs=pl.BlockSpec((tl, d), lambda i: (i, 0)),
        compiler_params=_cparams(("arbitrary",)),
        name="final_norm",
    )(x, g)


def _hyena_kernel(u_ref, x0_ref, c_ref, s_ref, kr_ref, ki_ref, kn_ref, fb_ref, o_ref):
    ub = u_ref[0]
    l = ub.shape[0]
    mb = min(l, MM_ROWS)

    def rowblocks(fn):
        return jnp.concatenate([fn(slice(i, i + mb)) for i in range(0, l, mb)], axis=0)

    a = rowblocks(lambda r: _dot(c_ref[r, :], ub))
    b = rowblocks(lambda r: _dot(s_ref[r, :], ub))
    kr = kr_ref[...]
    ki = ki_ref[...]
    zr = (a * kr + b * ki).astype(BF16)
    zi = (b * kr - a * ki).astype(BF16)
    y = rowblocks(lambda r: _dot(c_ref[r, :], zr) + _dot(s_ref[r, :], zi))
    uf = ub.astype(F32)
    tpar = lax.broadcasted_iota(jnp.int32, (l, 1), 0) & 1
    alt = (1 - 2 * tpar).astype(F32)
    un = jnp.sum(uf * alt, axis=0, keepdims=True)
    y = y + alt * (un * kn_ref[...]) + uf * fb_ref[...]
    o_ref[0] = (y * x0_ref[0].astype(F32)).astype(o_ref.dtype)


def _hyena_conv(u, x0, cmat, smat, kr, ki, kn, fbias, td=256):
    b, l, d = u.shape
    seq = pl.BlockSpec((1, l, td), lambda i, j: (i, 0, j))
    mat = pl.BlockSpec((l, l), lambda i, j: (0, 0), pipeline_mode=pl.Buffered(1))
    spec = pl.BlockSpec((l, td), lambda i, j: (0, j))
    vec = pl.BlockSpec((1, td), lambda i, j: (0, j))
    return pl.pallas_call(
        _hyena_kernel,
        out_shape=jax.ShapeDtypeStruct((b, l, d), BF16),
        grid=(b, d // td),
        in_specs=[seq, seq, mat, mat, spec, spec, vec, vec],
        out_specs=seq,
        compiler_params=_cparams(("arbitrary", "arbitrary")),
        name="hyena_conv",
    )(u, x0, cmat, smat, kr, ki, kn, fbias)


def _mm3_kernel(a_ref, b_ref, o_ref):
    ah, al = _split_bf16(a_ref[...])
    bh, bl = _split_bf16(b_ref[...])
    o_ref[...] = _dot(ah, bh) + (_dot(ah, bl) + _dot(al, bh))


def _mm3(a, b, tm=256, tn=256):
    m, k = a.shape
    n = b.shape[1]
    tm = min(tm, m)
    return pl.pallas_call(
        _mm3_kernel,
        out_shape=jax.ShapeDtypeStruct((m, n), F32),
        grid=(n // tn, m // tm),
        in_specs=[pl.BlockSpec((tm, k), lambda j, i: (i, 0)),
                  pl.BlockSpec((k, tn), lambda j, i: (0, j))],
        out_specs=pl.BlockSpec((tm, tn), lambda j, i: (i, j)),
        compiler_params=_cparams(("arbitrary", "arbitrary")),
        name="matmul_f32x3",
    )(a, b)


def _na_kernel(q_ref, k_ref, v_ref, kc_ref, vc_ref, bias_ref, o_ref, *, rows):
    r = pl.program_id(1)
    rs = jnp.clip(r - NA_WIN_R // 2, 0, rows - NA_WIN_R)
    t0 = pl.multiple_of(rs * GRID_W, GRID_W)
    nk = NA_WIN_R * GRID_W
    scale = NA_DH ** -0.5
    lane = lax.broadcasted_iota(jnp.int32, (GRID_W, LANES), 1)
    first = lane < NA_DH
    nq = GRID_W
    scores = []
    for hp in range(NA_HEADS // 2):
        cols = slice(hp * LANES, (hp + 1) * LANES)
        q2 = q_ref[0, :, cols]
        zero = jnp.zeros_like(q2)
        qm = jnp.concatenate([jnp.where(first, q2, zero), jnp.where(first, zero, q2)], axis=0)
        bias = bias_ref[0, 2 * hp:2 * hp + 2].reshape(2 * nq, nk)
        s_loc = _dot_nt(qm, k_ref[0, pl.ds(t0, nk), cols]) * scale + bias
        s_ctx = _dot_nt(qm, kc_ref[0, :, cols]) * scale
        scores.append((s_loc, s_ctx))
    probs = []
    for s_loc, s_ctx in scores:
        m = jnp.maximum(jnp.max(s_loc, axis=1, keepdims=True), jnp.max(s_ctx, axis=1, keepdims=True))
        p_loc = jnp.exp(s_loc - m)
        p_ctx = jnp.exp(s_ctx - m)
        den = jnp.sum(p_loc, axis=1, keepdims=True) + jnp.sum(p_ctx, axis=1, keepdims=True)
        probs.append((p_loc.astype(BF16), p_ctx.astype(BF16), den))
    pairs = []
    for hp, (p_loc, p_ctx, den) in enumerate(probs):
        cols = slice(hp * LANES, (hp + 1) * LANES)
        o = (_dot(p_loc, v_ref[0, pl.ds(t0, nk), cols]) + _dot(p_ctx, vc_ref[0, :, cols])) / den
        pairs.append(jnp.where(first, o[:nq], o[nq:]).astype(o_ref.dtype))
    o_ref[0] = jnp.concatenate(pairs, axis=1)


def _na_attention(q, k, v, kc, vc, bias_tab):
    b, l, hd = q.shape
    lc = kc.shape[1]
    rows = l // GRID_W
    half = NA_WIN_R // 2
    return pl.pallas_call(
        functools.partial(_na_kernel, rows=rows),
        out_shape=jax.ShapeDtypeStruct((b, l, hd), BF16),
        grid=(b, rows),
        in_specs=[pl.BlockSpec((1, GRID_W, hd), lambda i, r: (i, r, 0)),
                  pl.BlockSpec((1, l, hd), lambda i, r: (i, 0, 0)),
                  pl.BlockSpec((1, l, hd), lambda i, r: (i, 0, 0)),
                  pl.BlockSpec((1, lc, hd), lambda i, r: (i, 0, 0)),
                  pl.BlockSpec((1, lc, hd), lambda i, r: (i, 0, 0)),
                  pl.BlockSpec((1, NA_HEADS, GRID_W, NA_WIN_R * GRID_W),
                               lambda i, r: (r - jnp.clip(r - half, 0, rows - NA_WIN_R), 0, 0, 0))],
        out_specs=pl.BlockSpec((1, GRID_W, hd), lambda i, r: (i, r, 0)),
        compiler_params=_cparams(("arbitrary", "arbitrary")),
        name="na_attention",
    )(q, k, v, kc, vc, bias_tab)


def _na_bias_table(rpb):
    qc = np.arange(GRID_W)
    kc = np.arange(GRID_W)
    cstart = np.clip(qc - NA_WIN_C // 2, 0, GRID_W - NA_WIN_C)
    valid = (kc[None, :] >= cstart[:, None]) & (kc[None, :] < cstart[:, None] + NA_WIN_C)
    dc = np.clip(kc[None, :] - qc[:, None] + NA_WIN_C - 1, 0, 2 * NA_WIN_C - 2)
    off = np.arange(NA_WIN_R)
    j = np.arange(NA_WIN_R)
    dr = j[None, :] - off[:, None] + NA_WIN_R - 1
    sel_r = (dr[:, :, None] == np.arange(2 * NA_WIN_R - 1)).astype(np.float32)
    sel_c = (dc[:, :, None] == np.arange(2 * NA_WIN_C - 1)).astype(np.float32)
    t = jnp.einsum("ojr,hrc,qkc->ohqjk", sel_r, rpb.astype(F32), sel_c, precision=lax.Precision.HIGHEST)
    t = jnp.where(jnp.asarray(valid)[None, None, :, None, :], t, jnp.float32(-1e30))
    return t.reshape(NA_WIN_R, NA_HEADS, GRID_W, NA_WIN_R * GRID_W)


def _rope_tables(n_tok, dh):
    pos = jnp.arange(n_tok)
    row = (pos // GRID_W).astype(F32)
    col = (pos % GRID_W).astype(F32)
    n_freq = dh // 4
    inv = ROPE_BASE ** (-jnp.arange(n_freq, dtype=F32) / n_freq)
    ang = jnp.concatenate([row[:, None] * inv, col[:, None] * inv], axis=-1)
    cos, sin = jnp.cos(ang), jnp.sin(ang)
    return jnp.concatenate([cos, cos], axis=-1), jnp.concatenate([-sin, sin], axis=-1)


def _gdn_gate_lanes():
    ngrp = GDN_HEADS // GDN_HB
    src = -np.ones((ngrp * LANES,), np.int64)
    for grp in range(ngrp):
        for t in range(2):
            for dr in range(2):
                for hh in range(GDN_HB):
                    lane = grp * LANES + t * 2 * GDN_HB + dr * GDN_HB + hh
                    src[lane] = t * 2 * GDN_HEADS + dr * GDN_HEADS + grp * GDN_HB + hh
    return src


_GDN_LANE_SRC = _gdn_gate_lanes()


def _gdn_layer(x, xc, mod_l, mod_c, gate_l, gate_c, bn, w_in, conv_w, a_log, dt_bias, norm_g, w_out, ctx_out):
    d = D_MODEL
    hd = GDN_HEADS * GDN_DK
    n_lat = x.shape[0] // bn
    n_ctx = xc.shape[0] // bn
    w_main = w_in[:, :4 * hd].astype(BF16)
    used = jnp.asarray(_GDN_LANE_SRC >= 0)
    lane_src = jnp.asarray(np.maximum(_GDN_LANE_SRC, 0))
    w_ab = jnp.where(used[None, :], w_in[:, 4 * hd:][:, lane_src], 0.0).astype(BF16)
    decay_lane = used & (lane_src < 2 * GDN_HEADS)
    a_log_l = jnp.where(decay_lane, a_log.reshape(-1)[lane_src % (2 * GDN_HEADS)], 0.0)[None, :].astype(F32)
    dt_l = jnp.where(decay_lane, dt_bias.reshape(-1)[lane_src % (2 * GDN_HEADS)], 0.0)[None, :].astype(F32)
    cosf, sinf = _rope_tables(n_lat, GDN_DK)
    dummy = jnp.zeros((n_ctx, GDN_DK), F32)
    p_l, g_l = _gdn_proj(x.reshape(bn, n_lat, d), mod_l, w_main, w_ab, conv_w, cosf, sinf, a_log_l, dt_l, True)
    p_c, g_c = _gdn_proj(xc.reshape(bn, n_ctx, d), mod_c, w_main, w_ab, conv_w, dummy, dummy, a_log_l, dt_l, False)
    o_l, o_c = _gdn_scan(p_l, g_l, p_c, g_c)
    wo = w_out.astype(BF16)
    ng = norm_g.astype(F32)[None, :]
    x_new = _gdn_out(o_l.reshape(bn * n_lat, hd), p_l.reshape(bn * n_lat, 4 * hd), ng, wo, x, gate_l, n_lat)
    xc_new = None
    if ctx_out:
        xc_new = _gdn_out(o_c.reshape(bn * n_ctx, hd), p_c.reshape(bn * n_ctx, 4 * hd), ng, wo, xc, gate_c,
                          xc.shape[0])
    return x_new, xc_new


def _hyena_filter_taps(n_tok, w1, b1, w2, b2, w3, b3, freq, w4):
    t = jnp.linspace(0.0, 1.0, n_tok, dtype=F32)[:, None]
    bands = (HY_EMB_DIM - 1) // 2
    wpos = 2.0 * math.pi * jnp.arange(n_tok, dtype=F32)[:, None] / n_tok
    fr = jnp.linspace(1e-4, bands - 1, bands, dtype=F32)[None, :]
    z = jnp.concatenate([t, jnp.cos(fr * wpos), -jnp.sin(fr * wpos)], axis=-1)
    freq = freq.astype(F32)
    hdn = jnp.sin(freq[0] * (z @ w1.astype(F32) + b1.astype(F32)))
    hdn = jnp.sin(freq[1] * (hdn @ w2.astype(F32) + b2.astype(F32)))
    hdn = jnp.sin(freq[2] * (hdn @ w3.astype(F32) + b3.astype(F32)))
    filt = (hdn @ w4.astype(F32)).reshape(n_tok, 2, D_MODEL)
    deltas = jnp.abs(jnp.linspace(HY_MIN_DECAY, HY_MAX_DECAY, D_MODEL, dtype=F32))
    filt = filt * jnp.exp(-t * deltas)[:, None, :]
    return filt[:, 0], filt[:, 1]


def _dft_mats(l):
    k = jnp.arange(l, dtype=jnp.int32)
    ks = (k[:, None] * k[None, :]) % (2 * l)
    ang = ks.astype(F32) * (math.pi / l)
    return jnp.cos(ang), jnp.sin(ang)


def _hyena_layer(rows, mod3, gate, rows_per_seg, bn, w_in, short_w, short_b, f_w1, f_b1, f_w2, f_b2, f_w3, f_b3,
                 f_freq, f_w4, f_bias, w_out):
    d = D_MODEL
    n_tok = rows.shape[0] // bn
    cw = short_w.reshape(3, 3, d).transpose(1, 0, 2).astype(F32)
    u, x0 = _proj3(rows.reshape(bn, n_tok, d), mod3, w_in.astype(BF16), cw, short_b.reshape(3, d).astype(F32),
                   hyena=True)
    hf, hb = _hyena_filter_taps(n_tok, f_w1, f_b1, f_w2, f_b2, f_w3, f_b3, f_freq, f_w4)
    hb = hb.at[0].set(0.0)
    cmat, smat = _dft_mats(n_tok)
    n2 = 2 * n_tok
    wk = jnp.full((n_tok, 1), 2.0 / n2, F32).at[0, 0].set(1.0 / n2)
    kr = _mm3(cmat, hf + hb) * wk
    ki = _mm3(smat, hb - hf) * wk
    alt = (1.0 - 2.0 * (jnp.arange(n_tok) % 2)).astype(F32)[:, None]
    kn = jnp.sum((hf + hb) * alt, axis=0, keepdims=True) / n2
    y = _hyena_conv(u, x0, cmat.astype(BF16), smat.astype(BF16), kr, ki, kn, f_bias.astype(F32)[None, :])
    return _mm_res(y.reshape(rows.shape[0], d), w_out.astype(BF16), rows, gate, rows_per_seg)


def _na_layer(x, xc, mod_l, mod_c, gate_l, bn, w_qkv, rpb, w_out):
    hd = NA_HEADS * NA_DH
    n_lat = x.shape[0] // bn
    n_ctx = xc.shape[0] // bn
    wq = w_qkv.astype(BF16)
    z = _modmm(x, mod_l, wq, n_lat, out_dtype=BF16).reshape(bn, n_lat, 3 * hd)
    zc = _modmm(xc, mod_c, wq, xc.shape[0], out_dtype=BF16).reshape(bn, n_ctx, 3 * hd)
    o = _na_attention(z[..., :hd], z[..., hd:2 * hd], z[..., 2 * hd:], zc[..., hd:2 * hd], zc[..., 2 * hd:],
                      _na_bias_table(rpb))
    return _mm_res(o.reshape(x.shape[0], hd), w_out.astype(BF16), x, gate_l, n_lat)


def _shortconv_layer(rows, mod3, gate, rows_per_seg, bn, w_in, conv_w, w_out):
    d = D_MODEL
    n_tok = rows.shape[0] // bn
    (a,) = _proj3(rows.reshape(bn, n_tok, d), mod3, w_in.astype(BF16), conv_w.astype(F32)[None],
                  jnp.zeros((3, d), F32), hyena=False)
    return _mm_res(a.reshape(rows.shape[0], d), w_out.astype(BF16), rows, gate, rows_per_seg)


def kernel(x, c, ctx, c_ctx, ln_g, w_mod, b_mod, final_g, gdn_w_in, gdn_conv_w, gdn_a_log, gdn_dt_bias, gdn_norm_g, gdn_w_out, hy_w_in, hy_short_w, hy_short_b, hy_f_w1, hy_f_b1, hy_f_w2, hy_f_b2, hy_f_w3, hy_f_b3, hy_f_freq, hy_f_w4, hy_f_bias, hy_w_out, na_w_qkv, na_rpb, na_w_out, sc_w_in, sc_conv_w, sc_w_out, moe_w_gr, moe_b_gr, moe_w_er, moe_b_er, moe_w_gate, moe_w_up, moe_w_down):
    bn, n_lat, d = x.shape
    n_ctx = ctx.shape[1]
    rows_ctx = bn * n_ctx
    pad = (-(bn + 1)) % 8
    c_all = jnp.concatenate([c, c_ctx[None, :], jnp.zeros((pad, d), F32)], axis=0)
    mods = _mod_vectors(c_all, w_mod, b_mod)
    xr = x.reshape(bn * n_lat, d)
    xcr = ctx.reshape(rows_ctx, d)
    for i in range(DEPTH):
        m, s = i % N_MIXERS, i // N_MIXERS
        reads_ctx = m in CTX_READING_MIXERS
        upd_ctx = any((j % N_MIXERS) in CTX_READING_MIXERS for j in range(i + 1, DEPTH))
        mv = mods[i].reshape(-1, 6, d)

        def mod3(rows, which, norm_g):
            sh, sc = mv[rows, 3 * which], mv[rows, 3 * which + 1]
            return jnp.stack([jnp.broadcast_to(norm_g, sh.shape), sh, sc], axis=1)

        lat = slice(0, bn)
        cx = slice(bn, bn + 1)
        mod_l1, mod_l2 = mod3(lat, 0, ln_g[i, 0]), mod3(lat, 1, ln_g[i, 1])
        mod_c1, mod_c2 = mod3(cx, 0, ln_g[i, 0]), mod3(cx, 1, ln_g[i, 1])
        g1_l, g2_l = mv[lat, 2][:, None, :], mv[lat, 5][:, None, :]
        g1_c, g2_c = mv[cx, 2][:, None, :], mv[cx, 5][:, None, :]
        xc_new = None
        if m == 0:
            xr, xc_new = _gdn_layer(xr, xcr, mod_l1, mod_c1, g1_l, g1_c, bn, gdn_w_in[s], gdn_conv_w[s],
                                    gdn_a_log[s], gdn_dt_bias[s], gdn_norm_g[s], gdn_w_out[s], upd_ctx)
        elif m == 1:
            hy = (hy_w_in[s], hy_short_w[s], hy_short_b[s], hy_f_w1[s], hy_f_b1[s], hy_f_w2[s], hy_f_b2[s],
                  hy_f_w3[s], hy_f_b3[s], hy_f_freq[s], hy_f_w4[s], hy_f_bias[s], hy_w_out[s])
            xr = _hyena_layer(xr, mod_l1, g1_l, n_lat, bn, *hy)
            if upd_ctx:
                xc_new = _hyena_layer(xcr, mod_c1, g1_c, rows_ctx, bn, *hy)
        elif m == 2:
            xr = _na_layer(xr, xcr, mod_l1, mod_c1, g1_l, bn, na_w_qkv[s], na_rpb[s], na_w_out[s])
            assert not upd_ctx
        else:
            xr = _shortconv_layer(xr, mod_l1, g1_l, n_lat, bn, sc_w_in[s], sc_conv_w[s], sc_w_out[s])
            if upd_ctx:
                xc_new = _shortconv_layer(xcr, mod_c1, g1_c, rows_ctx, bn, sc_w_in[s], sc_conv_w[s], sc_w_out[s])
        moe = (moe_w_gr[i], moe_b_gr[i], moe_w_er[i], moe_b_er[i], moe_w_gate[i], moe_w_up[i], moe_w_down[i])
        if upd_ctx:
            xr, xcr = _hier_moe_residual(xr, mod_l2, g2_l, n_lat, *moe, xc=xc_new, mod3c=mod_c2, gatec=g2_c)
        else:
            xr = _hier_moe_residual(xr, mod_l2, g2_l, n_lat, *moe)
    return _final_norm(xr, final_g.astype(F32)[None, :]).reshape(bn, n_lat, d)
```

```python
import functools
import math

import numpy as np
import jax
import jax.numpy as jnp
from jax import lax
from jax.experimental import pallas as pl
from jax.experimental.pallas import tpu as pltpu

F32 = jnp.float32
BF16 = jnp.bfloat16

D_MODEL = 1024
DEPTH = 4
GRID_W = 64
N_MIXERS = 4
CTX_READING_MIXERS = (0, 2)
NORM_EPS = 1e-6

GDN_HEADS = 8
GDN_DK = 128
GDN_CHUNK = 64
ROPE_BASE = 10000.0

HY_EMB_DIM = 33
HY_DECAY_TARGET = 1e-2
HY_MAX_DECAY = math.log(HY_DECAY_TARGET) / 0.3
HY_MIN_DECAY = math.log(HY_DECAY_TARGET) / 1.5

NA_HEADS = 16
NA_DH = 64
NA_WIN_R = 8
NA_WIN_C = 16

MOE_GROUPS = 4
MOE_EPG = 8
MOE_FF = 256
MOE_PAIRS = MOE_EPG * (MOE_EPG - 1) // 2
MOE_CLASSES = MOE_GROUPS * MOE_PAIRS
MOE_TM = 256

LANES = 128
VMEM_LIMIT = 56 << 20


def _cparams(sem, vmem=VMEM_LIMIT):
    return pltpu.CompilerParams(dimension_semantics=sem, vmem_limit_bytes=vmem)


def _rms_mod(x, m):
    ms = jnp.mean(x * x, axis=-1, keepdims=True)
    xn = (x * lax.rsqrt(ms + NORM_EPS)) * m[0:1]
    return xn * (1.0 + m[2:3]) + m[1:2]


def _split_bf16(x):
    hi = x.astype(BF16)
    lo = (x - hi.astype(F32)).astype(BF16)
    return hi, lo


def _dot(a, b):
    return jnp.dot(a, b, preferred_element_type=F32)


MM_ROWS = 256


def _dot_rows(a, b):
    m = a.shape[0]
    if m <= MM_ROWS:
        return _dot(a[...], b)
    return jnp.concatenate([_dot(a[i:i + MM_ROWS], b) for i in range(0, m, MM_ROWS)], axis=0)


def _dot_nt(a, b):
    return lax.dot_general(a, b, (((1,), (1,)), ((), ())), preferred_element_type=F32)


def _bdot(a, b):
    return lax.dot_general(a, b, (((2,), (1,)), ((0,), (0,))), preferred_element_type=F32)


def _bdot_nt(a, b):
    return lax.dot_general(a, b, (((2,), (2,)), ((0,), (0,))), preferred_element_type=F32)


def _bdot_tn(a, b):
    return lax.dot_general(a, b, (((1,), (1,)), ((0,), (0,))), preferred_element_type=F32)


def _bdot3(a, b):
    ah, al = _split_bf16(a)
    bh, bl = _split_bf16(b)
    return _bdot(ah, bh) + (_bdot(ah, bl) + _bdot(al, bh))


def _silu(x):
    return x * (1.0 / (1.0 + jnp.exp(-x)))


def _modvec_kernel(c_ref, w_ref, b_ref, o_ref):
    c = c_ref[...]
    a = _silu(c)
    ah, al = _split_bf16(a)
    wh, wl = _split_bf16(w_ref[0])
    o_ref[0] = _dot(ah, wh) + (_dot(ah, wl) + _dot(al, wh)) + b_ref[0]


def _mod_vectors(c_all, w_mod, b_mod):
    rows, d = c_all.shape
    depth, _, n = w_mod.shape
    tn = 1536
    return pl.pallas_call(
        _modvec_kernel,
        out_shape=jax.ShapeDtypeStruct((depth, rows, n), F32),
        grid=(depth, n // tn),
        in_specs=[pl.BlockSpec((rows, d), lambda i, j: (0, 0)),
                  pl.BlockSpec((1, d, tn), lambda i, j: (i, 0, j)),
                  pl.BlockSpec((1, 1, tn), lambda i, j: (i, 0, j))],
        out_specs=pl.BlockSpec((1, rows, tn), lambda i, j: (i, 0, j)),
        compiler_params=_cparams(("arbitrary", "arbitrary")),
        name="mod_vectors",
    )(c_all, w_mod, b_mod.reshape(depth, 1, n))


def _modmm_kernel(x_ref, m_ref, w_ref, o_ref, *, ncol):
    h = _rms_mod(x_ref[...], m_ref[0]).astype(BF16)
    n = w_ref.shape[1]
    for n0 in range(0, n, ncol):
        o_ref[:, n0:n0 + ncol] = _dot_rows(h, w_ref[:, n0:n0 + ncol]).astype(o_ref.dtype)


def _modmm(x, mod3, w, rows_per_seg, out_dtype=F32, tl=512):
    r, d = x.shape
    n = w.shape[1]
    tps = rows_per_seg // tl
    ncol = 512 if n % 512 == 0 else n
    return pl.pallas_call(
        functools.partial(_modmm_kernel, ncol=ncol),
        out_shape=jax.ShapeDtypeStruct((r, n), out_dtype),
        grid=(r // tl,),
        in_specs=[pl.BlockSpec((tl, d), lambda i: (i, 0)),
                  pl.BlockSpec((1, 3, d), lambda i: (i // tps, 0, 0)),
                  pl.BlockSpec((d, n), lambda i: (0, 0))],
        out_specs=pl.BlockSpec((tl, n), lambda i: (i, 0)),
        compiler_params=_cparams(("arbitrary",)),
        name="mod_matmul",
    )(x, mod3, w)


def _mmres_kernel(a_ref, w_ref, x_ref, g_ref, o_ref):
    o_ref[...] = x_ref[...] + g_ref[0] * _dot_rows(a_ref, w_ref[...])


def _mm_res(a, w, x, gate, rows_per_seg, tl=512):
    r, k = a.shape
    d = w.shape[1]
    tps = rows_per_seg // tl
    return pl.pallas_call(
        _mmres_kernel,
        out_shape=jax.ShapeDtypeStruct((r, d), F32),
        grid=(r // tl,),
        in_specs=[pl.BlockSpec((tl, k), lambda i: (i, 0)),
                  pl.BlockSpec((k, d), lambda i: (0, 0)),
                  pl.BlockSpec((tl, d), lambda i: (i, 0)),
                  pl.BlockSpec((1, 1, d), lambda i: (i // tps, 0, 0))],
        out_specs=pl.BlockSpec((tl, d), lambda i: (i, 0)),
        compiler_params=_cparams(("arbitrary",)),
        name="matmul_residual",
    )(a, w, x, gate)


def _router_kernel(xa_ref, xb_ref, m_ref, wh_ref, wl_ref, b_ref, h_ref, route_ref, cnt_ref, carry_ref, *, na):
    i = pl.program_id(0)

    @pl.when(i == 0)
    def _():
        carry_ref[...] = jnp.zeros_like(carry_ref)

    h = _rms_mod(jnp.where(i < na, xa_ref[...], xb_ref[...]), m_ref[0])
    hh, hl = _split_bf16(h)
    h_ref[...] = hh
    wh = wh_ref[...]
    logits = _dot_rows(hh, wh) + (_dot_rows(hl, wh) + _dot_rows(hh, wl_ref[...])) + b_ref[...]
    t = logits.shape[0]
    lane = lax.broadcasted_iota(jnp.int32, logits.shape, 1).astype(F32)
    neg = -jnp.inf
    big = 1e9

    def first_argmax(vals):
        m = jnp.max(vals, axis=1, keepdims=True)
        idx = jnp.min(jnp.where(vals == m, lane, big), axis=1, keepdims=True)
        return m, idx

    lg = jnp.where(lane < MOE_GROUPS, logits, neg)
    mg, gsel = first_argmax(lg)
    wg = 1.0 / jnp.sum(jnp.exp(lg - mg), axis=1, keepdims=True)
    e_first = MOE_GROUPS + gsel * MOE_EPG
    in_grp = (lane >= e_first) & (lane < e_first + MOE_EPG)
    le = jnp.where(in_grp, logits, neg)
    m1, i1 = first_argmax(le)
    m2, i2 = first_argmax(jnp.where(lane == i1, neg, le))
    p = jnp.exp(m2 - m1)
    w1 = wg / (1.0 + p)
    w2 = wg * p / (1.0 + p)
    l1 = i1 - e_first
    l2 = i2 - e_first
    lo = jnp.minimum(l1, l2)
    hi = jnp.maximum(l1, l2)
    first_is_lo = l1 < l2
    wlo = jnp.where(first_is_lo, w1, w2)
    whi = jnp.where(first_is_lo, w2, w1)
    cls = gsel * MOE_PAIRS + lo * (2 * MOE_EPG - 1 - lo) * 0.5 + (hi - lo - 1.0)

    onehot = lane == cls
    ri = lax.broadcasted_iota(jnp.int32, (t, t), 0)
    ci = lax.broadcasted_iota(jnp.int32, (t, t), 1)
    before = (ci < ri).astype(BF16)
    prefix = _dot(before, onehot.astype(BF16)) + carry_ref[...]
    rank = jnp.sum(jnp.where(onehot, prefix, 0.0), axis=1, keepdims=True)
    carry = carry_ref[...] + jnp.sum(onehot.astype(F32), axis=0, keepdims=True)
    carry_ref[...] = carry
    cnt_ref[...] = carry
    route_ref[...] = jnp.where(lane == 0, cls, jnp.where(lane == 1, rank, jnp.where(
        lane == 2, wlo, jnp.where(lane == 3, whi, 0.0))))


def _router(xa, xb, mod3, w_r, b_r, rows_per_seg_a, tl=512):
    ra, d = xa.shape
    rb = 0 if xb is None else xb.shape[0]
    xb = xa if xb is None else xb
    r = ra + rb
    na = ra // tl
    tps = rows_per_seg_a // tl
    nseg = mod3.shape[0]
    wh, wl = _split_bf16(w_r)
    return pl.pallas_call(
        functools.partial(_router_kernel, na=na),
        out_shape=(jax.ShapeDtypeStruct((r, d), BF16),
                   jax.ShapeDtypeStruct((r, LANES), F32),
                   jax.ShapeDtypeStruct((1, LANES), F32)),
        grid=(r // tl,),
        in_specs=[pl.BlockSpec((tl, d), lambda i: (jnp.minimum(i, na - 1), 0)),
                  pl.BlockSpec((tl, d), lambda i: (jnp.maximum(i - na, 0), 0)),
                  pl.BlockSpec((1, 3, d), lambda i: (jnp.minimum(i // tps, nseg - 1), 0, 0)),
                  pl.BlockSpec((d, LANES), lambda i: (0, 0)),
                  pl.BlockSpec((d, LANES), lambda i: (0, 0)),
                  pl.BlockSpec((1, LANES), lambda i: (0, 0))],
        out_specs=(pl.BlockSpec((tl, d), lambda i: (i, 0)),
                   pl.BlockSpec((tl, LANES), lambda i: (i, 0)),
                   pl.BlockSpec((1, LANES), lambda i: (0, 0))),
        scratch_shapes=[pltpu.VMEM((1, LANES), F32)],
        compiler_params=_cparams(("arbitrary",)),
        name="moe_router",
    )(xa, xb, mod3, wh, wl, b_r)


def _moe_kernel(elo_ref, ehi_ref, val_ref, x_ref, wt_ref, wgl, wul, wdl, wgh, wuh, wdh, o_ref):
    t = pl.program_id(0)

    @pl.when(val_ref[t] != 0)
    def _():
        x = x_ref[...]
        wt = wt_ref[...]
        w16 = lambda ref: ref[0].astype(BF16)
        gl, ul, gh, uh = _dot(x, w16(wgl)), _dot(x, w16(wul)), _dot(x, w16(wgh)), _dot(x, w16(wuh))
        hl = (_silu(gl) * ul) * wt[:, 0:1]
        hh = (_silu(gh) * uh) * wt[:, 1:2]
        o_ref[...] = (_dot(hl.astype(BF16), w16(wdl)) + _dot(hh.astype(BF16), w16(wdh))).astype(o_ref.dtype)

    @pl.when(val_ref[t] == 0)
    def _():
        o_ref[...] = jnp.zeros_like(o_ref)


def _moe_experts(xs, wts, tile_elo, tile_ehi, tile_val, w_gate, w_up, w_down):
    p, d = xs.shape
    ff = w_gate.shape[2]
    nt = p // MOE_TM
    lo = lambda t, elo, ehi, val: (elo[t], 0, 0)
    hi = lambda t, elo, ehi, val: (ehi[t], 0, 0)
    row = lambda t, elo, ehi, val: (t, 0)
    grid_spec = pltpu.PrefetchScalarGridSpec(
        num_scalar_prefetch=3,
        grid=(nt,),
        in_specs=[pl.BlockSpec((MOE_TM, d), row),
                  pl.BlockSpec((MOE_TM, 2), row),
                  pl.BlockSpec((1, d, ff), lo), pl.BlockSpec((1, d, ff), lo), pl.BlockSpec((1, ff, d), lo),
                  pl.BlockSpec((1, d, ff), hi), pl.BlockSpec((1, d, ff), hi), pl.BlockSpec((1, ff, d), hi)],
        out_specs=pl.BlockSpec((MOE_TM, d), row),
    )
    return pl.pallas_call(
        _moe_kernel,
        out_shape=jax.ShapeDtypeStruct((p, d), BF16),
        grid_spec=grid_spec,
        compiler_params=_cparams(("arbitrary",)),
        name="moe_experts",
    )(tile_elo, tile_ehi, tile_val, xs, wts, w_gate, w_up, w_down, w_gate, w_up, w_down)


def _gated_add_kernel(x_ref, y_ref, g_ref, o_ref):
    o_ref[...] = x_ref[...] + g_ref[0] * y_ref[...].astype(F32)


def _gated_add(x, y, gate, rows_per_seg, tl=1024):
    r, d = x.shape
    tps = rows_per_seg // tl
    row = pl.BlockSpec((tl, d), lambda i: (i, 0))
    return pl.pallas_call(
        _gated_add_kernel,
        out_shape=jax.ShapeDtypeStruct((r, d), F32),
        grid=(r // tl,),
        in_specs=[row, row, pl.BlockSpec((1, 1, d), lambda i: (i // tps, 0, 0))],
        out_specs=row,
        compiler_params=_cparams(("arbitrary",)),
        name="moe_combine",
    )(x, y, gate)


def _class_tables():
    elo = np.zeros((MOE_CLASSES,), np.int32)
    ehi = np.zeros((MOE_CLASSES,), np.int32)
    for g in range(MOE_GROUPS):
        for lo in range(MOE_EPG):
            for hi in range(lo + 1, MOE_EPG):
                c = g * MOE_PAIRS + lo * (2 * MOE_EPG - 1 - lo) // 2 + (hi - lo - 1)
                elo[c] = g * MOE_EPG + lo
                ehi[c] = g * MOE_EPG + hi
    return elo, ehi


_CLASS_ELO, _CLASS_EHI = _class_tables()


def _hier_moe_residual(x, mod3, gate, rows_per_seg, w_gr, b_gr, w_er, b_er, w_gate, w_up, w_down,
                       xc=None, mod3c=None, gatec=None):
    ra, d = x.shape
    r = ra + (0 if xc is None else xc.shape[0])
    w_r = jnp.zeros((d, LANES), F32).at[:, :MOE_GROUPS].set(w_gr)
    w_r = w_r.at[:, MOE_GROUPS:MOE_GROUPS + MOE_GROUPS * MOE_EPG].set(w_er)
    b_r = jnp.zeros((1, LANES), F32).at[0, :MOE_GROUPS].set(b_gr)
    b_r = b_r.at[0, MOE_GROUPS:MOE_GROUPS + MOE_GROUPS * MOE_EPG].set(b_er.reshape(-1))
    mods = mod3 if xc is None else jnp.concatenate([mod3, mod3c], axis=0)
    h, route, cnt = _router(x, xc, mods, w_r, b_r, rows_per_seg)

    cls = route[:, 0:1].astype(jnp.int32)
    rank = route[:, 1].astype(jnp.int32)
    counts = cnt[0].astype(jnp.int32)
    pc = ((counts + MOE_TM - 1) // MOE_TM) * MOE_TM
    pend = jnp.cumsum(pc)[:MOE_CLASSES]
    pstart = jnp.cumsum(pc) - pc
    dest = rank + jnp.sum(jnp.where(cls == jnp.arange(LANES, dtype=jnp.int32)[None, :], pstart[None, :], 0), axis=1)
    p_rows = ((r + MOE_CLASSES * (MOE_TM - 1)) // MOE_TM + 1) * MOE_TM
    nt = p_rows // MOE_TM
    vals = jnp.stack([jnp.arange(r, dtype=F32), route[:, 2], route[:, 3]], axis=1)
    pad_src = (jnp.arange(p_rows, dtype=jnp.int32) % r).astype(F32)
    tab0 = jnp.stack([pad_src, jnp.zeros((p_rows,), F32), jnp.zeros((p_rows,), F32)], axis=1)
    tab = tab0.at[dest].set(vals, unique_indices=True, mode="promise_in_bounds")
    src = tab[:, 0].astype(jnp.int32)
    wts = tab[:, 1:3]
    tile_p0 = jnp.arange(nt, dtype=jnp.int32) * MOE_TM
    tile_val = (tile_p0 < pend[-1]).astype(jnp.int32)
    first_row = jnp.minimum(tile_p0, pend[-1] - 1)
    tile_cls = jnp.sum((pend[None, :] <= first_row[:, None]).astype(jnp.int32), axis=1)
    tile_cls = jnp.clip(tile_cls, 0, MOE_CLASSES - 1)
    tile_elo = jnp.asarray(_CLASS_ELO)[tile_cls]
    tile_ehi = jnp.asarray(_CLASS_EHI)[tile_cls]

    xs = h.at[src].get(mode="promise_in_bounds")
    ys = _moe_experts(xs, wts, tile_elo, tile_ehi, tile_val, w_gate, w_up, w_down)
    y = ys.at[dest].get(mode="promise_in_bounds", unique_indices=True)
    x_new = _gated_add(x, y, gate, rows_per_seg)
    if xc is None:
        return x_new
    return x_new, xc + gatec[0] * y[ra:]


GDN_HB = 4
GDN_NB = 4


def _seq_conv3(z, w, row):
    n = z.shape[0]
    zp = jnp.where(row == 0, 0.0, pltpu.roll(z, 1, 0))
    zn = jnp.where(row == n - 1, 0.0, pltpu.roll(z, n - 1, 0))
    return w[0:1] * zp + w[1:2] * z + w[2:3] * zn


def _gdn_proj_kernel(x_ref, m_ref, w_ref, wab_ref, cw_ref, cos_ref, sin_ref, al_ref, dt_ref,
                     o_ref, g_ref, h_s, *, rope):
    j = pl.program_id(1)
    n = x_ref.shape[1]
    dk = GDN_DK

    @pl.when(j == 0)
    def _():
        h_s[...] = _rms_mod(x_ref[0], m_ref[0]).astype(BF16)

    row = lax.broadcasted_iota(jnp.int32, (n, 1), 0)

    def conv_silu():
        return _silu(_seq_conv3(_dot_rows(h_s, w_ref[...]), cw_ref[...], row))

    @pl.when(j < 4)
    def _():
        z = conv_silu()
        qscale = jnp.where(j < 2, dk ** -0.5, 1.0).astype(F32)
        for hh in range(z.shape[1] // dk):
            zh = z[:, hh * dk:(hh + 1) * dk]
            t = (zh * lax.rsqrt(jnp.sum(zh * zh, axis=-1, keepdims=True) + NORM_EPS)) * qscale
            if rope:
                t = t * cos_ref[...] + pltpu.roll(t, dk // 2, 1) * sin_ref[...]
            o_ref[0, :, hh * dk:(hh + 1) * dk] = t.astype(o_ref.dtype)

    @pl.when((j >= 4) & (j < 6))
    def _():
        o_ref[0] = conv_silu().astype(o_ref.dtype)

    @pl.when((j >= 6) & (j < 8))
    def _():
        o_ref[0] = _dot_rows(h_s, w_ref[...]).astype(o_ref.dtype)

    @pl.when(j == 8)
    def _():
        ab = _dot_rows(h_s, wab_ref[...])
        lane = lax.broadcasted_iota(jnp.int32, ab.shape, 1) % LANES
        xs = ab + dt_ref[...]
        softplus = jnp.maximum(xs, 0.0) + jnp.log(1.0 + jnp.exp(-jnp.abs(xs)))
        g = -jnp.exp(al_ref[...]) * softplus
        beta = 1.0 / (1.0 + jnp.exp(-ab))
        g_ref[0] = jnp.where(lane < 2 * GDN_HB, g, jnp.where(lane < 4 * GDN_HB, beta, 0.0))


def _gdn_proj(x3, mod3, w_main, w_ab, conv_w, cosf, sinf, a_log_l, dt_l, rope):
    bn, n, d = x3.shape
    segs = mod3.shape[0]
    tc = 512
    nq = w_main.shape[1] // tc
    gl = w_ab.shape[1]
    mi = (lambda b, j: (b, 0, 0)) if segs > 1 else (lambda b, j: (0, 0, 0))
    const = lambda b, j: (0, 0)
    return pl.pallas_call(
        functools.partial(_gdn_proj_kernel, rope=rope),
        out_shape=(jax.ShapeDtypeStruct((bn, n, w_main.shape[1]), BF16),
                   jax.ShapeDtypeStruct((bn, n, gl), F32)),
        grid=(bn, nq + 1),
        in_specs=[pl.BlockSpec((1, n, d), lambda b, j: (b, 0, 0)),
                  pl.BlockSpec((1, 3, d), mi),
                  pl.BlockSpec((d, tc), lambda b, j: (0, jnp.minimum(j, nq - 1))),
                  pl.BlockSpec((d, gl), const),
                  pl.BlockSpec((3, tc), lambda b, j: (0, jnp.minimum(j, 5))),
                  pl.BlockSpec((n, GDN_DK), const), pl.BlockSpec((n, GDN_DK), const),
                  pl.BlockSpec((1, gl), const), pl.BlockSpec((1, gl), const)],
        out_specs=(pl.BlockSpec((1, n, tc), lambda b, j: (b, 0, jnp.minimum(j, nq - 1))),
                   pl.BlockSpec((1, n, gl), lambda b, j: (b, 0, 0))),
        scratch_shapes=[pltpu.VMEM((n, d), BF16)],
        compiler_params=_cparams(("arbitrary", "arbitrary")),
        name="gdn_proj",
    )(x3, mod3, w_main, w_ab, conv_w, cosf, sinf, a_log_l, dt_l)


def _gdn_scan_kernel(ql, kl, vl, gl, qc, kc, vc, gx, ol, oc, s_ref, u_s, wq_s, at_s, kd_s, la_s, *, hb):
    cs = GDN_CHUNK
    dk = GDN_DK
    ns = 2 * hb
    n_lat = ql.shape[1]
    n_ctx = qc.shape[1]
    s_ref[...] = jnp.zeros_like(s_ref)
    ol[...] = jnp.zeros_like(ol)
    oc[...] = jnp.zeros_like(oc)
    c2 = 2 * cs
    ii = lax.broadcasted_iota(jnp.int32, (c2, c2), 0)
    jj = lax.broadcasted_iota(jnp.int32, (c2, c2), 1)
    same = (ii // cs) == (jj // cs)
    eye = (ii == jj)[None]
    lower = (same & (ii >= jj))[None]
    upper = (same & (ii <= jj))[None]
    eye_f = eye.astype(F32)
    first_chunk = lax.broadcasted_iota(jnp.int32, (1, c2, 1), 1) < cs

    def run_window(qr, kr, vr, gr, orf, t0f, t0b, nb):
        npair = nb // 2
        bp = ns * npair
        bi = lax.broadcasted_iota(jnp.int32, (bp, c2, c2), 0)
        i3 = lax.broadcasted_iota(jnp.int32, (bp, c2, c2), 1)
        j3 = lax.broadcasted_iota(jnp.int32, (bp, c2, c2), 2)
        dij = (1 - 2 * ((bi // npair) % 2)) * (i3 - j3)
        same3 = (i3 // cs) == (j3 // cs)
        tri = same3 & (dij >= 0)
        stri = same3 & (dij > 0)

        def gates(t0, d):
            slab = gr[0, pl.ds(t0, nb * cs), :].reshape(npair, c2, LANES)
            g1 = slab.astype(BF16)
            r1 = slab - g1.astype(F32)
            g2 = r1.astype(BF16)
            g3 = (r1 - g2.astype(F32)).astype(BF16)
            t = jnp.broadcast_to((lower if d == 0 else upper).astype(BF16), (npair, c2, c2))
            gcum = _bdot(t, g1) + (_bdot(t, g2) + _bdot(t, g3))
            tot0 = jnp.sum(slab[:, :cs], axis=1, keepdims=True)
            tot1 = jnp.sum(slab[:, cs:], axis=1, keepdims=True)
            return slab, gcum, jnp.where(first_chunk, tot0, tot1)

        gate_d = (gates(t0f, 0), gates(t0b, 1))
        gc_l, be_l, gt_l, q_l, k_l, v_l = [], [], [], [], [], []
        for h in range(hb):
            for d in range(2):
                slab, gcum, gtot = gate_d[d]
                lg = d * hb + h
                lb = 2 * hb + lg
                gc_l.append(gcum[:, :, lg:lg + 1])
                be_l.append(slab[:, :, lb:lb + 1])
                gt_l.append(gtot[:, :, lg:lg + 1])
                t0 = t0f if d == 0 else t0b
                cols = slice(h * dk, (h + 1) * dk)
                q_l.append(qr[0, pl.ds(t0, nb * cs), cols].reshape(npair, c2, dk))
                k_l.append(kr[0, pl.ds(t0, nb * cs), cols].reshape(npair, c2, dk))
                v_l.append(vr[0, pl.ds(t0, nb * cs), cols].reshape(npair, c2, dk))
        gc = jnp.concatenate(gc_l, axis=0)
        be = jnp.concatenate(be_l, axis=0)
        gt = jnp.concatenate(gt_l, axis=0)
        q = jnp.concatenate(q_l, axis=0)
        k = jnp.concatenate(k_l, axis=0)
        v = jnp.concatenate(v_l, axis=0)

        gcl = gc + jnp.zeros((1, 1, c2), F32)
        gcr = jnp.sum(jnp.where(eye, gcl, 0.0), axis=1, keepdims=True)
        decay = jnp.where(tri, jnp.exp(jnp.where(tri, gcl - gcr, 0.0)), 0.0)
        kq = _bdot_nt(jnp.concatenate([k, q], axis=1), k)
        a = jnp.where(stri, kq[:, :c2] * be * decay, 0.0)
        mm = lambda l, r: _bdot(l.astype(BF16), r.astype(BF16))
        x = mm(a, a)
        p = eye_f - a
        for _ in range(4):
            px = mm(jnp.concatenate([p, x], axis=1), x)
            p = p + px[:, :c2]
            x = px[:, c2:]
        p = p + mm(p, x)
        pb = p.astype(BF16)
        eg = jnp.exp(gc)
        kf = k.astype(F32)
        uw = _bdot(pb, jnp.concatenate([(v.astype(F32) * be).astype(BF16), (kf * (be * eg)).astype(BF16)], axis=2))
        u_s[:, :nb] = uw[:, :, :dk].reshape(ns, nb, cs, dk)
        w4 = uw[:, :, dk:].astype(BF16).reshape(ns, nb, cs, dk)
        qd4 = (q.astype(F32) * eg).astype(BF16).reshape(ns, nb, cs, dk)
        wq_s[:, :nb] = jnp.concatenate([w4, qd4], axis=2)
        attn = jnp.where(tri, kq[:, c2:] * decay, 0.0).astype(BF16)
        at_s[:, :nb] = jnp.concatenate([attn[:, None, :cs, :cs], attn[:, None, cs:, cs:]],
                                       axis=1).reshape(ns, nb, cs, cs)
        kd_s[:, :nb] = (kf * jnp.exp(gt - gc)).astype(BF16).reshape(ns, nb, cs, dk)
        la = jnp.exp(gt) + jnp.zeros((1, 1, dk), F32)
        la_s[:, :nb] = jnp.concatenate([la[:, None, 0:1], la[:, None, cs:cs + 1]], axis=1).reshape(ns, nb, 1, dk)

        for j in range(nb):
            pick = lambda ref: jnp.stack([ref[sidx, j if sidx % 2 == 0 else nb - 1 - j] for sidx in range(ns)], axis=0)
            s = s_ref[...]
            sbf = s.astype(BF16)
            ws_qs = _bdot(pick(wq_s), sbf)
            v_new = pick(u_s) - ws_qs[:, :cs]
            vb = v_new.astype(BF16)
            o = ws_qs[:, cs:] + _bdot(pick(at_s), vb)
            s_ref[...] = s * pick(la_s) + _bdot_tn(pick(kd_s), vb)
            for h in range(hb):
                cols = slice(h * dk, (h + 1) * dk)
                orf[0, pl.ds(t0f + j * cs, cs), cols] += o[2 * h]
                orf[0, pl.ds(t0b + (nb - 1 - j) * cs, cs), cols] += o[2 * h + 1]

    run_window(qc, kc, vc, gx, oc, 0, 0, n_ctx // cs)
    wrows = GDN_NB * cs
    nwin = n_lat // wrows

    def body(wi, carry):
        t0f = pl.multiple_of(wi * wrows, wrows)
        t0b = pl.multiple_of((nwin - 1 - wi) * wrows, wrows)
        run_window(ql, kl, vl, gl, ol, t0f, t0b, GDN_NB)
        return carry

    lax.fori_loop(0, nwin, body, 0)


def _gdn_scan(pl_, gl_, pc_, gc_):
    bn, n_lat, _ = pl_.shape
    n_ctx = pc_.shape[1]
    hb, dk, cs = GDN_HB, GDN_DK, GDN_CHUNK
    hd = GDN_HEADS * dk
    ngrp = GDN_HEADS // hb
    wcol = hb * dk
    nbm = max(GDN_NB, n_ctx // cs)

    def sec(n, k):
        return pl.BlockSpec((1, n, wcol), lambda b, g: (b, 0, k * ngrp + g))

    gate = lambda n: pl.BlockSpec((1, n, LANES), lambda b, g: (b, 0, g))
    out = lambda n: pl.BlockSpec((1, n, wcol), lambda b, g: (b, 0, g))
    return pl.pallas_call(
        functools.partial(_gdn_scan_kernel, hb=hb),
        out_shape=(jax.ShapeDtypeStruct((bn, n_lat, hd), F32), jax.ShapeDtypeStruct((bn, n_ctx, hd), F32)),
        grid=(bn, ngrp),
        in_specs=[sec(n_lat, 0), sec(n_lat, 1), sec(n_lat, 2), gate(n_lat),
                  sec(n_ctx, 0), sec(n_ctx, 1), sec(n_ctx, 2), gate(n_ctx)],
        out_specs=(out(n_lat), out(n_ctx)),
        scratch_shapes=[pltpu.VMEM((2 * hb, dk, dk), F32),
                        pltpu.VMEM((2 * hb, nbm, cs, dk), F32),
                        pltpu.VMEM((2 * hb, nbm, 2 * cs, dk), BF16),
                        pltpu.VMEM((2 * hb, nbm, cs, cs), BF16),
                        pltpu.VMEM((2 * hb, nbm, cs, dk), BF16),
                        pltpu.VMEM((2 * hb, nbm, 1, dk), F32)],
        compiler_params=_cparams(("arbitrary", "arbitrary")),
        name="gdn_scan",
    )(pl_, pl_, pl_, gl_, pc_, pc_, pc_, gc_)


def _gdn_out_kernel(o_ref, z_ref, ng_ref, w_ref, x_ref, g_ref, y_ref):
    dk = GDN_DK
    parts = []
    for h in range(o_ref.shape[1] // dk):
        cols = slice(h * dk, (h + 1) * dk)
        oh = o_ref[:, cols]
        ms = jnp.mean(oh * oh, axis=-1, keepdims=True)
        on = (oh * lax.rsqrt(ms + NORM_EPS)) * ng_ref[...]
        parts.append((on * _silu(z_ref[:, cols].astype(F32))).astype(BF16))
    a = jnp.concatenate(parts, axis=1)
    y_ref[...] = x_ref[...] + g_ref[0] * _dot_rows(a, w_ref[...])


def _gdn_out(o, proj, norm_g, w_out, x, gate, rows_per_seg, tl=512):
    r, hd = o.shape
    d = w_out.shape[1]
    tps = rows_per_seg // tl
    return pl.pallas_call(
        _gdn_out_kernel,
        out_shape=jax.ShapeDtypeStruct((r, d), F32),
        grid=(r // tl,),
        in_specs=[pl.BlockSpec((tl, hd), lambda i: (i, 0)),
                  pl.BlockSpec((tl, hd), lambda i: (i, 3)),
                  pl.BlockSpec((1, GDN_DK), lambda i: (0, 0)),
                  pl.BlockSpec((hd, d), lambda i: (0, 0)),
                  pl.BlockSpec((tl, d), lambda i: (i, 0)),
                  pl.BlockSpec((1, 1, d), lambda i: (i // tps, 0, 0))],
        out_specs=pl.BlockSpec((tl, d), lambda i: (i, 0)),
        compiler_params=_cparams(("arbitrary",)),
        name="gdn_out",
    )(o, proj, norm_g, w_out, x, gate)


def _proj3_kernel(x_ref, m_ref, w0_ref, w1_ref, w2_ref, cw_ref, cb_ref, *rest, hyena):
    j = pl.program_id(1)
    o_refs, h_s = rest[:-1], rest[-1]
    n = x_ref.shape[1]

    @pl.when(j == 0)
    def _():
        h_s[...] = _rms_mod(x_ref[0], m_ref[0]).astype(BF16)

    row = lax.broadcasted_iota(jnp.int32, (n, 1), 0)
    z0 = _dot_rows(h_s, w0_ref[...])
    z1 = _dot_rows(h_s, w1_ref[...])
    z2 = _dot_rows(h_s, w2_ref[...])
    if hyena:
        cb = cb_ref[...]
        x0 = _seq_conv3(z0, cw_ref[0], row) + cb[0:1]
        x1 = _seq_conv3(z1, cw_ref[1], row) + cb[1:2]
        v = _seq_conv3(z2, cw_ref[2], row) + cb[2:3]
        o_refs[0][0] = (v * x1).astype(BF16)
        o_refs[1][0] = x0.astype(BF16)
    else:
        o_refs[0][0] = (z1 * _seq_conv3(z2 * z0, cw_ref[0], row)).astype(BF16)


def _proj3(x3, mod3, w_in, cw, cb, hyena, tc=256):
    bn, n, d = x3.shape
    segs = mod3.shape[0]
    nj = d // tc
    mi = (lambda b, j: (b, 0, 0)) if segs > 1 else (lambda b, j: (0, 0, 0))
    wspec = lambda k: pl.BlockSpec((d, tc), lambda b, j: (0, k * nj + j))
    seq = pl.BlockSpec((1, n, tc), lambda b, j: (b, 0, j))
    n_out = 2 if hyena else 1
    outs = pl.pallas_call(
        functools.partial(_proj3_kernel, hyena=hyena),
        out_shape=tuple(jax.ShapeDtypeStruct((bn, n, d), BF16) for _ in range(n_out)),
        grid=(bn, nj),
        in_specs=[pl.BlockSpec((1, n, d), lambda b, j: (b, 0, 0)),
                  pl.BlockSpec((1, 3, d), mi),
                  wspec(0), wspec(1), wspec(2),
                  pl.BlockSpec((cw.shape[0], 3, tc), lambda b, j: (0, 0, j)),
                  pl.BlockSpec((3, tc), lambda b, j: (0, j))],
        out_specs=tuple(seq for _ in range(n_out)),
        scratch_shapes=[pltpu.VMEM((n, d), BF16)],
        compiler_params=_cparams(("arbitrary", "arbitrary")),
        name="hyena_proj" if hyena else "shortconv_proj",
    )(x3, mod3, w_in, w_in, w_in, cw, cb)
    return outs


def _final_norm_kernel(x_ref, g_ref, o_ref):
    x = x_ref[...]
    ms = jnp.mean(x * x, axis=-1, keepdims=True)
    o_ref[...] = (x * lax.rsqrt(ms + NORM_EPS)) * g_ref[...]


def _final_norm(x, g, tl=1024):
    r, d = x.shape
    return pl.pallas_call(
        _final_norm_kernel,
        out_shape=jax.ShapeDtypeStruct((r, d), F32),
        grid=(r // tl,),
        in_specs=[pl.BlockSpec((tl, d), lambda i: (i, 0)), pl.BlockSpec((1, d), lambda i: (0, 0))],
        out_specs=pl.BlockSpec((tl, d), lambda i: (i, 0)),
        compiler_params=_cparams(("arbitrary",)),
        name="final_norm",
    )(x, g)


def _hyena_kernel(u_ref, x0_ref, c_ref, s_ref, kr_ref, ki_ref, kn_ref, fb_ref, o_ref):
    ub = u_ref[0]
    l = ub.shape[0]
    mb = min(l, MM_ROWS)

    def rowblocks(fn):
        return jnp.concatenate([fn(slice(i, i + mb)) for i in range(0, l, mb)], axis=0)

    a = rowblocks(lambda r: _dot(c_ref[r, :], ub))
    b = rowblocks(lambda r: _dot(s_ref[r, :], ub))
    kr = kr_ref[...]
    ki = ki_ref[...]
    zr = (a * kr + b * ki).astype(BF16)
    zi = (b * kr - a * ki).astype(BF16)
    y = rowblocks(lambda r: _dot(c_ref[r, :], zr) + _dot(s_ref[r, :], zi))
    uf = ub.astype(F32)
    tpar = lax.broadcasted_iota(jnp.int32, (l, 1), 0) & 1
    alt = (1 - 2 * tpar).astype(F32)
    un = jnp.sum(uf * alt, axis=0, keepdims=True)
    y = y + alt * (un * kn_ref[...]) + uf * fb_ref[...]
    o_ref[0] = (y * x0_ref[0].astype(F32)).astype(o_ref.dtype)


def _hyena_conv(u, x0, cmat, smat, kr, ki, kn, fbias, td=256):
    b, l, d = u.shape
    seq = pl.BlockSpec((1, l, td), lambda i, j: (i, 0, j))
    mat = pl.BlockSpec((l, l), lambda i, j: (0, 0), pipeline_mode=pl.Buffered(1))
    spec = pl.BlockSpec((l, td), lambda i, j: (0, j))
    vec = pl.BlockSpec((1, td), lambda i, j: (0, j))
    return pl.pallas_call(
        _hyena_kernel,
        out_shape=jax.ShapeDtypeStruct((b, l, d), BF16),
        grid=(b, d // td),
        in_specs=[seq, seq, mat, mat, spec, spec, vec, vec],
        out_specs=seq,
        compiler_params=_cparams(("arbitrary", "arbitrary")),
        name="hyena_conv",
    )(u, x0, cmat, smat, kr, ki, kn, fbias)


def _mm3_kernel(a_ref, b_ref, o_ref):
    ah, al = _split_bf16(a_ref[...])
    bh, bl = _split_bf16(b_ref[...])
    o_ref[...] = _dot(ah, bh) + (_dot(ah, bl) + _dot(al, bh))


def _mm3(a, b, tm=256, tn=256):
    m, k = a.shape
    n = b.shape[1]
    tm = min(tm, m)
    return pl.pallas_call(
        _mm3_kernel,
        out_shape=jax.ShapeDtypeStruct((m, n), F32),
        grid=(n // tn, m // tm),
        in_specs=[pl.BlockSpec((tm, k), lambda j, i: (i, 0)),
                  pl.BlockSpec((k, tn), lambda j, i: (0, j))],
        out_specs=pl.BlockSpec((tm, tn), lambda j, i: (i, j)),
        compiler_params=_cparams(("arbitrary", "arbitrary")),
        name="matmul_f32x3",
    )(a, b)


def _na_kernel(q_ref, k_ref, v_ref, kc_ref, vc_ref, bias_ref, o_ref, *, rows):
    r = pl.program_id(1)
    rs = jnp.clip(r - NA_WIN_R // 2, 0, rows - NA_WIN_R)
    t0 = pl.multiple_of(rs * GRID_W, GRID_W)
    nk = NA_WIN_R * GRID_W
    scale = NA_DH ** -0.5
    lane = lax.broadcasted_iota(jnp.int32, (GRID_W, LANES), 1)
    first = lane < NA_DH
    nq = GRID_W
    scores = []
    for hp in range(NA_HEADS // 2):
        cols = slice(hp * LANES, (hp + 1) * LANES)
        q2 = q_ref[0, :, cols] * scale
        zero = jnp.zeros_like(q2)
        qm = jnp.concatenate([jnp.where(first, q2, zero), jnp.where(first, zero, q2)], axis=0)
        bias = bias_ref[0, 2 * hp:2 * hp + 2].reshape(2 * nq, nk)
        s_loc = _dot_nt(qm, k_ref[0, pl.ds(t0, nk), cols]) + bias
        s_ctx = _dot_nt(qm, kc_ref[0, :, cols])
        scores.append((s_loc, s_ctx))
    probs = []
    for s_loc, s_ctx in scores:
        m = jnp.maximum(jnp.max(s_loc, axis=1, keepdims=True), jnp.max(s_ctx, axis=1, keepdims=True))
        p_loc = jnp.exp(s_loc - m)
        p_ctx = jnp.exp(s_ctx - m)
        den = jnp.sum(p_loc, axis=1, keepdims=True) + jnp.sum(p_ctx, axis=1, keepdims=True)
        probs.append((p_loc.astype(BF16), p_ctx.astype(BF16), den))
    pairs = []
    for hp, (p_loc, p_ctx, den) in enumerate(probs):
        cols = slice(hp * LANES, (hp + 1) * LANES)
        o = (_dot(p_loc, v_ref[0, pl.ds(t0, nk), cols]) + _dot(p_ctx, vc_ref[0, :, cols])) / den
        pairs.append(jnp.where(first, o[:nq], o[nq:]).astype(o_ref.dtype))
    o_ref[0] = jnp.concatenate(pairs, axis=1)


def _na_attention(q, k, v, kc, vc, bias_tab):
    b, l, hd = q.shape
    lc = kc.shape[1]
    rows = l // GRID_W
    half = NA_WIN_R // 2
    return pl.pallas_call(
        functools.partial(_na_kernel, rows=rows),
        out_shape=jax.ShapeDtypeStruct((b, l, hd), BF16),
        grid=(b, rows),
        in_specs=[pl.BlockSpec((1, GRID_W, hd), lambda i, r: (i, r, 0)),
                  pl.BlockSpec((1, l, hd), lambda i, r: (i, 0, 0)),
                  pl.BlockSpec((1, l, hd), lambda i, r: (i, 0, 0)),
                  pl.BlockSpec((1, lc, hd), lambda i, r: (i, 0, 0)),
                  pl.BlockSpec((1, lc, hd), lambda i, r: (i, 0, 0)),
                  pl.BlockSpec((1, NA_HEADS, GRID_W, NA_WIN_R * GRID_W),
                               lambda i, r: (r - jnp.clip(r - half, 0, rows - NA_WIN_R), 0, 0, 0))],
        out_specs=pl.BlockSpec((1, GRID_W, hd), lambda i, r: (i, r, 0)),
        compiler_params=_cparams(("arbitrary", "arbitrary")),
        name="na_attention",
    )(q, k, v, kc, vc, bias_tab)


def _na_bias_table(rpb):
    qc = np.arange(GRID_W)
    kc = np.arange(GRID_W)
    cstart = np.clip(qc - NA_WIN_C // 2, 0, GRID_W - NA_WIN_C)
    valid = (kc[None, :] >= cstart[:, None]) & (kc[None, :] < cstart[:, None] + NA_WIN_C)
    dc = np.clip(kc[None, :] - qc[:, None] + NA_WIN_C - 1, 0, 2 * NA_WIN_C - 2)
    off = np.arange(NA_WIN_R)
    j = np.arange(NA_WIN_R)
    dr = j[None, :] - off[:, None] + NA_WIN_R - 1
    sel_r = (dr[:, :, None] == np.arange(2 * NA_WIN_R - 1)).astype(np.float32)
    sel_c = (dc[:, :, None] == np.arange(2 * NA_WIN_C - 1)).astype(np.float32)
    t = jnp.einsum("ojr,hrc,qkc->ohqjk", sel_r, rpb.astype(F32), sel_c, precision=lax.Precision.HIGHEST)
    t = jnp.where(jnp.asarray(valid)[None, None, :, None, :], t, jnp.float32(-1e30))
    return t.reshape(NA_WIN_R, NA_HEADS, GRID_W, NA_WIN_R * GRID_W)


def _rope_tables(n_tok, dh):
    pos = jnp.arange(n_tok)
    row = (pos // GRID_W).astype(F32)
    col = (pos % GRID_W).astype(F32)
    n_freq = dh // 4
    inv = ROPE_BASE ** (-jnp.arange(n_freq, dtype=F32) / n_freq)
    ang = jnp.concatenate([row[:, None] * inv, col[:, None] * inv], axis=-1)
    cos, sin = jnp.cos(ang), jnp.sin(ang)
    return jnp.concatenate([cos, cos], axis=-1), jnp.concatenate([-sin, sin], axis=-1)


def _gdn_gate_lanes():
    ngrp = GDN_HEADS // GDN_HB
    src = -np.ones((ngrp * LANES,), np.int64)
    for grp in range(ngrp):
        for t in range(2):
            for dr in range(2):
                for hh in range(GDN_HB):
                    lane = grp * LANES + t * 2 * GDN_HB + dr * GDN_HB + hh
                    src[lane] = t * 2 * GDN_HEADS + dr * GDN_HEADS + grp * GDN_HB + hh
    return src


_GDN_LANE_SRC = _gdn_gate_lanes()


def _gdn_layer(x, xc, mod_l, mod_c, gate_l, gate_c, bn, w_in, conv_w, a_log, dt_bias, norm_g, w_out, ctx_out):
    d = D_MODEL
    hd = GDN_HEADS * GDN_DK
    n_lat = x.shape[0] // bn
    n_ctx = xc.shape[0] // bn
    w_main = w_in[:, :4 * hd].astype(BF16)
    used = jnp.asarray(_GDN_LANE_SRC >= 0)
    lane_src = jnp.asarray(np.maximum(_GDN_LANE_SRC, 0))
    w_ab = jnp.where(used[None, :], w_in[:, 4 * hd:][:, lane_src], 0.0).astype(BF16)
    decay_lane = used & (lane_src < 2 * GDN_HEADS)
    a_log_l = jnp.where(decay_lane, a_log.reshape(-1)[lane_src % (2 * GDN_HEADS)], 0.0)[None, :].astype(F32)
    dt_l = jnp.where(decay_lane, dt_bias.reshape(-1)[lane_src % (2 * GDN_HEADS)], 0.0)[None, :].astype(F32)
    cosf, sinf = _rope_tables(n_lat, GDN_DK)
    dummy = jnp.zeros((n_ctx, GDN_DK), F32)
    p_l, g_l = _gdn_proj(x.reshape(bn, n_lat, d), mod_l, w_main, w_ab, conv_w, cosf, sinf, a_log_l, dt_l, True)
    p_c, g_c = _gdn_proj(xc.reshape(bn, n_ctx, d), mod_c, w_main, w_ab, conv_w, dummy, dummy, a_log_l, dt_l, False)
    o_l, o_c = _gdn_scan(p_l, g_l, p_c, g_c)
    wo = w_out.astype(BF16)
    ng = norm_g.astype(F32)[None, :]
    x_new = _gdn_out(o_l.reshape(bn * n_lat, hd), p_l.reshape(bn * n_lat, 4 * hd), ng, wo, x, gate_l, n_lat)
    xc_new = None
    if ctx_out:
        xc_new = _gdn_out(o_c.reshape(bn * n_ctx, hd), p_c.reshape(bn * n_ctx, 4 * hd), ng, wo, xc, gate_c,
                          xc.shape[0])
    return x_new, xc_new


def _hyena_filter_kernel(z_ref, w1_ref, b1_ref, w2_ref, b2_ref, w3_ref, b3_ref, fq_ref, w4_ref, win_ref, o_ref):
    def dot3(a, b):
        ah, al = _split_bf16(a)
        bh, bl = _split_bf16(b)
        return _dot(ah, bh) + (_dot(ah, bl) + _dot(al, bh))

    fq = fq_ref[...]
    hdn = jnp.sin(fq[0:1] * (dot3(z_ref[...], w1_ref[...]) + b1_ref[...]))
    hdn = jnp.sin(fq[1:2] * (dot3(hdn, w2_ref[...]) + b2_ref[...]))
    hdn = jnp.sin(fq[2:3] * (dot3(hdn, w3_ref[...]) + b3_ref[...]))
    o_ref[...] = dot3(hdn, w4_ref[...]) * win_ref[...]


def _hyena_filter_taps(n_tok, w1, b1, w2, b2, w3, b3, freq, w4):
    t = jnp.linspace(0.0, 1.0, n_tok, dtype=F32)[:, None]
    bands = (HY_EMB_DIM - 1) // 2
    wpos = 2.0 * math.pi * jnp.arange(n_tok, dtype=F32)[:, None] / n_tok
    fr = jnp.linspace(1e-4, bands - 1, bands, dtype=F32)[None, :]
    z = jnp.concatenate([t, jnp.cos(fr * wpos), -jnp.sin(fr * wpos)], axis=-1)
    z = jnp.pad(z, ((0, 0), (0, LANES - HY_EMB_DIM)))
    fw = w1.shape[1]
    w1p = jnp.pad(w1.astype(F32), ((0, LANES - HY_EMB_DIM), (0, 0)))
    deltas = jnp.abs(jnp.linspace(HY_MIN_DECAY, HY_MAX_DECAY, D_MODEL, dtype=F32))
    window = jnp.tile(jnp.exp(-t * deltas), (1, 2))
    nout = w4.shape[1]
    tn = 512
    const = lambda j: (0, 0)
    row = lambda a: a.astype(F32)[None, :]
    filt = pl.pallas_call(
        _hyena_filter_kernel,
        out_shape=jax.ShapeDtypeStruct((n_tok, nout), F32),
        grid=(nout // tn,),
        in_specs=[pl.BlockSpec((n_tok, LANES), const),
                  pl.BlockSpec((LANES, fw), const), pl.BlockSpec((1, fw), const),
                  pl.BlockSpec((fw, fw), const), pl.BlockSpec((1, fw), const),
                  pl.BlockSpec((fw, fw), const), pl.BlockSpec((1, fw), const),
                  pl.BlockSpec((3, fw), const),
                  pl.BlockSpec((fw, tn), lambda j: (0, j)),
                  pl.BlockSpec((n_tok, tn), lambda j: (0, j))],
        out_specs=pl.BlockSpec((n_tok, tn), lambda j: (0, j)),
        compiler_params=_cparams(("arbitrary",)),
        name="hyena_filter",
    )(z, w1p, row(b1), w2.astype(F32), row(b2), w3.astype(F32), row(b3), freq.astype(F32), w4.astype(F32), window)
    return filt[:, :D_MODEL], filt[:, D_MODEL:]


def _dft_mats(l):
    k = jnp.arange(l, dtype=jnp.int32)
    ks = (k[:, None] * k[None, :]) % (2 * l)
    ang = ks.astype(F32) * (math.pi / l)
    return jnp.cos(ang), jnp.sin(ang)


def _hyena_layer(rows, mod3, gate, rows_per_seg, bn, w_in, short_w, short_b, f_w1, f_b1, f_w2, f_b2, f_w3, f_b3,
                 f_freq, f_w4, f_bias, w_out):
    d = D_MODEL
    n_tok = rows.shape[0] // bn
    cw = short_w.reshape(3, 3, d).transpose(1, 0, 2).astype(F32)
    u, x0 = _proj3(rows.reshape(bn, n_tok, d), mod3, w_in.astype(BF16), cw, short_b.reshape(3, d).astype(F32),
                   hyena=True)
    hf, hb = _hyena_filter_taps(n_tok, f_w1, f_b1, f_w2, f_b2, f_w3, f_b3, f_freq, f_w4)
    hb = hb.at[0].set(0.0)
    cmat, smat = _dft_mats(n_tok)
    n2 = 2 * n_tok
    wk = jnp.full((n_tok, 1), 2.0 / n2, F32).at[0, 0].set(1.0 / n2)
    kr = _mm3(cmat, hf + hb) * wk
    ki = _mm3(smat, hb - hf) * wk
    alt = (1.0 - 2.0 * (jnp.arange(n_tok) % 2)).astype(F32)[:, None]
    kn = jnp.sum((hf + hb) * alt, axis=0, keepdims=True) / n2
    y = _hyena_conv(u, x0, cmat.astype(BF16), smat.astype(BF16), kr, ki, kn, f_bias.astype(F32)[None, :])
    return _mm_res(y.reshape(rows.shape[0], d), w_out.astype(BF16), rows, gate, rows_per_seg)


def _na_layer(x, xc, mod_l, mod_c, gate_l, bn, w_qkv, rpb, w_out):
    hd = NA_HEADS * NA_DH
    n_lat = x.shape[0] // bn
    n_ctx = xc.shape[0] // bn
    wq = w_qkv.astype(BF16)
    z = _modmm(x, mod_l, wq, n_lat, out_dtype=BF16).reshape(bn, n_lat, 3 * hd)
    zc = _modmm(xc, mod_c, wq, xc.shape[0], out_dtype=BF16).reshape(bn, n_ctx, 3 * hd)
    o = _na_attention(z[..., :hd], z[..., hd:2 * hd], z[..., 2 * hd:], zc[..., hd:2 * hd], zc[..., 2 * hd:],
                      _na_bias_table(rpb))
    return _mm_res(o.reshape(x.shape[0], hd), w_out.astype(BF16), x, gate_l, n_lat)


def _shortconv_layer(rows, mod3, gate, rows_per_seg, bn, w_in, conv_w, w_out):
    d = D_MODEL
    n_tok = rows.shape[0] // bn
    (a,) = _proj3(rows.reshape(bn, n_tok, d), mod3, w_in.astype(BF16), conv_w.astype(F32)[None],
                  jnp.zeros((3, d), F32), hyena=False)
    return _mm_res(a.reshape(rows.shape[0], d), w_out.astype(BF16), rows, gate, rows_per_seg)


def kernel(x, c, ctx, c_ctx, ln_g, w_mod, b_mod, final_g, gdn_w_in, gdn_conv_w, gdn_a_log, gdn_dt_bias, gdn_norm_g, gdn_w_out, hy_w_in, hy_short_w, hy_short_b, hy_f_w1, hy_f_b1, hy_f_w2, hy_f_b2, hy_f_w3, hy_f_b3, hy_f_freq, hy_f_w4, hy_f_bias, hy_w_out, na_w_qkv, na_rpb, na_w_out, sc_w_in, sc_conv_w, sc_w_out, moe_w_gr, moe_b_gr, moe_w_er, moe_b_er, moe_w_gate, moe_w_up, moe_w_down):
    bn, n_lat, d = x.shape
    n_ctx = ctx.shape[1]
    rows_ctx = bn * n_ctx
    pad = (-(bn + 1)) % 8
    c_all = jnp.concatenate([c, c_ctx[None, :], jnp.zeros((pad, d), F32)], axis=0)
    mods = _mod_vectors(c_all, w_mod, b_mod)
    xr = x.reshape(bn * n_lat, d)
    xcr = ctx.reshape(rows_ctx, d)
    for i in range(DEPTH):
        m, s = i % N_MIXERS, i // N_MIXERS
        reads_ctx = m in CTX_READING_MIXERS
        upd_ctx = any((j % N_MIXERS) in CTX_READING_MIXERS for j in range(i + 1, DEPTH))
        mv = mods[i].reshape(-1, 6, d)

        def mod3(rows, which, norm_g):
            sh, sc = mv[rows, 3 * which], mv[rows, 3 * which + 1]
            return jnp.stack([jnp.broadcast_to(norm_g, sh.shape), sh, sc], axis=1)

        lat = slice(0, bn)
        cx = slice(bn, bn + 1)
        mod_l1, mod_l2 = mod3(lat, 0, ln_g[i, 0]), mod3(lat, 1, ln_g[i, 1])
        mod_c1, mod_c2 = mod3(cx, 0, ln_g[i, 0]), mod3(cx, 1, ln_g[i, 1])
        g1_l, g2_l = mv[lat, 2][:, None, :], mv[lat, 5][:, None, :]
        g1_c, g2_c = mv[cx, 2][:, None, :], mv[cx, 5][:, None, :]
        xc_new = None
        if m == 0:
            xr, xc_new = _gdn_layer(xr, xcr, mod_l1, mod_c1, g1_l, g1_c, bn, gdn_w_in[s], gdn_conv_w[s],
                                    gdn_a_log[s], gdn_dt_bias[s], gdn_norm_g[s], gdn_w_out[s], upd_ctx)
        elif m == 1:
            hy = (hy_w_in[s], hy_short_w[s], hy_short_b[s], hy_f_w1[s], hy_f_b1[s], hy_f_w2[s], hy_f_b2[s],
                  hy_f_w3[s], hy_f_b3[s], hy_f_freq[s], hy_f_w4[s], hy_f_bias[s], hy_w_out[s])
            xr = _hyena_layer(xr, mod_l1, g1_l, n_lat, bn, *hy)
            if upd_ctx:
                xc_new = _hyena_layer(xcr, mod_c1, g1_c, rows_ctx, bn, *hy)
        elif m == 2:
            xr = _na_layer(xr, xcr, mod_l1, mod_c1, g1_l, bn, na_w_qkv[s], na_rpb[s], na_w_out[s])
            assert not upd_ctx
        else:
            xr = _shortconv_layer(xr, mod_l1, g1_l, n_lat, bn, sc_w_in[s], sc_conv_w[s], sc_w_out[s])
            if upd_ctx:
                xc_new = _shortconv_layer(xcr, mod_c1, g1_c, rows_ctx, bn, sc_w_in[s], sc_conv_w[s], sc_w_out[s])
        moe = (moe_w_gr[i], moe_b_gr[i], moe_w_er[i], moe_b_er[i], moe_w_gate[i], moe_w_up[i], moe_w_down[i])
        if upd_ctx:
            xr, xcr = _hier_moe_residual(xr, mod_l2, g2_l, n_lat, *moe, xc=xc_new, mod3c=mod_c2, gatec=g2_c)
        else:
            xr = _hier_moe_residual(xr, mod_l2, g2_l, n_lat, *moe)
    return _final_norm(xr, final_g.astype(F32)[None, :]).reshape(bn, n_lat, d)
```

```python
import functools
import math

import numpy as np
import jax
import jax.numpy as jnp
from jax import lax
from jax.experimental import pallas as pl
from jax.experimental.pallas import tpu as pltpu

F32 = jnp.float32
BF16 = jnp.bfloat16

D_MODEL = 1024
DEPTH = 4
GRID_W = 64
N_MIXERS = 4
CTX_READING_MIXERS = (0, 2)
NORM_EPS = 1e-6

GDN_HEADS = 8
GDN_DK = 128
GDN_CHUNK = 64
ROPE_BASE = 10000.0

HY_EMB_DIM = 33
HY_DECAY_TARGET = 1e-2
HY_MAX_DECAY = math.log(HY_DECAY_TARGET) / 0.3
HY_MIN_DECAY = math.log(HY_DECAY_TARGET) / 1.5

NA_HEADS = 16
NA_DH = 64
NA_WIN_R = 8
NA_WIN_C = 16

MOE_GROUPS = 4
MOE_EPG = 8
MOE_FF = 256
MOE_PAIRS = MOE_EPG * (MOE_EPG - 1) // 2
MOE_CLASSES = MOE_GROUPS * MOE_PAIRS
MOE_TM = 256

LANES = 128
VMEM_LIMIT = 56 << 20


def _cparams(sem, vmem=VMEM_LIMIT):
    return pltpu.CompilerParams(dimension_semantics=sem, vmem_limit_bytes=vmem)


def _rms_mod(x, m):
    ms = jnp.mean(x * x, axis=-1, keepdims=True)
    xn = (x * lax.rsqrt(ms + NORM_EPS)) * m[0:1]
    return xn * (1.0 + m[2:3]) + m[1:2]


def _split_bf16(x):
    hi = x.astype(BF16)
    lo = (x - hi.astype(F32)).astype(BF16)
    return hi, lo


def _dot(a, b):
    return jnp.dot(a, b, preferred_element_type=F32)


MM_ROWS = 256


def _dot_rows(a, b):
    m = a.shape[0]
    if m <= MM_ROWS:
        return _dot(a[...], b)
    return jnp.concatenate([_dot(a[i:i + MM_ROWS], b) for i in range(0, m, MM_ROWS)], axis=0)


def _dot_nt(a, b):
    return lax.dot_general(a, b, (((1,), (1,)), ((), ())), preferred_element_type=F32)


def _bdot(a, b):
    return lax.dot_general(a, b, (((2,), (1,)), ((0,), (0,))), preferred_element_type=F32)


def _bdot_nt(a, b):
    return lax.dot_general(a, b, (((2,), (2,)), ((0,), (0,))), preferred_element_type=F32)


def _bdot_tn(a, b):
    return lax.dot_general(a, b, (((1,), (1,)), ((0,), (0,))), preferred_element_type=F32)


def _bdot3(a, b):
    ah, al = _split_bf16(a)
    bh, bl = _split_bf16(b)
    return _bdot(ah, bh) + (_bdot(ah, bl) + _bdot(al, bh))


def _silu(x):
    return x * (1.0 / (1.0 + jnp.exp(-x)))


def _modvec_kernel(c_ref, w_ref, b_ref, o_ref):
    c = c_ref[...]
    a = _silu(c)
    ah, al = _split_bf16(a)
    wh, wl = _split_bf16(w_ref[0])
    o_ref[0] = _dot(ah, wh) + (_dot(ah, wl) + _dot(al, wh)) + b_ref[0]


def _mod_vectors(c_all, w_mod, b_mod):
    rows, d = c_all.shape
    depth, _, n = w_mod.shape
    tn = 1536
    return pl.pallas_call(
        _modvec_kernel,
        out_shape=jax.ShapeDtypeStruct((depth, rows, n), F32),
        grid=(depth, n // tn),
        in_specs=[pl.BlockSpec((rows, d), lambda i, j: (0, 0)),
                  pl.BlockSpec((1, d, tn), lambda i, j: (i, 0, j)),
                  pl.BlockSpec((1, 1, tn), lambda i, j: (i, 0, j))],
        out_specs=pl.BlockSpec((1, rows, tn), lambda i, j: (i, 0, j)),
        compiler_params=_cparams(("arbitrary", "arbitrary")),
        name="mod_vectors",
    )(c_all, w_mod, b_mod.reshape(depth, 1, n))


def _modmm_kernel(x_ref, m_ref, w_ref, o_ref, *, ncol):
    h = _rms_mod(x_ref[...], m_ref[0]).astype(BF16)
    n = w_ref.shape[1]
    for n0 in range(0, n, ncol):
        o_ref[:, n0:n0 + ncol] = _dot_rows(h, w_ref[:, n0:n0 + ncol]).astype(o_ref.dtype)


def _modmm(x, mod3, w, rows_per_seg, out_dtype=F32, tl=512):
    r, d = x.shape
    n = w.shape[1]
    tps = rows_per_seg // tl
    ncol = 512 if n % 512 == 0 else n
    return pl.pallas_call(
        functools.partial(_modmm_kernel, ncol=ncol),
        out_shape=jax.ShapeDtypeStruct((r, n), out_dtype),
        grid=(r // tl,),
        in_specs=[pl.BlockSpec((tl, d), lambda i: (i, 0)),
                  pl.BlockSpec((1, 3, d), lambda i: (i // tps, 0, 0)),
                  pl.BlockSpec((d, n), lambda i: (0, 0))],
        out_specs=pl.BlockSpec((tl, n), lambda i: (i, 0)),
        compiler_params=_cparams(("arbitrary",)),
        name="mod_matmul",
    )(x, mod3, w)


def _mmres_kernel(a_ref, w_ref, x_ref, g_ref, o_ref):
    o_ref[...] = x_ref[...] + g_ref[0] * _dot_rows(a_ref, w_ref[...])


def _mm_res(a, w, x, gate, rows_per_seg, tl=512):
    r, k = a.shape
    d = w.shape[1]
    tps = rows_per_seg // tl
    return pl.pallas_call(
        _mmres_kernel,
        out_shape=jax.ShapeDtypeStruct((r, d), F32),
        grid=(r // tl,),
        in_specs=[pl.BlockSpec((tl, k), lambda i: (i, 0)),
                  pl.BlockSpec((k, d), lambda i: (0, 0)),
                  pl.BlockSpec((tl, d), lambda i: (i, 0)),
                  pl.BlockSpec((1, 1, d), lambda i: (i // tps, 0, 0))],
        out_specs=pl.BlockSpec((tl, d), lambda i: (i, 0)),
        compiler_params=_cparams(("arbitrary",)),
        name="matmul_residual",
    )(a, w, x, gate)


def _router_kernel(xa_ref, xb_ref, m_ref, wh_ref, wl_ref, b_ref, h_ref, route_ref, cnt_ref, carry_ref, *, na):
    i = pl.program_id(0)

    @pl.when(i == 0)
    def _():
        carry_ref[...] = jnp.zeros_like(carry_ref)

    h = _rms_mod(jnp.where(i < na, xa_ref[...], xb_ref[...]), m_ref[0])
    hh, hl = _split_bf16(h)
    h_ref[...] = hh
    wh = wh_ref[...]
    logits = _dot_rows(hh, wh) + (_dot_rows(hl, wh) + _dot_rows(hh, wl_ref[...])) + b_ref[...]
    t = logits.shape[0]
    lane = lax.broadcasted_iota(jnp.int32, logits.shape, 1).astype(F32)
    neg = -jnp.inf
    big = 1e9

    def first_argmax(vals):
        m = jnp.max(vals, axis=1, keepdims=True)
        idx = jnp.min(jnp.where(vals == m, lane, big), axis=1, keepdims=True)
        return m, idx

    lg = jnp.where(lane < MOE_GROUPS, logits, neg)
    mg, gsel = first_argmax(lg)
    wg = 1.0 / jnp.sum(jnp.exp(lg - mg), axis=1, keepdims=True)
    e_first = MOE_GROUPS + gsel * MOE_EPG
    in_grp = (lane >= e_first) & (lane < e_first + MOE_EPG)
    le = jnp.where(in_grp, logits, neg)
    m1, i1 = first_argmax(le)
    m2, i2 = first_argmax(jnp.where(lane == i1, neg, le))
    p = jnp.exp(m2 - m1)
    w1 = wg / (1.0 + p)
    w2 = wg * p / (1.0 + p)
    l1 = i1 - e_first
    l2 = i2 - e_first
    lo = jnp.minimum(l1, l2)
    hi = jnp.maximum(l1, l2)
    first_is_lo = l1 < l2
    wlo = jnp.where(first_is_lo, w1, w2)
    whi = jnp.where(first_is_lo, w2, w1)
    cls = gsel * MOE_PAIRS + lo * (2 * MOE_EPG - 1 - lo) * 0.5 + (hi - lo - 1.0)

    onehot = lane == cls
    ri = lax.broadcasted_iota(jnp.int32, (t, t), 0)
    ci = lax.broadcasted_iota(jnp.int32, (t, t), 1)
    before = (ci < ri).astype(BF16)
    prefix = _dot(before, onehot.astype(BF16)) + carry_ref[...]
    rank = jnp.sum(jnp.where(onehot, prefix, 0.0), axis=1, keepdims=True)
    carry = carry_ref[...] + jnp.sum(onehot.astype(F32), axis=0, keepdims=True)
    carry_ref[...] = carry
    cnt_ref[...] = carry
    route_ref[...] = jnp.where(lane == 0, cls, jnp.where(lane == 1, rank, jnp.where(
        lane == 2, wlo, jnp.where(lane == 3, whi, 0.0))))


def _router(xa, xb, mod3, w_r, b_r, rows_per_seg_a, tl=512):
    ra, d = xa.shape
    rb = 0 if xb is None else xb.shape[0]
    xb = xa if xb is None else xb
    r = ra + rb
    na = ra // tl
    tps = rows_per_seg_a // tl
    nseg = mod3.shape[0]
    wh, wl = _split_bf16(w_r)
    return pl.pallas_call(
        functools.partial(_router_kernel, na=na),
        out_shape=(jax.ShapeDtypeStruct((r, d), BF16),
                   jax.ShapeDtypeStruct((r, LANES), F32),
                   jax.ShapeDtypeStruct((1, LANES), F32)),
        grid=(r // tl,),
        in_specs=[pl.BlockSpec((tl, d), lambda i: (jnp.minimum(i, na - 1), 0)),
                  pl.BlockSpec((tl, d), lambda i: (jnp.maximum(i - na, 0), 0)),
                  pl.BlockSpec((1, 3, d), lambda i: (jnp.minimum(i // tps, nseg - 1), 0, 0)),
                  pl.BlockSpec((d, LANES), lambda i: (0, 0)),
                  pl.BlockSpec((d, LANES), lambda i: (0, 0)),
                  pl.BlockSpec((1, LANES), lambda i: (0, 0))],
        out_specs=(pl.BlockSpec((tl, d), lambda i: (i, 0)),
                   pl.BlockSpec((tl, LANES), lambda i: (i, 0)),
                   pl.BlockSpec((1, LANES), lambda i: (0, 0))),
        scratch_shapes=[pltpu.VMEM((1, LANES), F32)],
        compiler_params=_cparams(("arbitrary",)),
        name="moe_router",
    )(xa, xb, mod3, wh, wl, b_r)


def _moe_kernel(elo_ref, ehi_ref, val_ref, x_ref, wt_ref, wgl, wul, wdl, wgh, wuh, wdh, o_ref):
    t = pl.program_id(0)

    @pl.when(val_ref[t] != 0)
    def _():
        x = x_ref[...]
        wt = wt_ref[...]
        w16 = lambda ref: ref[0].astype(BF16)
        gl, ul, gh, uh = _dot(x, w16(wgl)), _dot(x, w16(wul)), _dot(x, w16(wgh)), _dot(x, w16(wuh))
        hl = (_silu(gl) * ul) * wt[:, 0:1]
        hh = (_silu(gh) * uh) * wt[:, 1:2]
        o_ref[...] = (_dot(hl.astype(BF16), w16(wdl)) + _dot(hh.astype(BF16), w16(wdh))).astype(o_ref.dtype)

    @pl.when(val_ref[t] == 0)
    def _():
        o_ref[...] = jnp.zeros_like(o_ref)


def _moe_experts(xs, wts, tile_elo, tile_ehi, tile_val, w_gate, w_up, w_down):
    p, d = xs.shape
    ff = w_gate.shape[2]
    nt = p // MOE_TM
    lo = lambda t, elo, ehi, val: (elo[t], 0, 0)
    hi = lambda t, elo, ehi, val: (ehi[t], 0, 0)
    row = lambda t, elo, ehi, val: (t, 0)
    grid_spec = pltpu.PrefetchScalarGridSpec(
        num_scalar_prefetch=3,
        grid=(nt,),
        in_specs=[pl.BlockSpec((MOE_TM, d), row),
                  pl.BlockSpec((MOE_TM, 2), row),
                  pl.BlockSpec((1, d, ff), lo), pl.BlockSpec((1, d, ff), lo), pl.BlockSpec((1, ff, d), lo),
                  pl.BlockSpec((1, d, ff), hi), pl.BlockSpec((1, d, ff), hi), pl.BlockSpec((1, ff, d), hi)],
        out_specs=pl.BlockSpec((MOE_TM, d), row),
    )
    return pl.pallas_call(
        _moe_kernel,
        out_shape=jax.ShapeDtypeStruct((p, d), BF16),
        grid_spec=grid_spec,
        compiler_params=_cparams(("arbitrary",)),
        name="moe_experts",
    )(tile_elo, tile_ehi, tile_val, xs, wts, w_gate, w_up, w_down, w_gate, w_up, w_down)


def _gated_add_kernel(x_ref, y_ref, g_ref, n_ref, o_ref, *, norm):
    x = x_ref[...] + g_ref[0] * y_ref[...].astype(F32)
    if norm:
        x = (x * lax.rsqrt(jnp.mean(x * x, axis=-1, keepdims=True) + NORM_EPS)) * n_ref[...]
    o_ref[...] = x


def _gated_add(x, y, gate, rows_per_seg, norm_g=None, tl=1024):
    r, d = x.shape
    tps = rows_per_seg // tl
    row = pl.BlockSpec((tl, d), lambda i: (i, 0))
    norm = norm_g is not None
    n_vec = norm_g.astype(F32)[None, :] if norm else jnp.ones((1, d), F32)
    return pl.pallas_call(
        functools.partial(_gated_add_kernel, norm=norm),
        out_shape=jax.ShapeDtypeStruct((r, d), F32),
        grid=(r // tl,),
        in_specs=[row, row, pl.BlockSpec((1, 1, d), lambda i: (i // tps, 0, 0)),
                  pl.BlockSpec((1, d), lambda i: (0, 0))],
        out_specs=row,
        compiler_params=_cparams(("arbitrary",)),
        name="moe_combine",
    )(x, y, gate, n_vec)


def _class_tables():
    elo = np.zeros((MOE_CLASSES,), np.int32)
    ehi = np.zeros((MOE_CLASSES,), np.int32)
    for g in range(MOE_GROUPS):
        for lo in range(MOE_EPG):
            for hi in range(lo + 1, MOE_EPG):
                c = g * MOE_PAIRS + lo * (2 * MOE_EPG - 1 - lo) // 2 + (hi - lo - 1)
                elo[c] = g * MOE_EPG + lo
                ehi[c] = g * MOE_EPG + hi
    return elo, ehi


_CLASS_ELO, _CLASS_EHI = _class_tables()


def _hier_moe_residual(x, mod3, gate, rows_per_seg, w_gr, b_gr, w_er, b_er, w_gate, w_up, w_down,
                       xc=None, mod3c=None, gatec=None, norm_g=None):
    ra, d = x.shape
    r = ra + (0 if xc is None else xc.shape[0])
    w_r = jnp.zeros((d, LANES), F32).at[:, :MOE_GROUPS].set(w_gr)
    w_r = w_r.at[:, MOE_GROUPS:MOE_GROUPS + MOE_GROUPS * MOE_EPG].set(w_er)
    b_r = jnp.zeros((1, LANES), F32).at[0, :MOE_GROUPS].set(b_gr)
    b_r = b_r.at[0, MOE_GROUPS:MOE_GROUPS + MOE_GROUPS * MOE_EPG].set(b_er.reshape(-1))
    mods = mod3 if xc is None else jnp.concatenate([mod3, mod3c], axis=0)
    h, route, cnt = _router(x, xc, mods, w_r, b_r, rows_per_seg)

    cls = route[:, 0:1].astype(jnp.int32)
    rank = route[:, 1].astype(jnp.int32)
    counts = cnt[0].astype(jnp.int32)
    pc = ((counts + MOE_TM - 1) // MOE_TM) * MOE_TM
    pend = jnp.cumsum(pc)[:MOE_CLASSES]
    pstart = jnp.cumsum(pc) - pc
    dest = rank + jnp.sum(jnp.where(cls == jnp.arange(LANES, dtype=jnp.int32)[None, :], pstart[None, :], 0), axis=1)
    p_rows = ((r + MOE_CLASSES * (MOE_TM - 1)) // MOE_TM + 1) * MOE_TM
    nt = p_rows // MOE_TM
    vals = jnp.stack([jnp.arange(r, dtype=F32), route[:, 2], route[:, 3]], axis=1)
    pad_src = (jnp.arange(p_rows, dtype=jnp.int32) % r).astype(F32)
    tab0 = jnp.stack([pad_src, jnp.zeros((p_rows,), F32), jnp.zeros((p_rows,), F32)], axis=1)
    tab = tab0.at[dest].set(vals, unique_indices=True, mode="promise_in_bounds")
    src = tab[:, 0].astype(jnp.int32)
    wts = tab[:, 1:3]
    tile_p0 = jnp.arange(nt, dtype=jnp.int32) * MOE_TM
    tile_val = (tile_p0 < pend[-1]).astype(jnp.int32)
    first_row = jnp.minimum(tile_p0, pend[-1] - 1)
    tile_cls = jnp.sum((pend[None, :] <= first_row[:, None]).astype(jnp.int32), axis=1)
    tile_cls = jnp.clip(tile_cls, 0, MOE_CLASSES - 1)
    tile_elo = jnp.asarray(_CLASS_ELO)[tile_cls]
    tile_ehi = jnp.asarray(_CLASS_EHI)[tile_cls]

    xs = h.at[src].get(mode="promise_in_bounds")
    ys = _moe_experts(xs, wts, tile_elo, tile_ehi, tile_val, w_gate, w_up, w_down)
    y = ys.at[dest].get(mode="promise_in_bounds", unique_indices=True)
    x_new = _gated_add(x, y, gate, rows_per_seg, norm_g=norm_g)
    if xc is None:
        return x_new
    return x_new, xc + gatec[0] * y[ra:]


GDN_HB = 4
GDN_NB = 4


def _seq_conv3(z, w, row):
    n = z.shape[0]
    zp = jnp.where(row == 0, 0.0, pltpu.roll(z, 1, 0))
    zn = jnp.where(row == n - 1, 0.0, pltpu.roll(z, n - 1, 0))
    return w[0:1] * zp + w[1:2] * z + w[2:3] * zn


def _gdn_proj_kernel(x_ref, m_ref, w_ref, wab_ref, cw_ref, cos_ref, sin_ref, al_ref, dt_ref,
                     o_ref, g_ref, h_s, *, rope):
    j = pl.program_id(1)
    n = x_ref.shape[1]
    dk = GDN_DK

    @pl.when(j == 0)
    def _():
        h_s[...] = _rms_mod(x_ref[0], m_ref[0]).astype(BF16)

    row = lax.broadcasted_iota(jnp.int32, (n, 1), 0)

    def conv_silu():
        return _silu(_seq_conv3(_dot_rows(h_s, w_ref[...]), cw_ref[...], row))

    @pl.when(j < 4)
    def _():
        z = conv_silu()
        qscale = jnp.where(j < 2, dk ** -0.5, 1.0).astype(F32)
        for hh in range(z.shape[1] // dk):
            zh = z[:, hh * dk:(hh + 1) * dk]
            t = (zh * lax.rsqrt(jnp.sum(zh * zh, axis=-1, keepdims=True) + NORM_EPS)) * qscale
            if rope:
                t = t * cos_ref[...] + pltpu.roll(t, dk // 2, 1) * sin_ref[...]
            o_ref[0, :, hh * dk:(hh + 1) * dk] = t.astype(o_ref.dtype)

    @pl.when((j >= 4) & (j < 6))
    def _():
        o_ref[0] = conv_silu().astype(o_ref.dtype)

    @pl.when((j >= 6) & (j < 8))
    def _():
        o_ref[0] = _dot_rows(h_s, w_ref[...]).astype(o_ref.dtype)

    @pl.when(j == 8)
    def _():
        ab = _dot_rows(h_s, wab_ref[...])
        lane = lax.broadcasted_iota(jnp.int32, ab.shape, 1) % LANES
        xs = ab + dt_ref[...]
        softplus = jnp.maximum(xs, 0.0) + jnp.log(1.0 + jnp.exp(-jnp.abs(xs)))
        g = -jnp.exp(al_ref[...]) * softplus
        beta = 1.0 / (1.0 + jnp.exp(-ab))
        g_ref[0] = jnp.where(lane < 2 * GDN_HB, g, jnp.where(lane < 4 * GDN_HB, beta, 0.0))


def _gdn_proj(x3, mod3, w_main, w_ab, conv_w, cosf, sinf, a_log_l, dt_l, rope):
    bn, n, d = x3.shape
    segs = mod3.shape[0]
    tc = 512
    nq = w_main.shape[1] // tc
    gl = w_ab.shape[1]
    mi = (lambda b, j: (b, 0, 0)) if segs > 1 else (lambda b, j: (0, 0, 0))
    const = lambda b, j: (0, 0)
    return pl.pallas_call(
        functools.partial(_gdn_proj_kernel, rope=rope),
        out_shape=(jax.ShapeDtypeStruct((bn, n, w_main.shape[1]), BF16),
                   jax.ShapeDtypeStruct((bn, n, gl), F32)),
        grid=(bn, nq + 1),
        in_specs=[pl.BlockSpec((1, n, d), lambda b, j: (b, 0, 0)),
                  pl.BlockSpec((1, 3, d), mi),
                  pl.BlockSpec((d, tc), lambda b, j: (0, jnp.minimum(j, nq - 1))),
                  pl.BlockSpec((d, gl), const),
                  pl.BlockSpec((3, tc), lambda b, j: (0, jnp.minimum(j, 5))),
                  pl.BlockSpec((n, GDN_DK), const), pl.BlockSpec((n, GDN_DK), const),
                  pl.BlockSpec((1, gl), const), pl.BlockSpec((1, gl), const)],
        out_specs=(pl.BlockSpec((1, n, tc), lambda b, j: (b, 0, jnp.minimum(j, nq - 1))),
                   pl.BlockSpec((1, n, gl), lambda b, j: (b, 0, 0))),
        scratch_shapes=[pltpu.VMEM((n, d), BF16)],
        compiler_params=_cparams(("arbitrary", "arbitrary")),
        name="gdn_proj",
    )(x3, mod3, w_main, w_ab, conv_w, cosf, sinf, a_log_l, dt_l)


def _gdn_scan_kernel(ql, kl, vl, gl, qc, kc, vc, gx, ol, oc, s_ref, u_s, wq_s, at_s, kd_s, la_s, *, hb):
    cs = GDN_CHUNK
    dk = GDN_DK
    ns = 2 * hb
    n_lat = ql.shape[1]
    n_ctx = qc.shape[1]
    s_ref[...] = jnp.zeros_like(s_ref)
    ol[...] = jnp.zeros_like(ol)
    oc[...] = jnp.zeros_like(oc)
    c2 = 2 * cs
    ii = lax.broadcasted_iota(jnp.int32, (c2, c2), 0)
    jj = lax.broadcasted_iota(jnp.int32, (c2, c2), 1)
    same = (ii // cs) == (jj // cs)
    eye = (ii == jj)[None]
    lower = (same & (ii >= jj))[None]
    upper = (same & (ii <= jj))[None]
    eye_f = eye.astype(F32)
    first_chunk = lax.broadcasted_iota(jnp.int32, (1, c2, 1), 1) < cs

    def run_window(qr, kr, vr, gr, orf, t0f, t0b, nb):
        npair = nb // 2
        bp = ns * npair
        bi = lax.broadcasted_iota(jnp.int32, (bp, c2, c2), 0)
        i3 = lax.broadcasted_iota(jnp.int32, (bp, c2, c2), 1)
        j3 = lax.broadcasted_iota(jnp.int32, (bp, c2, c2), 2)
        dij = (1 - 2 * ((bi // npair) % 2)) * (i3 - j3)
        same3 = (i3 // cs) == (j3 // cs)
        tri = same3 & (dij >= 0)
        stri = same3 & (dij > 0)

        def gates(t0, d):
            slab = gr[0, pl.ds(t0, nb * cs), :].reshape(npair, c2, LANES)
            g1 = slab.astype(BF16)
            r1 = slab - g1.astype(F32)
            g2 = r1.astype(BF16)
            g3 = (r1 - g2.astype(F32)).astype(BF16)
            t = jnp.broadcast_to((lower if d == 0 else upper).astype(BF16), (npair, c2, c2))
            gcum = _bdot(t, g1) + (_bdot(t, g2) + _bdot(t, g3))
            tot0 = jnp.sum(slab[:, :cs], axis=1, keepdims=True)
            tot1 = jnp.sum(slab[:, cs:], axis=1, keepdims=True)
            return slab, gcum, jnp.where(first_chunk, tot0, tot1)

        gate_d = (gates(t0f, 0), gates(t0b, 1))
        gc_l, be_l, gt_l, q_l, k_l, v_l = [], [], [], [], [], []
        for h in range(hb):
            for d in range(2):
                slab, gcum, gtot = gate_d[d]
                lg = d * hb + h
                lb = 2 * hb + lg
                gc_l.append(gcum[:, :, lg:lg + 1])
                be_l.append(slab[:, :, lb:lb + 1])
                gt_l.append(gtot[:, :, lg:lg + 1])
                t0 = t0f if d == 0 else t0b
                cols = slice(h * dk, (h + 1) * dk)
                q_l.append(qr[0, pl.ds(t0, nb * cs), cols].reshape(npair, c2, dk))
                k_l.append(kr[0, pl.ds(t0, nb * cs), cols].reshape(npair, c2, dk))
                v_l.append(vr[0, pl.ds(t0, nb * cs), cols].reshape(npair, c2, dk))
        gc = jnp.concatenate(gc_l, axis=0)
        be = jnp.concatenate(be_l, axis=0)
        gt = jnp.concatenate(gt_l, axis=0)
        q = jnp.concatenate(q_l, axis=0)
        k = jnp.concatenate(k_l, axis=0)
        v = jnp.concatenate(v_l, axis=0)

        gcl = gc + jnp.zeros((1, 1, c2), F32)
        gcr = jnp.sum(jnp.where(eye, gcl, 0.0), axis=1, keepdims=True)
        decay = jnp.where(tri, jnp.exp(jnp.where(tri, gcl - gcr, 0.0)), 0.0)
        kq = _bdot_nt(jnp.concatenate([k, q], axis=1), k)
        a = jnp.where(stri, kq[:, :c2] * be * decay, 0.0)
        mm = lambda l, r: _bdot(l.astype(BF16), r.astype(BF16))
        x = mm(a, a)
        p = eye_f - a
        for _ in range(4):
            px = mm(jnp.concatenate([p, x], axis=1), x)
            p = p + px[:, :c2]
            x = px[:, c2:]
        p = p + mm(p, x)
        pb = p.astype(BF16)
        eg = jnp.exp(gc)
        kf = k.astype(F32)
        uw = _bdot(pb, jnp.concatenate([(v.astype(F32) * be).astype(BF16), (kf * (be * eg)).astype(BF16)], axis=2))
        u_s[:, :nb] = uw[:, :, :dk].reshape(ns, nb, cs, dk)
        w4 = uw[:, :, dk:].astype(BF16).reshape(ns, nb, cs, dk)
        qd4 = (q.astype(F32) * eg).astype(BF16).reshape(ns, nb, cs, dk)
        wq_s[:, :nb] = jnp.concatenate([w4, qd4], axis=2)
        attn = jnp.where(tri, kq[:, c2:] * decay, 0.0).astype(BF16)
        at_s[:, :nb] = jnp.concatenate([attn[:, None, :cs, :cs], attn[:, None, cs:, cs:]],
                                       axis=1).reshape(ns, nb, cs, cs)
        kd_s[:, :nb] = (kf * jnp.exp(gt - gc)).astype(BF16).reshape(ns, nb, cs, dk)
        la = jnp.exp(gt) + jnp.zeros((1, 1, dk), F32)
        la_s[:, :nb] = jnp.concatenate([la[:, None, 0:1], la[:, None, cs:cs + 1]], axis=1).reshape(ns, nb, 1, dk)

        for j in range(nb):
            pick = lambda ref: jnp.stack([ref[sidx, j if sidx % 2 == 0 else nb - 1 - j] for sidx in range(ns)], axis=0)
            s = s_ref[...]
            sbf = s.astype(BF16)
            ws_qs = _bdot(pick(wq_s), sbf)
            v_new = pick(u_s) - ws_qs[:, :cs]
            vb = v_new.astype(BF16)
            o = ws_qs[:, cs:] + _bdot(pick(at_s), vb)
            s_ref[...] = s * pick(la_s) + _bdot_tn(pick(kd_s), vb)
            for h in range(hb):
                cols = slice(h * dk, (h + 1) * dk)
                orf[0, pl.ds(t0f + j * cs, cs), cols] += o[2 * h]
                orf[0, pl.ds(t0b + (nb - 1 - j) * cs, cs), cols] += o[2 * h + 1]

    run_window(qc, kc, vc, gx, oc, 0, 0, n_ctx // cs)
    wrows = GDN_NB * cs
    nwin = n_lat // wrows

    def body(wi, carry):
        t0f = pl.multiple_of(wi * wrows, wrows)
        t0b = pl.multiple_of((nwin - 1 - wi) * wrows, wrows)
        run_window(ql, kl, vl, gl, ol, t0f, t0b, GDN_NB)
        return carry

    lax.fori_loop(0, nwin, body, 0)


def _gdn_scan(pl_, gl_, pc_, gc_):
    bn, n_lat, _ = pl_.shape
    n_ctx = pc_.shape[1]
    hb, dk, cs = GDN_HB, GDN_DK, GDN_CHUNK
    hd = GDN_HEADS * dk
    ngrp = GDN_HEADS // hb
    wcol = hb * dk
    nbm = max(GDN_NB, n_ctx // cs)

    def sec(n, k):
        return pl.BlockSpec((1, n, wcol), lambda b, g: (b, 0, k * ngrp + g))

    gate = lambda n: pl.BlockSpec((1, n, LANES), lambda b, g: (b, 0, g))
    out = lambda n: pl.BlockSpec((1, n, wcol), lambda b, g: (b, 0, g))
    return pl.pallas_call(
        functools.partial(_gdn_scan_kernel, hb=hb),
        out_shape=(jax.ShapeDtypeStruct((bn, n_lat, hd), F32), jax.ShapeDtypeStruct((bn, n_ctx, hd), F32)),
        grid=(bn, ngrp),
        in_specs=[sec(n_lat, 0), sec(n_lat, 1), sec(n_lat, 2), gate(n_lat),
                  sec(n_ctx, 0), sec(n_ctx, 1), sec(n_ctx, 2), gate(n_ctx)],
        out_specs=(out(n_lat), out(n_ctx)),
        scratch_shapes=[pltpu.VMEM((2 * hb, dk, dk), F32),
                        pltpu.VMEM((2 * hb, nbm, cs, dk), F32),
                        pltpu.VMEM((2 * hb, nbm, 2 * cs, dk), BF16),
                        pltpu.VMEM((2 * hb, nbm, cs, cs), BF16),
                        pltpu.VMEM((2 * hb, nbm, cs, dk), BF16),
                        pltpu.VMEM((2 * hb, nbm, 1, dk), F32)],
        compiler_params=_cparams(("arbitrary", "arbitrary")),
        name="gdn_scan",
    )(pl_, pl_, pl_, gl_, pc_, pc_, pc_, gc_)


def _gdn_out_kernel(o_ref, z_ref, ng_ref, w_ref, x_ref, g_ref, y_ref):
    dk = GDN_DK
    parts = []
    for h in range(o_ref.shape[1] // dk):
        cols = slice(h * dk, (h + 1) * dk)
        oh = o_ref[:, cols]
        ms = jnp.mean(oh * oh, axis=-1, keepdims=True)
        on = (oh * lax.rsqrt(ms + NORM_EPS)) * ng_ref[...]
        parts.append((on * _silu(z_ref[:, cols].astype(F32))).astype(BF16))
    a = jnp.concatenate(parts, axis=1)
    y_ref[...] = x_ref[...] + g_ref[0] * _dot_rows(a, w_ref[...])


def _gdn_out(o, proj, norm_g, w_out, x, gate, rows_per_seg, tl=512):
    r, hd = o.shape
    d = w_out.shape[1]
    tps = rows_per_seg // tl
    return pl.pallas_call(
        _gdn_out_kernel,
        out_shape=jax.ShapeDtypeStruct((r, d), F32),
        grid=(r // tl,),
        in_specs=[pl.BlockSpec((tl, hd), lambda i: (i, 0)),
                  pl.BlockSpec((tl, hd), lambda i: (i, 3)),
                  pl.BlockSpec((1, GDN_DK), lambda i: (0, 0)),
                  pl.BlockSpec((hd, d), lambda i: (0, 0)),
                  pl.BlockSpec((tl, d), lambda i: (i, 0)),
                  pl.BlockSpec((1, 1, d), lambda i: (i // tps, 0, 0))],
        out_specs=pl.BlockSpec((tl, d), lambda i: (i, 0)),
        compiler_params=_cparams(("arbitrary",)),
        name="gdn_out",
    )(o, proj, norm_g, w_out, x, gate)


def _proj3_kernel(x_ref, m_ref, w0_ref, w1_ref, w2_ref, cw_ref, cb_ref, *rest, hyena):
    j = pl.program_id(1)
    o_refs, h_s = rest[:-1], rest[-1]
    n = x_ref.shape[1]

    @pl.when(j == 0)
    def _():
        h_s[...] = _rms_mod(x_ref[0], m_ref[0]).astype(BF16)

    row = lax.broadcasted_iota(jnp.int32, (n, 1), 0)
    z0 = _dot_rows(h_s, w0_ref[...])
    z1 = _dot_rows(h_s, w1_ref[...])
    z2 = _dot_rows(h_s, w2_ref[...])
    if hyena:
        cb = cb_ref[...]
        x0 = _seq_conv3(z0, cw_ref[0], row) + cb[0:1]
        x1 = _seq_conv3(z1, cw_ref[1], row) + cb[1:2]
        v = _seq_conv3(z2, cw_ref[2], row) + cb[2:3]
        o_refs[0][0] = (v * x1).astype(BF16)
        o_refs[1][0] = x0.astype(BF16)
    else:
        o_refs[0][0] = (z1 * _seq_conv3(z2 * z0, cw_ref[0], row)).astype(BF16)


def _proj3(x3, mod3, w_in, cw, cb, hyena, tc=256):
    bn, n, d = x3.shape
    segs = mod3.shape[0]
    nj = d // tc
    mi = (lambda b, j: (b, 0, 0)) if segs > 1 else (lambda b, j: (0, 0, 0))
    wspec = lambda k: pl.BlockSpec((d, tc), lambda b, j: (0, k * nj + j))
    seq = pl.BlockSpec((1, n, tc), lambda b, j: (b, 0, j))
    n_out = 2 if hyena else 1
    outs = pl.pallas_call(
        functools.partial(_proj3_kernel, hyena=hyena),
        out_shape=tuple(jax.ShapeDtypeStruct((bn, n, d), BF16) for _ in range(n_out)),
        grid=(bn, nj),
        in_specs=[pl.BlockSpec((1, n, d), lambda b, j: (b, 0, 0)),
                  pl.BlockSpec((1, 3, d), mi),
                  wspec(0), wspec(1), wspec(2),
                  pl.BlockSpec((cw.shape[0], 3, tc), lambda b, j: (0, 0, j)),
                  pl.BlockSpec((3, tc), lambda b, j: (0, j))],
        out_specs=tuple(seq for _ in range(n_out)),
        scratch_shapes=[pltpu.VMEM((n, d), BF16)],
        compiler_params=_cparams(("arbitrary", "arbitrary")),
        name="hyena_proj" if hyena else "shortconv_proj",
    )(x3, mod3, w_in, w_in, w_in, cw, cb)
    return outs


def _final_norm_kernel(x_ref, g_ref, o_ref):
    x = x_ref[...]
    ms = jnp.mean(x * x, axis=-1, keepdims=True)
    o_ref[...] = (x * lax.rsqrt(ms + NORM_EPS)) * g_ref[...]


def _final_norm(x, g, tl=1024):
    r, d = x.shape
    return pl.pallas_call(
        _final_norm_kernel,
        out_shape=jax.ShapeDtypeStruct((r, d), F32),
        grid=(r // tl,),
        in_specs=[pl.BlockSpec((tl, d), lambda i: (i, 0)), pl.BlockSpec((1, d), lambda i: (0, 0))],
        out_specs=pl.BlockSpec((tl, d), lambda i: (i, 0)),
        compiler_params=_cparams(("arbitrary",)),
        name="final_norm",
    )(x, g)


def _hyena_kernel(u_ref, x0_ref, c_ref, s_ref, kr_ref, ki_ref, kn_ref, fb_ref, o_ref):
    ub = u_ref[0]
    l = ub.shape[0]
    mb = min(l, MM_ROWS)

    def rowblocks(fn):
        return jnp.concatenate([fn(slice(i, i + mb)) for i in range(0, l, mb)], axis=0)

    a = rowblocks(lambda r: _dot(c_ref[r, :], ub))
    b = rowblocks(lambda r: _dot(s_ref[r, :], ub))
    kr = kr_ref[...]
    ki = ki_ref[...]
    zr = (a * kr + b * ki).astype(BF16)
    zi = (b * kr - a * ki).astype(BF16)
    y = rowblocks(lambda r: _dot(c_ref[r, :], zr) + _dot(s_ref[r, :], zi))
    uf = ub.astype(F32)
    tpar = lax.broadcasted_iota(jnp.int32, (l, 1), 0) & 1
    alt = (1 - 2 * tpar).astype(F32)
    un = jnp.sum(uf * alt, axis=0, keepdims=True)
    y = y + alt * (un * kn_ref[...]) + uf * fb_ref[...]
    o_ref[0] = (y * x0_ref[0].astype(F32)).astype(o_ref.dtype)


def _hyena_conv(u, x0, cmat, smat, kr, ki, kn, fbias, td=256):
    b, l, d = u.shape
    seq = pl.BlockSpec((1, l, td), lambda i, j: (i, 0, j))
    mat = pl.BlockSpec((l, l), lambda i, j: (0, 0), pipeline_mode=pl.Buffered(1))
    spec = pl.BlockSpec((l, td), lambda i, j: (0, j))
    vec = pl.BlockSpec((1, td), lambda i, j: (0, j))
    return pl.pallas_call(
        _hyena_kernel,
        out_shape=jax.ShapeDtypeStruct((b, l, d), BF16),
        grid=(b, d // td),
        in_specs=[seq, seq, mat, mat, spec, spec, vec, vec],
        out_specs=seq,
        compiler_params=_cparams(("arbitrary", "arbitrary")),
        name="hyena_conv",
    )(u, x0, cmat, smat, kr, ki, kn, fbias)


def _mm3_kernel(a_ref, b_ref, o_ref):
    ah, al = _split_bf16(a_ref[...])
    bh, bl = _split_bf16(b_ref[...])
    o_ref[...] = _dot(ah, bh) + (_dot(ah, bl) + _dot(al, bh))


def _mm3(a, b, tm=256, tn=256):
    m, k = a.shape
    n = b.shape[1]
    tm = min(tm, m)
    return pl.pallas_call(
        _mm3_kernel,
        out_shape=jax.ShapeDtypeStruct((m, n), F32),
        grid=(n // tn, m // tm),
        in_specs=[pl.BlockSpec((tm, k), lambda j, i: (i, 0)),
                  pl.BlockSpec((k, tn), lambda j, i: (0, j))],
        out_specs=pl.BlockSpec((tm, tn), lambda j, i: (i, j)),
        compiler_params=_cparams(("arbitrary", "arbitrary")),
        name="matmul_f32x3",
    )(a, b)


def _na_kernel(q_ref, k_ref, v_ref, kc_ref, vc_ref, bias_ref, o_ref, *, rows):
    r = pl.program_id(1)
    rs = jnp.clip(r - NA_WIN_R // 2, 0, rows - NA_WIN_R)
    t0 = pl.multiple_of(rs * GRID_W, GRID_W)
    nk = NA_WIN_R * GRID_W
    scale = NA_DH ** -0.5
    lane = lax.broadcasted_iota(jnp.int32, (GRID_W, LANES), 1)
    first = lane < NA_DH
    nq = GRID_W
    scores = []
    for hp in range(NA_HEADS // 2):
        cols = slice(hp * LANES, (hp + 1) * LANES)
        q2 = q_ref[0, :, cols] * scale
        zero = jnp.zeros_like(q2)
        qm = jnp.concatenate([jnp.where(first, q2, zero), jnp.where(first, zero, q2)], axis=0)
        bias = bias_ref[0, 2 * hp:2 * hp + 2].reshape(2 * nq, nk)
        s_loc = _dot_nt(qm, k_ref[0, pl.ds(t0, nk), cols]) + bias
        s_ctx = _dot_nt(qm, kc_ref[0, :, cols])
        scores.append((s_loc, s_ctx))
    probs = []
    for s_loc, s_ctx in scores:
        m = jnp.maximum(jnp.max(s_loc, axis=1, keepdims=True), jnp.max(s_ctx, axis=1, keepdims=True))
        p_loc = jnp.exp(s_loc - m)
        p_ctx = jnp.exp(s_ctx - m)
        den = jnp.sum(p_loc, axis=1, keepdims=True) + jnp.sum(p_ctx, axis=1, keepdims=True)
        probs.append((p_loc.astype(BF16), p_ctx.astype(BF16), den))
    pairs = []
    for hp, (p_loc, p_ctx, den) in enumerate(probs):
        cols = slice(hp * LANES, (hp + 1) * LANES)
        o = (_dot(p_loc, v_ref[0, pl.ds(t0, nk), cols]) + _dot(p_ctx, vc_ref[0, :, cols])) / den
        pairs.append(jnp.where(first, o[:nq], o[nq:]).astype(o_ref.dtype))
    o_ref[0] = jnp.concatenate(pairs, axis=1)


def _na_attention(q, k, v, kc, vc, bias_tab):
    b, l, hd = q.shape
    lc = kc.shape[1]
    rows = l // GRID_W
    half = NA_WIN_R // 2
    return pl.pallas_call(
        functools.partial(_na_kernel, rows=rows),
        out_shape=jax.ShapeDtypeStruct((b, l, hd), BF16),
        grid=(b, rows),
        in_specs=[pl.BlockSpec((1, GRID_W, hd), lambda i, r: (i, r, 0)),
                  pl.BlockSpec((1, l, hd), lambda i, r: (i, 0, 0)),
                  pl.BlockSpec((1, l, hd), lambda i, r: (i, 0, 0)),
                  pl.BlockSpec((1, lc, hd), lambda i, r: (i, 0, 0)),
                  pl.BlockSpec((1, lc, hd), lambda i, r: (i, 0, 0)),
                  pl.BlockSpec((1, NA_HEADS, GRID_W, NA_WIN_R * GRID_W),
                               lambda i, r: (r - jnp.clip(r - half, 0, rows - NA_WIN_R), 0, 0, 0))],
        out_specs=pl.BlockSpec((1, GRID_W, hd), lambda i, r: (i, r, 0)),
        compiler_params=_cparams(("arbitrary", "arbitrary")),
        name="na_attention",
    )(q, k, v, kc, vc, bias_tab)


def _na_bias_table(rpb):
    qc = np.arange(GRID_W)
    kc = np.arange(GRID_W)
    cstart = np.clip(qc - NA_WIN_C // 2, 0, GRID_W - NA_WIN_C)
    valid = (kc[None, :] >= cstart[:, None]) & (kc[None, :] < cstart[:, None] + NA_WIN_C)
    dc = np.clip(kc[None, :] - qc[:, None] + NA_WIN_C - 1, 0, 2 * NA_WIN_C - 2)
    off = np.arange(NA_WIN_R)
    j = np.arange(NA_WIN_R)
    dr = j[None, :] - off[:, None] + NA_WIN_R - 1
    sel_r = (dr[:, :, None] == np.arange(2 * NA_WIN_R - 1)).astype(np.float32)
    sel_c = (dc[:, :, None] == np.arange(2 * NA_WIN_C - 1)).astype(np.float32)
    t = jnp.einsum("ojr,hrc,qkc->ohqjk", sel_r, rpb.astype(F32), sel_c, precision=lax.Precision.HIGHEST)
    t = jnp.where(jnp.asarray(valid)[None, None, :, None, :], t, jnp.float32(-1e30))
    return t.reshape(NA_WIN_R, NA_HEADS, GRID_W, NA_WIN_R * GRID_W)


def _rope_tables(n_tok, dh):
    pos = jnp.arange(n_tok)
    row = (pos // GRID_W).astype(F32)
    col = (pos % GRID_W).astype(F32)
    n_freq = dh // 4
    inv = ROPE_BASE ** (-jnp.arange(n_freq, dtype=F32) / n_freq)
    ang = jnp.concatenate([row[:, None] * inv, col[:, None] * inv], axis=-1)
    cos, sin = jnp.cos(ang), jnp.sin(ang)
    return jnp.concatenate([cos, cos], axis=-1), jnp.concatenate([-sin, sin], axis=-1)


def _gdn_gate_lanes():
    ngrp = GDN_HEADS // GDN_HB
    src = -np.ones((ngrp * LANES,), np.int64)
    for grp in range(ngrp):
        for t in range(2):
            for dr in range(2):
                for hh in range(GDN_HB):
                    lane = grp * LANES + t * 2 * GDN_HB + dr * GDN_HB + hh
                    src[lane] = t * 2 * GDN_HEADS + dr * GDN_HEADS + grp * GDN_HB + hh
    return src


_GDN_LANE_SRC = _gdn_gate_lanes()


def _gdn_layer(x, xc, mod_l, mod_c, gate_l, gate_c, bn, w_in, conv_w, a_log, dt_bias, norm_g, w_out, ctx_out):
    d = D_MODEL
    hd = GDN_HEADS * GDN_DK
    n_lat = x.shape[0] // bn
    n_ctx = xc.shape[0] // bn
    w_main = w_in[:, :4 * hd].astype(BF16)
    used = jnp.asarray(_GDN_LANE_SRC >= 0)
    lane_src = jnp.asarray(np.maximum(_GDN_LANE_SRC, 0))
    w_ab = jnp.where(used[None, :], w_in[:, 4 * hd:][:, lane_src], 0.0).astype(BF16)
    decay_lane = used & (lane_src < 2 * GDN_HEADS)
    a_log_l = jnp.where(decay_lane, a_log.reshape(-1)[lane_src % (2 * GDN_HEADS)], 0.0)[None, :].astype(F32)
    dt_l = jnp.where(decay_lane, dt_bias.reshape(-1)[lane_src % (2 * GDN_HEADS)], 0.0)[None, :].astype(F32)
    cosf, sinf = _rope_tables(n_lat, GDN_DK)
    dummy = jnp.zeros((n_ctx, GDN_DK), F32)
    p_l, g_l = _gdn_proj(x.reshape(bn, n_lat, d), mod_l, w_main, w_ab, conv_w, cosf, sinf, a_log_l, dt_l, True)
    p_c, g_c = _gdn_proj(xc.reshape(bn, n_ctx, d), mod_c, w_main, w_ab, conv_w, dummy, dummy, a_log_l, dt_l, False)
    o_l, o_c = _gdn_scan(p_l, g_l, p_c, g_c)
    wo = w_out.astype(BF16)
    ng = norm_g.astype(F32)[None, :]
    x_new = _gdn_out(o_l.reshape(bn * n_lat, hd), p_l.reshape(bn * n_lat, 4 * hd), ng, wo, x, gate_l, n_lat)
    xc_new = None
    if ctx_out:
        xc_new = _gdn_out(o_c.reshape(bn * n_ctx, hd), p_c.reshape(bn * n_ctx, 4 * hd), ng, wo, xc, gate_c,
                          xc.shape[0])
    return x_new, xc_new


def _hyena_filter_taps(n_tok, w1, b1, w2, b2, w3, b3, freq, w4):
    t = jnp.linspace(0.0, 1.0, n_tok, dtype=F32)[:, None]
    bands = (HY_EMB_DIM - 1) // 2
    wpos = 2.0 * math.pi * jnp.arange(n_tok, dtype=F32)[:, None] / n_tok
    fr = jnp.linspace(1e-4, bands - 1, bands, dtype=F32)[None, :]
    z = jnp.concatenate([t, jnp.cos(fr * wpos), -jnp.sin(fr * wpos)], axis=-1)
    freq = freq.astype(F32)
    hdn = jnp.sin(freq[0] * (z @ w1.astype(F32) + b1.astype(F32)))
    hdn = jnp.sin(freq[1] * (hdn @ w2.astype(F32) + b2.astype(F32)))
    hdn = jnp.sin(freq[2] * (hdn @ w3.astype(F32) + b3.astype(F32)))
    filt = (hdn @ w4.astype(F32)).reshape(n_tok, 2, D_MODEL)
    deltas = jnp.abs(jnp.linspace(HY_MIN_DECAY, HY_MAX_DECAY, D_MODEL, dtype=F32))
    filt = filt * jnp.exp(-t * deltas)[:, None, :]
    return filt[:, 0], filt[:, 1]


def _dft_mats(l):
    k = jnp.arange(l, dtype=jnp.int32)
    ks = (k[:, None] * k[None, :]) % (2 * l)
    ang = ks.astype(F32) * (math.pi / l)
    return jnp.cos(ang), jnp.sin(ang)


def _hyena_layer(rows, mod3, gate, rows_per_seg, bn, w_in, short_w, short_b, f_w1, f_b1, f_w2, f_b2, f_w3, f_b3,
                 f_freq, f_w4, f_bias, w_out):
    d = D_MODEL
    n_tok = rows.shape[0] // bn
    cw = short_w.reshape(3, 3, d).transpose(1, 0, 2).astype(F32)
    u, x0 = _proj3(rows.reshape(bn, n_tok, d), mod3, w_in.astype(BF16), cw, short_b.reshape(3, d).astype(F32),
                   hyena=True)
    hf, hb = _hyena_filter_taps(n_tok, f_w1, f_b1, f_w2, f_b2, f_w3, f_b3, f_freq, f_w4)
    hb = hb.at[0].set(0.0)
    cmat, smat = _dft_mats(n_tok)
    n2 = 2 * n_tok
    wk = jnp.full((n_tok, 1), 2.0 / n2, F32).at[0, 0].set(1.0 / n2)
    kr = _mm3(cmat, hf + hb) * wk
    ki = _mm3(smat, hb - hf) * wk
    alt = (1.0 - 2.0 * (jnp.arange(n_tok) % 2)).astype(F32)[:, None]
    kn = jnp.sum((hf + hb) * alt, axis=0, keepdims=True) / n2
    y = _hyena_conv(u, x0, cmat.astype(BF16), smat.astype(BF16), kr, ki, kn, f_bias.astype(F32)[None, :])
    return _mm_res(y.reshape(rows.shape[0], d), w_out.astype(BF16), rows, gate, rows_per_seg)


def _na_layer(x, xc, mod_l, mod_c, gate_l, bn, w_qkv, rpb, w_out):
    hd = NA_HEADS * NA_DH
    n_lat = x.shape[0] // bn
    n_ctx = xc.shape[0] // bn
    wq = w_qkv.astype(BF16)
    z = _modmm(x, mod_l, wq, n_lat, out_dtype=BF16).reshape(bn, n_lat, 3 * hd)
    zc = _modmm(xc, mod_c, wq, xc.shape[0], out_dtype=BF16).reshape(bn, n_ctx, 3 * hd)
    o = _na_attention(z[..., :hd], z[..., hd:2 * hd], z[..., 2 * hd:], zc[..., hd:2 * hd], zc[..., 2 * hd:],
                      _na_bias_table(rpb))
    return _mm_res(o.reshape(x.shape[0], hd), w_out.astype(BF16), x, gate_l, n_lat)


def _shortconv_layer(rows, mod3, gate, rows_per_seg, bn, w_in, conv_w, w_out):
    d = D_MODEL
    n_tok = rows.shape[0] // bn
    (a,) = _proj3(rows.reshape(bn, n_tok, d), mod3, w_in.astype(BF16), conv_w.astype(F32)[None],
                  jnp.zeros((3, d), F32), hyena=False)
    return _mm_res(a.reshape(rows.shape[0], d), w_out.astype(BF16), rows, gate, rows_per_seg)


def kernel(x, c, ctx, c_ctx, ln_g, w_mod, b_mod, final_g, gdn_w_in, gdn_conv_w, gdn_a_log, gdn_dt_bias, gdn_norm_g, gdn_w_out, hy_w_in, hy_short_w, hy_short_b, hy_f_w1, hy_f_b1, hy_f_w2, hy_f_b2, hy_f_w3, hy_f_b3, hy_f_freq, hy_f_w4, hy_f_bias, hy_w_out, na_w_qkv, na_rpb, na_w_out, sc_w_in, sc_conv_w, sc_w_out, moe_w_gr, moe_b_gr, moe_w_er, moe_b_er, moe_w_gate, moe_w_up, moe_w_down):
    bn, n_lat, d = x.shape
    n_ctx = ctx.shape[1]
    rows_ctx = bn * n_ctx
    pad = (-(bn + 1)) % 8
    c_all = jnp.concatenate([c, c_ctx[None, :], jnp.zeros((pad, d), F32)], axis=0)
    mods = _mod_vectors(c_all, w_mod, b_mod)
    xr = x.reshape(bn * n_lat, d)
    xcr = ctx.reshape(rows_ctx, d)
    for i in range(DEPTH):
        m, s = i % N_MIXERS, i // N_MIXERS
        reads_ctx = m in CTX_READING_MIXERS
        upd_ctx = any((j % N_MIXERS) in CTX_READING_MIXERS for j in range(i + 1, DEPTH))
        mv = mods[i].reshape(-1, 6, d)

        def mod3(rows, which, norm_g):
            sh, sc = mv[rows, 3 * which], mv[rows, 3 * which + 1]
            return jnp.stack([jnp.broadcast_to(norm_g, sh.shape), sh, sc], axis=1)

        lat = slice(0, bn)
        cx = slice(bn, bn + 1)
        mod_l1, mod_l2 = mod3(lat, 0, ln_g[i, 0]), mod3(lat, 1, ln_g[i, 1])
        mod_c1, mod_c2 = mod3(cx, 0, ln_g[i, 0]), mod3(cx, 1, ln_g[i, 1])
        g1_l, g2_l = mv[lat, 2][:, None, :], mv[lat, 5][:, None, :]
        g1_c, g2_c = mv[cx, 2][:, None, :], mv[cx, 5][:, None, :]
        xc_new = None
        if m == 0:
            xr, xc_new = _gdn_layer(xr, xcr, mod_l1, mod_c1, g1_l, g1_c, bn, gdn_w_in[s], gdn_conv_w[s],
                                    gdn_a_log[s], gdn_dt_bias[s], gdn_norm_g[s], gdn_w_out[s], upd_ctx)
        elif m == 1:
            hy = (hy_w_in[s], hy_short_w[s], hy_short_b[s], hy_f_w1[s], hy_f_b1[s], hy_f_w2[s], hy_f_b2[s],
                  hy_f_w3[s], hy_f_b3[s], hy_f_freq[s], hy_f_w4[s], hy_f_bias[s], hy_w_out[s])
            xr = _hyena_layer(xr, mod_l1, g1_l, n_lat, bn, *hy)
            if upd_ctx:
                xc_new = _hyena_layer(xcr, mod_c1, g1_c, rows_ctx, bn, *hy)
        elif m == 2:
            xr = _na_layer(xr, xcr, mod_l1, mod_c1, g1_l, bn, na_w_qkv[s], na_rpb[s], na_w_out[s])
            assert not upd_ctx
        else:
            xr = _shortconv_layer(xr, mod_l1, g1_l, n_lat, bn, sc_w_in[s], sc_conv_w[s], sc_w_out[s])
            if upd_ctx:
                xc_new = _shortconv_layer(xcr, mod_c1, g1_c, rows_ctx, bn, sc_w_in[s], sc_conv_w[s], sc_w_out[s])
        moe = (moe_w_gr[i], moe_b_gr[i], moe_w_er[i], moe_b_er[i], moe_w_gate[i], moe_w_up[i], moe_w_down[i])
        if upd_ctx:
            xr, xcr = _hier_moe_residual(xr, mod_l2, g2_l, n_lat, *moe, xc=xc_new, mod3c=mod_c2, gatec=g2_c)
        else:
            xr = _hier_moe_residual(xr, mod_l2, g2_l, n_lat, *moe, norm_g=final_g if i == DEPTH - 1 else None)
    if upd_ctx:
        xr = _final_norm(xr, final_g.astype(F32)[None, :])
    return xr.reshape(bn, n_lat, d)
```
